```python
import math
import jax, jax.numpy as jnp
from jax import lax
import numpy as np

D_MODEL = 1024
BATCH = 4
SEQ = 4096
DEPTH = 2

HEAD_DIM = 64
BRANCH_DIM = 512
CONV_CH = BRANCH_DIM
CONV_K = 3
ATT_HEADS = BRANCH_DIM // HEAD_DIM
ATT_DIM = ATT_HEADS * HEAD_DIM
RWKV_HEADS = BRANCH_DIM // HEAD_DIM
RWKV_DIM = RWKV_HEADS * HEAD_DIM
DECAY_LORA = 64
AAA_LORA = 64
GATE_LORA = 128
N_BRANCH = 3
MIX_DIM = CONV_CH + ATT_DIM + RWKV_DIM
MOBA_BLOCK = 256
MOBA_TOPK = 3
MOBA_Q_CHUNK = 32
ROPE_THETA = 10000.0
D_FF = 2816
N_EXPERTS = 8
EXPERT_TOPK = 2
MOE_ROWS = 256
LN_EPS = 1e-5
GN_EPS = 64e-5
DN_ALPHA = (2 * DEPTH) ** 0.25
DN_BETA = (8 * DEPTH) ** -0.25
N_DENSE = (DEPTH + 1) // 2
N_MOE = DEPTH // 2
IN_SPLITS = [CONV_CH] * 3 + [ATT_DIM] * 3 + [RWKV_DIM] * 3 + [DECAY_LORA, AAA_LORA, GATE_LORA] + [D_MODEL] * N_BRANCH
P_TOTAL = sum(IN_SPLITS)
SHIFT_DIM = 3 * RWKV_DIM + DECAY_LORA + AAA_LORA + GATE_LORA

kernel_name = "hybrid_conv_moba_rwkv7_deepnorm_moe"


def split_cols(a, sizes):
    idx = np.cumsum(sizes)[:-1].tolist()
    return jnp.split(a, idx, axis=-1)


def layer_norm(x, g, b):
    xf = x.astype(jnp.float32)
    mu = xf.mean(-1, keepdims=True)
    var = jnp.square(xf - mu).mean(-1, keepdims=True)
    return ((xf - mu) * lax.rsqrt(var + LN_EPS) * g + b).astype(x.dtype)


def token_shift(u):
    return jnp.pad(u, ((0, 0), (1, 0), (0, 0)))[:, :-1]


def rope(x, pos):
    half = HEAD_DIM // 2
    inv = ROPE_THETA ** (-jnp.arange(half, dtype=jnp.float32) / half)
    ang = pos.astype(jnp.float32)[:, None] * inv[None, :]
    cos = jnp.cos(ang)[None, :, None, :]
    sin = jnp.sin(ang)[None, :, None, :]
    x1 = x[..., :half].astype(jnp.float32)
    x2 = x[..., half:].astype(jnp.float32)
    return jnp.concatenate([x1 * cos - x2 * sin, x2 * cos + x1 * sin], -1).astype(x.dtype)


def short_conv(u, w):
    return lax.conv_general_dilated(
        u, w[:, None, :], window_strides=(1,), padding=((CONV_K - 1, 0),),
        dimension_numbers=('NWC', 'WIO', 'NWC'), feature_group_count=u.shape[-1])


def moba_attention(q, k, v):
    B, T, H, Dh = q.shape
    nb = -(-T // MOBA_BLOCK)
    tp = nb * MOBA_BLOCK
    pad = ((0, 0), (0, tp - T), (0, 0), (0, 0))
    q = jnp.pad(q, pad).transpose(0, 2, 1, 3)
    k = jnp.pad(k, pad).transpose(0, 2, 1, 3)
    v = jnp.pad(v, pad).transpose(0, 2, 1, 3)
    kb = k.reshape(B, H, nb, MOBA_BLOCK, Dh)
    vb = v.reshape(B, H, nb, MOBA_BLOCK, Dh)
    kmean = kb.astype(jnp.float32).mean(3)
    topk = min(MOBA_TOPK, nb)
    scale = Dh ** -0.5
    blk_ids = jnp.arange(nb)
    gather = jax.vmap(jax.vmap(lambda blocks, idx: blocks[idx]))

    def chunk(c):
        q0 = c * MOBA_Q_CHUNK
        qc = lax.dynamic_slice_in_dim(q, q0, MOBA_Q_CHUNK, axis=2)
        own = q0 // MOBA_BLOCK
        qpos = q0 + jnp.arange(MOBA_Q_CHUNK)
        gate = jnp.einsum('bhqd,bhnd->bhqn', qc.astype(jnp.float32), kmean)
        gate = jnp.where(blk_ids < own, gate, -jnp.inf)
        _, sel = lax.top_k(gate, topk)
        sel_ok = sel < own
        k_sel = gather(kb, sel)
        v_sel = gather(vb, sel)
        s_sel = jnp.einsum('bhqd,bhqnkd->bhqnk', qc, k_sel).astype(jnp.float32) * scale
        s_sel = jnp.where(sel_ok[..., None], s_sel, -jnp.inf)
        s_sel = s_sel.reshape(B, H, MOBA_Q_CHUNK, topk * MOBA_BLOCK)
        k_own = lax.dynamic_index_in_dim(kb, own, axis=2, keepdims=False)
        v_own = lax.dynamic_index_in_dim(vb, own, axis=2, keepdims=False)
        s_own = jnp.einsum('bhqd,bhkd->bhqk', qc, k_own).astype(jnp.float32) * scale
        kpos = own * MOBA_BLOCK + jnp.arange(MOBA_BLOCK)
        s_own = jnp.where(kpos[None, :] <= qpos[:, None], s_own, -jnp.inf)
        p = jax.nn.softmax(jnp.concatenate([s_sel, s_own], -1), axis=-1).astype(v.dtype)
        p_sel = p[..., :topk * MOBA_BLOCK].reshape(B, H, MOBA_Q_CHUNK, topk, MOBA_BLOCK)
        p_own = p[..., topk * MOBA_BLOCK:]
        return (jnp.einsum('bhqnk,bhqnkd->bhqd', p_sel, v_sel)
                + jnp.einsum('bhqk,bhkd->bhqd', p_own, v_own))

    o = lax.map(chunk, jnp.arange(tp // MOBA_Q_CHUNK))
    o = o.transpose(1, 0, 3, 2, 4).reshape(B, tp, H, Dh)
    return o[:, :T]


def delta_rule_scan(r, w, k, v, kk, a):
    B, T, H, N = r.shape
    xs = tuple(t.transpose(1, 0, 2, 3) for t in (r, w, k, v, kk, a))

    def step(S, inp):
        r_t, w_t, k_t, v_t, kk_t, a_t = inp
        sa = jnp.einsum('bhvk,bhk->bhv', S, -kk_t)
        S = S * w_t[:, :, None, :] + sa[..., None] * (kk_t * a_t)[:, :, None, :] + v_t[..., None] * k_t[:, :, None, :]
        return S, jnp.einsum('bhvk,bhk->bhv', S, r_t)

    S0 = jnp.zeros((B, H, N, N), r.dtype)
    _, y = lax.scan(step, S0, xs)
    return y.transpose(1, 0, 2, 3)


def rwkv7_time_mix(r, k, v, wl, al, gl, shift_mu, decay_w0, decay_w2, aaa_a0, aaa_w2, gate_w2,
                   k_k, k_a, r_k, gn_g, gn_b):
    B, T, _ = r.shape
    u = jnp.concatenate([r, k, v, wl, al, gl], -1).astype(jnp.float32)
    u = u + (token_shift(u) - u) * shift_mu
    r, k, v, wl, al, gl = split_cols(u, [RWKV_DIM] * 3 + [DECAY_LORA, AAA_LORA, GATE_LORA])
    w = -jax.nn.softplus(-(decay_w0 + jnp.tanh(wl) @ decay_w2)) - 0.5
    decay = jnp.exp(-jnp.exp(w))
    a = jax.nn.sigmoid(aaa_a0 + al @ aaa_w2)
    g = jax.nn.sigmoid(gl) @ gate_w2
    heads = lambda t: t.reshape(B, T, RWKV_HEADS, HEAD_DIM)
    kk = heads(k * k_k)
    kk = kk / jnp.maximum(jnp.linalg.norm(kk, axis=-1, keepdims=True), 1e-12)
    k = k * (1.0 + (a - 1.0) * k_a)
    rh, kh, vh, ah = heads(r), heads(k), heads(v), heads(a)
    y = delta_rule_scan(rh, heads(decay), kh, vh, kk, ah)
    mu = y.mean(-1, keepdims=True)
    var = jnp.square(y - mu).mean(-1, keepdims=True)
    y = ((y - mu) * lax.rsqrt(var + GN_EPS)).reshape(B, T, RWKV_DIM) * gn_g + gn_b
    y = y + ((rh * kh * r_k).sum(-1, keepdims=True) * vh).reshape(B, T, RWKV_DIM)
    return y * g


def hybrid_mixer(x, w_in, conv_w, shift_mu, decay_w0, decay_w2, aaa_a0, aaa_w2, gate_w2,
                 k_k, k_a, r_k, gn_g, gn_b, w_branch, w_out):
    B, T, _ = x.shape
    proj = x @ w_in
    (c_h, c_b, c_c, q, k, v, rr, rk, rv, wl, al, gl,
     g_conv, g_att, g_rwkv) = split_cols(proj, IN_SPLITS)
    o_conv = c_b * short_conv(c_c * c_h, conv_w)
    pos = jnp.arange(T)
    qh = rope(q.reshape(B, T, ATT_HEADS, HEAD_DIM), pos)
    kh = rope(k.reshape(B, T, ATT_HEADS, HEAD_DIM), pos)
    o_att = moba_attention(qh, kh, v.reshape(B, T, ATT_HEADS, HEAD_DIM)).reshape(B, T, ATT_DIM)
    o_rwkv = rwkv7_time_mix(rr, rk, rv, wl, al, gl, shift_mu, decay_w0, decay_w2, aaa_a0, aaa_w2,
                            gate_w2, k_k, k_a, r_k, gn_g, gn_b).astype(x.dtype)
    p_conv, p_att, p_rwkv = jnp.split(w_branch, [CONV_CH, CONV_CH + ATT_DIM], axis=0)
    merged = (jax.nn.sigmoid(g_conv) * (o_conv @ p_conv)
              + jax.nn.sigmoid(g_att) * (o_att @ p_att)
              + jax.nn.sigmoid(g_rwkv) * (o_rwkv @ p_rwkv))
    return merged @ w_out


def swiglu(x, w_gate, w_up, w_down):
    return (jax.nn.silu(x @ w_gate) * (x @ w_up)) @ w_down


def moe_swiglu(x, router_w, w_gate, w_up, w_down):
    B, T, D = x.shape
    xt = x.reshape(-1, D)
    N = xt.shape[0]
    NK = N * EXPERT_TOPK
    logits = (xt @ router_w).astype(jnp.float32)
    top_logit, top_e = lax.top_k(logits, EXPERT_TOPK)
    gate = jax.nn.softmax(top_logit, axis=-1).astype(x.dtype)
    flat_e = top_e.reshape(-1)
    flat_tok = jnp.arange(NK, dtype=jnp.int32) // EXPERT_TOPK
    flat_gate = gate.reshape(-1)
    order = jnp.argsort(flat_e)
    se = flat_e[order]
    counts = jnp.zeros((N_EXPERTS,), jnp.int32).at[flat_e].add(1)
    padded = (counts + MOE_ROWS - 1) // MOE_ROWS * MOE_ROWS
    pad_end = jnp.cumsum(padded)
    pad_start = pad_end - padded
    start = jnp.cumsum(counts) - counts
    dest = pad_start[se] + jnp.arange(NK, dtype=jnp.int32) - start[se]
    n_blocks = -(-NK // MOE_ROWS) + N_EXPERTS
    n_rows = n_blocks * MOE_ROWS
    buf_tok = jnp.zeros((n_rows,), jnp.int32).at[dest].set(flat_tok[order])
    buf_gate = jnp.zeros((n_rows,), x.dtype).at[dest].set(flat_gate[order])
    blk_start = jnp.arange(n_blocks, dtype=jnp.int32) * MOE_ROWS
    blk_e = jnp.minimum(jnp.searchsorted(pad_end, blk_start, side='right'), N_EXPERTS - 1)
    xb = xt[buf_tok].reshape(n_blocks, MOE_ROWS, D)

    def expert_block(args):
        xr, e = args
        return swiglu(xr, w_gate[e], w_up[e], w_down[e])

    yb = lax.map(expert_block, (xb, blk_e)).reshape(n_rows, D)
    out = jnp.zeros_like(xt).at[buf_tok].add(yb * buf_gate[:, None])
    return out.reshape(B, T, D)


def setup_inputs(seed: int = 0) -> dict:
    key = jax.random.key(seed)
    ks = jax.random.split(key, 32)
    n = lambda i, shape, s: jax.random.normal(ks[i], shape, jnp.float32) * s
    L = DEPTH
    return {
        "x": n(0, (BATCH, SEQ, D_MODEL), 1.0),
        "w_in": n(1, (L, D_MODEL, P_TOTAL), D_MODEL ** -0.5),
        "conv_w": n(2, (L, CONV_K, CONV_CH), CONV_K ** -0.5),
        "shift_mu": jax.random.uniform(ks[3], (L, SHIFT_DIM), jnp.float32),
        "decay_w0": jax.random.uniform(ks[4], (L, RWKV_DIM), jnp.float32, -4.0, 0.5),
        "decay_w2": n(5, (L, DECAY_LORA, RWKV_DIM), 0.1 * DECAY_LORA ** -0.5),
        "aaa_a0": n(6, (L, RWKV_DIM), 0.1),
        "aaa_w2": n(7, (L, AAA_LORA, RWKV_DIM), AAA_LORA ** -0.5),
        "gate_w2": n(8, (L, GATE_LORA, RWKV_DIM), GATE_LORA ** -0.5),
        "k_k": 1.0 + n(9, (L, RWKV_DIM), 0.1),
        "k_a": 1.0 + n(10, (L, RWKV_DIM), 0.1),
        "r_k": n(11, (L, RWKV_HEADS, HEAD_DIM), 0.1),
        "gn_g": 1.0 + n(12, (L, RWKV_DIM), 0.1),
        "gn_b": n(13, (L, RWKV_DIM), 0.02),
        "w_branch": n(14, (L, MIX_DIM, D_MODEL), BRANCH_DIM ** -0.5),
        "w_out": n(15, (L, D_MODEL, D_MODEL), D_MODEL ** -0.5 * DN_BETA),
        "ln1_g": 1.0 + n(16, (L, D_MODEL), 0.1),
        "ln1_b": n(17, (L, D_MODEL), 0.02),
        "ln2_g": 1.0 + n(18, (L, D_MODEL), 0.1),
        "ln2_b": n(19, (L, D_MODEL), 0.02),
        "ffn_w_gate": n(20, (N_DENSE, D_MODEL, D_FF), D_MODEL ** -0.5),
        "ffn_w_up": n(21, (N_DENSE, D_MODEL, D_FF), D_MODEL ** -0.5),
        "ffn_w_down": n(22, (N_DENSE, D_FF, D_MODEL), D_FF ** -0.5 * DN_BETA),
        "router_w": n(23, (N_MOE, D_MODEL, N_EXPERTS), D_MODEL ** -0.5),
        "moe_w_gate": n(24, (N_MOE, N_EXPERTS, D_MODEL, D_FF), D_MODEL ** -0.5),
        "moe_w_up": n(25, (N_MOE, N_EXPERTS, D_MODEL, D_FF), D_MODEL ** -0.5),
        "moe_w_down": n(26, (N_MOE, N_EXPERTS, D_FF, D_MODEL), D_FF ** -0.5 * DN_BETA),
    }


def reference(x, w_in, conv_w, shift_mu, decay_w0, decay_w2, aaa_a0, aaa_w2, gate_w2, k_k, k_a, r_k,
              gn_g, gn_b, w_branch, w_out, ln1_g, ln1_b, ln2_g, ln2_b, ffn_w_gate, ffn_w_up, ffn_w_down,
              router_w, moe_w_gate, moe_w_up, moe_w_down):
    for l in range(DEPTH):
        h = hybrid_mixer(x, w_in[l], conv_w[l], shift_mu[l], decay_w0[l], decay_w2[l], aaa_a0[l],
                         aaa_w2[l], gate_w2[l], k_k[l], k_a[l], r_k[l], gn_g[l], gn_b[l],
                         w_branch[l], w_out[l])
        x = layer_norm(DN_ALPHA * x + h, ln1_g[l], ln1_b[l])
        if l % 2 == 0:
            j = l // 2
            f = swiglu(x, ffn_w_gate[j], ffn_w_up[j], ffn_w_down[j])
        else:
            j = l // 2
            f = moe_swiglu(x, router_w[j], moe_w_gate[j], moe_w_up[j], moe_w_down[j])
        x = layer_norm(DN_ALPHA * x + f, ln2_g[l], ln2_b[l])
    return x
```

```python
import functools

import jax
import jax.numpy as jnp
from jax import lax
from jax.experimental import pallas as pl
from jax.experimental.pallas import tpu as pltpu

F32 = jnp.float32
BF16 = jnp.bfloat16

D_MODEL = 1024
HEAD_DIM = 64
BRANCH_DIM = 512
N_HEADS = BRANCH_DIM // HEAD_DIM
DECAY_LORA = 64
AAA_LORA = 64
GATE_LORA = 128
LORA_DIM = DECAY_LORA + AAA_LORA + GATE_LORA
MIX_COLS = 9 * BRANCH_DIM
GATE_COLS = 3 * D_MODEL
MOBA_BLOCK = 256
MOBA_TOPK = 3
ROPE_THETA = 10000.0
D_FF = 2816
N_EXPERTS = 8
EXPERT_TOPK = 2
MOE_ROWS = 256
LN_EPS = 1e-5
GN_EPS = 64e-5
DEPTH = 2
DN_ALPHA = (2 * DEPTH) ** 0.25

VMEM_LIMIT_BYTES = 56 * 1024 * 1024
SUBLANES = 8
RWKV_TILE = 256
RWKV_CHUNK = 64
FF_CHUNK = 256
GATHER_BATCH = 256
HEAD_SHIFT = HEAD_DIM.bit_length() - 1
CHUNK_SHIFT = RWKV_CHUNK.bit_length() - 1


def _cparams(*sem):
    return pltpu.CompilerParams(dimension_semantics=sem, vmem_limit_bytes=VMEM_LIMIT_BYTES)


def _vmem_whole():
    return pl.BlockSpec(memory_space=pltpu.VMEM)


def _dot(a, b):
    return jnp.dot(a.astype(BF16), b.astype(BF16), preferred_element_type=F32)


def _dot_nt(a, b):
    return lax.dot_general(a.astype(BF16), b.astype(BF16), (((1,), (1,)), ((), ())),
                           preferred_element_type=F32)


def _dot_tn(a, b):
    return lax.dot_general(a.astype(BF16), b.astype(BF16), (((0,), (0,)), ((), ())),
                           preferred_element_type=F32)


def _dot_f32(a, b):
    return jnp.dot(a, b, preferred_element_type=F32, precision=lax.Precision.HIGHEST)


def _sigmoid(x):
    return 1.0 / (1.0 + jnp.exp(-x))


def _layer_norm(y, g, b):
    mu = jnp.mean(y, axis=-1, keepdims=True)
    d = y - mu
    var = jnp.mean(d * d, axis=-1, keepdims=True)
    return d * lax.rsqrt(var + LN_EPS) * g + b


def _shift_rows(u, halo, n):
    out = pltpu.roll(u, n, axis=0)
    row = lax.broadcasted_iota(jnp.int32, u.shape, 0)
    for r in range(n):
        src = halo[SUBLANES - n + r:SUBLANES - n + r + 1, :]
        out = jnp.where(row == r, src, out)
    return out


def _halo_index(tile_rows):
    step = tile_rows // SUBLANES

    def index(col):
        return lambda b, i: (b, jnp.maximum(i * step - 1, 0), col)
    return index


def _in_proj_kernel(x_ref, wm_ref, wl_ref, wg_ref, mix_ref, lora_ref, gate_ref):
    xb = x_ref[...].astype(BF16)
    for j in range(0, MIX_COLS, BRANCH_DIM):
        mix_ref[:, j:j + BRANCH_DIM] = jnp.dot(xb, wm_ref[:, j:j + BRANCH_DIM],
                                               preferred_element_type=F32)
    lora_ref[...] = jnp.dot(xb, wl_ref[...], preferred_element_type=F32)
    for j in range(0, GATE_COLS, BRANCH_DIM):
        g = jnp.dot(xb, wg_ref[:, j:j + BRANCH_DIM], preferred_element_type=F32)
        gate_ref[:, j:j + BRANCH_DIM] = _sigmoid(g)


def _in_proj(x2, w_mix, w_lora, w_gate, tm=256):
    n = x2.shape[0]
    row = lambda i: (i, 0)
    return pl.pallas_call(
        _in_proj_kernel,
        grid=(n // tm,),
        in_specs=[pl.BlockSpec((tm, D_MODEL), row), _vmem_whole(), _vmem_whole(), _vmem_whole()],
        out_specs=[pl.BlockSpec((tm, MIX_COLS), row), pl.BlockSpec((tm, LORA_DIM), row),
                   pl.BlockSpec((tm, GATE_COLS), row)],
        out_shape=[jax.ShapeDtypeStruct((n, MIX_COLS), F32), jax.ShapeDtypeStruct((n, LORA_DIM), F32),
                   jax.ShapeDtypeStruct((n, GATE_COLS), F32)],
        compiler_params=_cparams("parallel"),
        name="in_proj",
    )(x2, w_mix, w_lora, w_gate)


def _conv_kernel(ch_ref, cb_ref, cc_ref, hh_ref, hc_ref, w_ref, o_ref):
    first = pl.program_id(1) == 0
    u = cc_ref[0] * ch_ref[0]
    halo = jnp.where(first, 0.0, hc_ref[0] * hh_ref[0])
    w = w_ref[...]
    y = w[2:3, :] * u + w[1:2, :] * _shift_rows(u, halo, 1) + w[0:1, :] * _shift_rows(u, halo, 2)
    o_ref[0] = (cb_ref[0] * y).astype(o_ref.dtype)


def _short_conv(mix3, conv_w, tq=512):
    b, t, _ = mix3.shape
    blk = lambda col: pl.BlockSpec((1, tq, BRANCH_DIM), lambda bi, i: (bi, i, col))
    halo = _halo_index(tq)
    hblk = lambda col: pl.BlockSpec((1, SUBLANES, BRANCH_DIM), halo(col))
    return pl.pallas_call(
        _conv_kernel,
        grid=(b, t // tq),
        in_specs=[blk(0), blk(1), blk(2), hblk(0), hblk(2),
                  pl.BlockSpec((3, BRANCH_DIM), lambda bi, i: (0, 0))],
        out_specs=pl.BlockSpec((1, tq, BRANCH_DIM), lambda bi, i: (bi, i, 0)),
        out_shape=jax.ShapeDtypeStruct((b, t, BRANCH_DIM), BF16),
        compiler_params=_cparams("parallel", "parallel"),
        name="short_conv",
    )(mix3, mix3, mix3, mix3, mix3, conv_w)


def _att_prep_kernel(q_ref, k_ref, v_ref, cos_ref, sin_ref, qo_ref, ko_ref, vo_ref, km_ref):
    cos = cos_ref[...]
    sin = sin_ref[...]
    lane = lax.broadcasted_iota(jnp.int32, cos.shape, 1)
    first_half = (lane & (HEAD_DIM - 1)) < (HEAD_DIM // 2)

    def rope(x):
        swapped = jnp.where(first_half, pltpu.roll(x, BRANCH_DIM - HEAD_DIM // 2, axis=1),
                            pltpu.roll(x, HEAD_DIM // 2, axis=1))
        return x * cos + swapped * sin

    qr = rope(q_ref[0])
    kr = rope(k_ref[0])
    v = v_ref[0]
    km_ref[0, 0] = jnp.mean(kr, axis=0, keepdims=True)
    for h in range(N_HEADS):
        sl = slice(h * HEAD_DIM, (h + 1) * HEAD_DIM)
        qo_ref[0, h] = qr[:, sl]
        ko_ref[0, h] = kr[:, sl].astype(BF16)
        vo_ref[0, h] = v[:, sl].astype(BF16)


def _att_prep(mix3, cos_t, sin_t):
    b, t, _ = mix3.shape
    tq = MOBA_BLOCK
    nb = t // tq
    blk = lambda col: pl.BlockSpec((1, tq, BRANCH_DIM), lambda bi, i: (bi, i, col))
    tab = pl.BlockSpec((tq, BRANCH_DIM), lambda bi, i: (i, 0))
    head_out = pl.BlockSpec((1, N_HEADS, tq, HEAD_DIM), lambda bi, i: (bi, 0, i, 0))
    return pl.pallas_call(
        _att_prep_kernel,
        grid=(b, nb),
        in_specs=[blk(3), blk(4), blk(5), tab, tab],
        out_specs=[head_out, head_out, head_out,
                   pl.BlockSpec((1, 1, 1, BRANCH_DIM), lambda bi, i: (bi, i, 0, 0))],
        out_shape=[jax.ShapeDtypeStruct((b, N_HEADS, t, HEAD_DIM), F32),
                   jax.ShapeDtypeStruct((b, N_HEADS, t, HEAD_DIM), BF16),
                   jax.ShapeDtypeStruct((b, N_HEADS, t, HEAD_DIM), BF16),
                   jax.ShapeDtypeStruct((b, nb, 1, BRANCH_DIM), F32)],
        compiler_params=_cparams("parallel", "parallel"),
        name="att_prep",
    )(mix3, mix3, mix3, cos_t, sin_t)


def _moba_kernel(q_ref, k_ref, v_ref, km_ref, o_ref, *, nb):
    i = pl.program_id(2)
    bs = MOBA_BLOCK
    q = q_ref[0, 0]
    qb = q.astype(BF16)
    gate = lax.dot_general(q, km_ref[0, 0], (((1,), (1,)), ((), ())), preferred_element_type=F32,
                           precision=lax.Precision.HIGHEST)
    blk = lax.broadcasted_iota(jnp.int32, (bs, nb), 1)
    valid = blk < i
    gate = jnp.where(valid, gate, -jnp.inf)
    rank = jnp.zeros((bs, nb), jnp.int32)
    for m in range(nb):
        gm = gate[:, m:m + 1]
        beats = (gm > gate) | ((gm == gate) & (blk > m))
        rank = rank + beats.astype(jnp.int32)
    sel = jnp.where(valid & (rank < MOBA_TOPK), 1.0, 0.0)

    scale = HEAD_DIM ** -0.5

    def scores(kb):
        return lax.dot_general(qb, kb, (((1,), (1,)), ((), ())), preferred_element_type=F32) * scale

    own = pl.multiple_of(i * bs, bs)
    s = scores(k_ref[0, 0, pl.ds(own, bs), :])
    r_id = lax.broadcasted_iota(jnp.int32, (bs, bs), 0)
    c_id = lax.broadcasted_iota(jnp.int32, (bs, bs), 1)
    s = jnp.where(c_id <= r_id, s, -jnp.inf)
    m0 = jnp.max(s, axis=1, keepdims=True)
    p = jnp.exp(s - m0)
    l0 = jnp.sum(p, axis=1, keepdims=True)
    acc0 = jnp.dot(p.astype(BF16), v_ref[0, 0, pl.ds(own, bs), :], preferred_element_type=F32)

    def body(j, carry):
        m, l, acc = carry
        off = pl.multiple_of(j * bs, bs)
        s = scores(k_ref[0, 0, pl.ds(off, bs), :])
        picked = jnp.sum(jnp.where(blk == j, sel, 0.0), axis=1, keepdims=True) > 0.5
        s = jnp.where(picked, s, -jnp.inf)
        m_new = jnp.maximum(m, jnp.max(s, axis=1, keepdims=True))
        alpha = jnp.exp(m - m_new)
        p = jnp.exp(s - m_new)
        l = alpha * l + jnp.sum(p, axis=1, keepdims=True)
        acc = alpha * acc + jnp.dot(p.astype(BF16), v_ref[0, 0, pl.ds(off, bs), :],
                                    preferred_element_type=F32)
        return m_new, l, acc

    _, l, acc = lax.fori_loop(0, i, body, (m0, l0, acc0))
    o_ref[0, 0] = (acc / l).astype(o_ref.dtype)


def _moba(qh, kh, vh, kmean):
    b, h, t, d = qh.shape
    nb = t // MOBA_BLOCK
    seq = pl.BlockSpec((1, 1, t, d), lambda bi, hi, i: (bi, hi, 0, 0))
    qblk = pl.BlockSpec((1, 1, MOBA_BLOCK, d), lambda bi, hi, i: (bi, hi, i, 0))
    return pl.pallas_call(
        functools.partial(_moba_kernel, nb=nb),
        grid=(b, h, nb),
        in_specs=[qblk, seq, seq, pl.BlockSpec((1, 1, nb, d), lambda bi, hi, i: (bi, hi, 0, 0))],
        out_specs=qblk,
        out_shape=jax.ShapeDtypeStruct((b, h, t, d), BF16),
        compiler_params=_cparams("parallel", "parallel", "arbitrary"),
        name="moba",
    )(qh, kh, vh, kmean)


def _rwkv_kernel(r_ref, k_ref, v_ref, lo_ref, hr_ref, hk_ref, hv_ref, hl_ref,
                 mu_r_ref, mu_k_ref, mu_v_ref, mu_l_ref, w0_ref, w2_ref, a0_ref, a2_ref, g2_ref,
                 kk_ref, ka_ref, rk_ref, gng_ref, gnb_ref, o_ref, state_ref):
    i = pl.program_id(1)
    first = i == 0
    tq, ch = RWKV_TILE, RWKV_CHUNK
    hd, bd = HEAD_DIM, BRANCH_DIM

    @pl.when(first)
    def _():
        state_ref[...] = jnp.zeros_like(state_ref)

    def lerp(x_ref, h_ref, mu_ref):
        x = x_ref[0]
        halo = jnp.where(first, 0.0, h_ref[0])
        return x + (_shift_rows(x, halo, 1) - x) * mu_ref[...]

    r = lerp(r_ref, hr_ref, mu_r_ref)
    k = lerp(k_ref, hk_ref, mu_k_ref)
    v = lerp(v_ref, hv_ref, mu_v_ref)
    lo = lerp(lo_ref, hl_ref, mu_l_ref)

    z = w0_ref[...] + _dot(jnp.tanh(lo), w2_ref[...])
    ew = jnp.exp(-0.5) * _sigmoid(z)
    a = _sigmoid(a0_ref[...] + _dot(lo, a2_ref[...]))
    g = _dot(_sigmoid(lo), g2_ref[...])

    lane_r = lax.broadcasted_iota(jnp.int32, (bd, bd), 0)
    lane_c = lax.broadcasted_iota(jnp.int32, (bd, bd), 1)
    same_head = (lane_r >> HEAD_SHIFT) == (lane_c >> HEAD_SHIFT)
    head_sum = jnp.where(same_head, 1.0, 0.0)

    kk = k * kk_ref[...]
    kk = kk / jnp.maximum(jnp.sqrt(_dot_f32(kk * kk, head_sum)), 1e-12)
    k2 = k * (1.0 + (a - 1.0) * ka_ref[...])
    bonus = _dot_f32(r * k2 * rk_ref[...], head_sum) * v

    t_r = lax.broadcasted_iota(jnp.int32, (tq, tq), 0)
    t_c = lax.broadcasted_iota(jnp.int32, (tq, tq), 1)
    same_chunk = (t_r >> CHUNK_SHIFT) == (t_c >> CHUNK_SHIFT)
    cs = _dot_f32(jnp.where(same_chunk & (t_c <= t_r), 1.0, 0.0), ew)
    cs_end = _dot_f32(jnp.where(same_chunk, 1.0, 0.0), ew)

    kka = kk * a
    a_t = -kk * jnp.exp(ew - cs)
    b_hat = kka * jnp.exp(cs)
    k_hat = k2 * jnp.exp(cs)
    r_t = r * jnp.exp(-cs)
    b_e = kka * jnp.exp(cs - cs_end)
    k_e = k2 * jnp.exp(cs - cs_end)
    p_c = jnp.exp(-cs_end)

    strict = jnp.where(same_chunk & (t_c < t_r), 1.0, 0.0)
    incl = jnp.where(same_chunk & (t_c <= t_r), 1.0, 0.0)
    eye = jnp.where(t_c == t_r, 1.0, 0.0)
    lane = lax.broadcasted_iota(jnp.int32, (1, bd), 1)

    a_til = jnp.zeros((tq, bd), F32)
    u_0 = jnp.zeros((tq, bd), F32)
    r_til = jnp.zeros((tq, bd), F32)
    y_hat = jnp.zeros((tq, bd), F32)
    for h in range(N_HEADS):
        hm = (lane >> HEAD_SHIFT) == h
        a_h = jnp.where(hm, a_t, 0.0)
        r_h = jnp.where(hm, r_t, 0.0)
        l_ab = strict * _dot_nt(a_h, b_hat)
        l_ak = strict * _dot_nt(a_h, k_hat)
        m_rb = incl * _dot_nt(r_h, b_hat)
        m_rk = incl * _dot_nt(r_h, k_hat)
        t_inv = eye + l_ab
        pw = l_ab
        n = 2
        while n < ch:
            pw = _dot(pw, pw)
            t_inv = t_inv + _dot(t_inv, pw)
            n *= 2
        at_h = _dot(t_inv, a_t)
        u0_h = _dot(t_inv, _dot(l_ak, v))
        rt_h = _dot(m_rb, at_h)
        yh_h = _dot(m_rb, u0_h) + _dot(m_rk, v)
        a_til = jnp.where(hm, at_h, a_til)
        u_0 = jnp.where(hm, u0_h, u_0)
        r_til = jnp.where(hm, rt_h, r_til)
        y_hat = jnp.where(hm, yh_h, y_hat)
    r_til = r_til + r_t

    st = state_ref[...]
    ys = []
    for c in range(tq // ch):
        rows = slice(c * ch, (c + 1) * ch)
        g_c = jnp.where(same_head, _dot_tn(a_til[rows], b_e[rows]), 0.0)
        h_c = jnp.where(same_head, _dot_tn(u_0[rows], b_e[rows]) + _dot_tn(v[rows], k_e[rows]), 0.0)
        ys.append(_dot_nt(r_til[rows], st) + y_hat[rows])
        st = st * p_c[c * ch:c * ch + 1, :] + _dot(st, g_c) + h_c
    state_ref[...] = st
    y = jnp.concatenate(ys, axis=0)

    mu = _dot_f32(y, head_sum) * (1.0 / hd)
    d = y - mu
    var = _dot_f32(d * d, head_sum) * (1.0 / hd)
    y = d * lax.rsqrt(var + GN_EPS) * gng_ref[...] + gnb_ref[...]
    o_ref[0] = ((y + bonus) * g).astype(o_ref.dtype)


def _rwkv(mix3, lora3, p):
    b, t, _ = mix3.shape
    tq = RWKV_TILE
    halo = _halo_index(tq)
    blk = lambda col: pl.BlockSpec((1, tq, BRANCH_DIM), lambda bi, i: (bi, i, col))
    hblk = lambda col: pl.BlockSpec((1, SUBLANES, BRANCH_DIM), halo(col))
    vec = lambda width: pl.BlockSpec((1, width), lambda bi, i: (0, 0))
    mat = lambda rows: pl.BlockSpec((rows, BRANCH_DIM), lambda bi, i: (0, 0))
    return pl.pallas_call(
        _rwkv_kernel,
        grid=(b, t // tq),
        in_specs=[blk(6), blk(7), blk(8), pl.BlockSpec((1, tq, LORA_DIM), lambda bi, i: (bi, i, 0)),
                  hblk(6), hblk(7), hblk(8), pl.BlockSpec((1, SUBLANES, LORA_DIM), halo(0)),
                  vec(BRANCH_DIM), vec(BRANCH_DIM), vec(BRANCH_DIM), vec(LORA_DIM),
                  vec(BRANCH_DIM), mat(LORA_DIM), vec(BRANCH_DIM), mat(LORA_DIM), mat(LORA_DIM),
                  vec(BRANCH_DIM), vec(BRANCH_DIM), vec(BRANCH_DIM), vec(BRANCH_DIM), vec(BRANCH_DIM)],
        out_specs=pl.BlockSpec((1, tq, BRANCH_DIM), lambda bi, i: (bi, i, 0)),
        out_shape=jax.ShapeDtypeStruct((b, t, BRANCH_DIM), BF16),
        scratch_shapes=[pltpu.VMEM((BRANCH_DIM, BRANCH_DIM), F32)],
        compiler_params=_cparams("parallel", "arbitrary"),
        name="rwkv7",
    )(mix3, mix3, mix3, lora3, mix3, mix3, mix3, lora3,
      p["mu_r"], p["mu_k"], p["mu_v"], p["mu_l"], p["w0"], p["w2"], p["a0"], p["a2"], p["g2"],
      p["k_k"], p["k_a"], p["r_k"], p["gn_g"], p["gn_b"])


def _merge_kernel(oc_ref, oa_ref, or_ref, gate_ref, x_ref, wb_ref, wo_ref, g_ref, b_ref, o_ref):
    d = D_MODEL
    bd = BRANCH_DIM
    merged = (gate_ref[:, 0:d] * jnp.dot(oc_ref[...], wb_ref[0:bd, :], preferred_element_type=F32)
              + gate_ref[:, d:2 * d] * jnp.dot(oa_ref[...], wb_ref[bd:2 * bd, :], preferred_element_type=F32)
              + gate_ref[:, 2 * d:3 * d] * jnp.dot(or_ref[...], wb_ref[2 * bd:3 * bd, :],
                                                   preferred_element_type=F32))
    h = jnp.dot(merged.astype(BF16), wo_ref[...], preferred_element_type=F32)
    o_ref[...] = _layer_norm(DN_ALPHA * x_ref[...] + h, g_ref[...], b_ref[...])


def _merge(o_conv, o_att, o_rwkv, gates, x2, w_branch, w_out, ln_g, ln_b, tm=256):
    n = x2.shape[0]
    row = lambda i: (i, 0)
    vec = pl.BlockSpec((1, D_MODEL), lambda i: (0, 0))
    return pl.pallas_call(
        _merge_kernel,
        grid=(n // tm,),
        in_specs=[pl.BlockSpec((tm, BRANCH_DIM), row)] * 3
                 + [pl.BlockSpec((tm, GATE_COLS), row), pl.BlockSpec((tm, D_MODEL), row),
                    _vmem_whole(), _vmem_whole(), vec, vec],
        out_specs=pl.BlockSpec((tm, D_MODEL), row),
        out_shape=jax.ShapeDtypeStruct((n, D_MODEL), F32),
        compiler_params=_cparams("parallel"),
        name="merge_ln",
    )(o_conv, o_att, o_rwkv, gates, x2, w_branch, w_out, ln_g, ln_b)


def _swiglu_rows(xb, wg_ref, wu_ref, wd_ref, lead):
    acc = jnp.zeros((xb.shape[0], D_MODEL), F32)
    for c in range(0, D_FF, FF_CHUNK):
        cols = slice(c, c + FF_CHUNK)
        hg = jnp.dot(xb, wg_ref[lead + (slice(None), cols)], preferred_element_type=F32)
        hu = jnp.dot(xb, wu_ref[lead + (slice(None), cols)], preferred_element_type=F32)
        hidden = hg * _sigmoid(hg) * hu
        acc = acc + jnp.dot(hidden.astype(BF16), wd_ref[lead + (cols, slice(None))],
                            preferred_element_type=F32)
    return acc


def _ffn_kernel(x_ref, wg_ref, wu_ref, wd_ref, g_ref, b_ref, o_ref):
    x = x_ref[...]
    f = _swiglu_rows(x.astype(BF16), wg_ref, wu_ref, wd_ref, ())
    o_ref[...] = _layer_norm(DN_ALPHA * x + f, g_ref[...], b_ref[...])


def _ffn(x2, wg, wu, wd, ln_g, ln_b, tm=256):
    n = x2.shape[0]
    row = lambda i: (i, 0)
    vec = pl.BlockSpec((1, D_MODEL), lambda i: (0, 0))
    return pl.pallas_call(
        _ffn_kernel,
        grid=(n // tm,),
        in_specs=[pl.BlockSpec((tm, D_MODEL), row), _vmem_whole(), _vmem_whole(), _vmem_whole(), vec, vec],
        out_specs=pl.BlockSpec((tm, D_MODEL), row),
        out_shape=jax.ShapeDtypeStruct((n, D_MODEL), F32),
        compiler_params=_cparams("parallel"),
        name="ffn_ln",
    )(x2, wg, wu, wd, ln_g, ln_b)


def _router_kernel(x_ref, w_ref, e_ref, g_ref):
    logits = _dot_f32(x_ref[...], w_ref[...])
    lane = lax.broadcasted_iota(jnp.int32, logits.shape, 1).astype(F32)
    m1 = jnp.max(logits, axis=1, keepdims=True)
    e1 = jnp.min(jnp.where(logits == m1, lane, float(N_EXPERTS)), axis=1, keepdims=True)
    rest = jnp.where(lane == e1, -jnp.inf, logits)
    m2 = jnp.max(rest, axis=1, keepdims=True)
    e2 = jnp.min(jnp.where(rest == m2, lane, float(N_EXPERTS)), axis=1, keepdims=True)
    w2 = jnp.exp(m2 - m1)
    denom = 1.0 + w2
    e_ref[...] = jnp.where(lane == 0.0, e1, e2).astype(jnp.int32)
    g_ref[...] = jnp.where(lane == 0.0, 1.0 / denom, w2 / denom)


def _router(x2, router_w, tm=512):
    n = x2.shape[0]
    row = lambda i: (i, 0)
    return pl.pallas_call(
        _router_kernel,
        grid=(n // tm,),
        in_specs=[pl.BlockSpec((tm, D_MODEL), row), pl.BlockSpec((D_MODEL, N_EXPERTS), lambda i: (0, 0))],
        out_specs=[pl.BlockSpec((tm, N_EXPERTS), row), pl.BlockSpec((tm, N_EXPERTS), row)],
        out_shape=[jax.ShapeDtypeStruct((n, N_EXPERTS), jnp.int32),
                   jax.ShapeDtypeStruct((n, N_EXPERTS), F32)],
        compiler_params=_cparams("parallel"),
        name="router",
    )(x2, router_w)


def _gather_kernel(src_idx_ref, src_ref, dst_ref, sems, *, n_copies):
    n_batches = n_copies // GATHER_BATCH

    def copy(j, slot):
        return pltpu.make_async_copy(src_ref.at[pl.ds(src_idx_ref[j], 1), :],
                                     dst_ref.at[pl.ds(j, 1), :], sems.at[slot])

    def start_batch(bi, slot):
        def one(j, carry):
            copy(bi * GATHER_BATCH + j, slot).start()
            return carry
        lax.fori_loop(0, GATHER_BATCH, one, 0)

    def wait_batch(bi, slot):
        def one(j, carry):
            copy(bi * GATHER_BATCH + j, slot).wait()
            return carry
        lax.fori_loop(0, GATHER_BATCH, one, 0)

    start_batch(0, 0)

    def step(bi, carry):
        slot = bi % 2

        @pl.when(bi + 1 < n_batches)
        def _():
            start_batch(bi + 1, 1 - slot)

        wait_batch(bi, slot)
        return carry

    lax.fori_loop(0, n_batches, step, 0)


def _gather_rows(src, src_idx):
    n_copies = src_idx.shape[0]
    assert n_copies % GATHER_BATCH == 0
    return pl.pallas_call(
        functools.partial(_gather_kernel, n_copies=n_copies),
        grid_spec=pltpu.PrefetchScalarGridSpec(
            num_scalar_prefetch=1,
            grid=(1,),
            in_specs=[pl.BlockSpec(memory_space=pl.ANY)],
            out_specs=pl.BlockSpec(memory_space=pl.ANY),
            scratch_shapes=[pltpu.SemaphoreType.DMA((2,))],
        ),
        out_shape=jax.ShapeDtypeStruct((n_copies, src.shape[1]), src.dtype),
        compiler_params=_cparams("arbitrary"),
        name="row_gather",
    )(src_idx, src)


def _moe_kernel(blk_e_ref, n_used_ref, x_ref, wg_ref, wu_ref, wd_ref, o_ref):
    i = pl.program_id(0)

    @pl.when(i < n_used_ref[0])
    def _():
        o_ref[...] = _swiglu_rows(x_ref[...].astype(BF16), wg_ref, wu_ref, wd_ref, (0,))

    @pl.when(i >= n_used_ref[0])
    def _():
        o_ref[...] = jnp.zeros_like(o_ref)


def _moe_experts(xb, blk_e, n_used, wg, wu, wd):
    n_rows = xb.shape[0]
    n_blocks = n_rows // MOE_ROWS
    row = lambda i, be, nu: (i, 0)
    return pl.pallas_call(
        _moe_kernel,
        grid_spec=pltpu.PrefetchScalarGridSpec(
            num_scalar_prefetch=2,
            grid=(n_blocks,),
            in_specs=[pl.BlockSpec((MOE_ROWS, D_MODEL), row),
                      pl.BlockSpec((1, D_MODEL, D_FF), lambda i, be, nu: (be[i], 0, 0)),
                      pl.BlockSpec((1, D_MODEL, D_FF), lambda i, be, nu: (be[i], 0, 0)),
                      pl.BlockSpec((1, D_FF, D_MODEL), lambda i, be, nu: (be[i], 0, 0))],
            out_specs=pl.BlockSpec((MOE_ROWS, D_MODEL), row),
        ),
        out_shape=jax.ShapeDtypeStruct((n_rows, D_MODEL), F32),
        compiler_params=_cparams("arbitrary"),
        name="moe_experts",
    )(blk_e, n_used, xb, wg, wu, wd)


def _combine_kernel(x_ref, y0_ref, y1_ref, gate_ref, g_ref, b_ref, o_ref):
    gate = gate_ref[...]
    f = gate[:, 0:1] * y0_ref[...] + gate[:, 1:2] * y1_ref[...]
    o_ref[...] = _layer_norm(DN_ALPHA * x_ref[...] + f, g_ref[...], b_ref[...])


def _combine(x2, y_slots, gate, ln_g, ln_b, tm=512):
    n = x2.shape[0]
    row = lambda i: (i, 0)
    vec = pl.BlockSpec((1, D_MODEL), lambda i: (0, 0))
    return pl.pallas_call(
        _combine_kernel,
        grid=(n // tm,),
        in_specs=[pl.BlockSpec((tm, D_MODEL), row), pl.BlockSpec((tm, D_MODEL), row),
                  pl.BlockSpec((tm, D_MODEL), lambda i: (n // tm + i, 0)),
                  pl.BlockSpec((tm, EXPERT_TOPK), row), vec, vec],
        out_specs=pl.BlockSpec((tm, D_MODEL), row),
        out_shape=jax.ShapeDtypeStruct((n, D_MODEL), F32),
        compiler_params=_cparams("parallel"),
        name="combine_ln",
    )(x2, y_slots, y_slots, gate, ln_g, ln_b)


def _moe_layer(x2, router_w, wg, wu, wd, ln_g, ln_b):
    n = x2.shape[0]
    nk = n * EXPERT_TOPK
    top_e, gate = _router(x2, router_w)
    top_e = top_e[:, :EXPERT_TOPK]
    gate = gate[:, :EXPERT_TOPK]
    flat_e = top_e.reshape(-1)
    onehot = (flat_e[:, None] == jnp.arange(N_EXPERTS, dtype=jnp.int32)[None, :]).astype(jnp.int32)
    counts = jnp.sum(onehot, axis=0)
    rank = jnp.sum((jnp.cumsum(onehot, axis=0) - onehot) * onehot, axis=1)
    padded = (counts + MOE_ROWS - 1) // MOE_ROWS * MOE_ROWS
    pad_end = jnp.cumsum(padded)
    pad_start = pad_end - padded
    dest = (pad_start[flat_e] + rank).astype(jnp.int32)
    n_blocks = -(-nk // MOE_ROWS) + N_EXPERTS
    n_rows = n_blocks * MOE_ROWS
    flat_tok = jnp.arange(nk, dtype=jnp.int32) // EXPERT_TOPK
    buf_tok = jnp.zeros((n_rows,), jnp.int32).at[dest].set(flat_tok)
    blk_start = jnp.arange(n_blocks, dtype=jnp.int32) * MOE_ROWS
    blk_e = jnp.minimum(jnp.searchsorted(pad_end, blk_start, side="right"), N_EXPERTS - 1).astype(jnp.int32)
    n_used = (pad_end[-1:] // MOE_ROWS).astype(jnp.int32)

    xb = _gather_rows(x2, buf_tok)
    yb = _moe_experts(xb, blk_e, n_used, wg, wu, wd)
    y_slots = _gather_rows(yb, dest.reshape(n, EXPERT_TOPK).T.reshape(-1))
    return _combine(x2, y_slots, gate, ln_g, ln_b)


def _rope_tables(t):
    half = HEAD_DIM // 2
    inv = ROPE_THETA ** (-jnp.arange(half, dtype=F32) / half)
    ang = jnp.arange(t).astype(F32)[:, None] * inv[None, :]
    cos = jnp.cos(ang)
    sin = jnp.sin(ang)
    cos_t = jnp.tile(jnp.concatenate([cos, cos], -1), (1, N_HEADS))
    sin_t = jnp.tile(jnp.concatenate([-sin, sin], -1), (1, N_HEADS))
    return cos_t, sin_t


def _pad_rows(w, start):
    out = jnp.zeros((LORA_DIM, BRANCH_DIM), F32)
    return lax.dynamic_update_slice(out, w, (start, 0)).astype(BF16)


def _mixer(x2, b, t, cos_t, sin_t, w_in, conv_w, shift_mu, decay_w0, decay_w2, aaa_a0, aaa_w2, gate_w2,
           k_k, k_a, r_k, gn_g, gn_b, w_branch, w_out, ln_g, ln_b):
    w_in = w_in.astype(BF16)
    mix, lora, gates = _in_proj(x2, w_in[:, :MIX_COLS], w_in[:, MIX_COLS:MIX_COLS + LORA_DIM],
                                w_in[:, MIX_COLS + LORA_DIM:])
    mix3 = mix.reshape(b, t, MIX_COLS)
    lora3 = lora.reshape(b, t, LORA_DIM)

    o_conv = _short_conv(mix3, conv_w)

    qh, kh, vh, kmean = _att_prep(mix3, cos_t, sin_t)
    nb = t // MOBA_BLOCK
    kmean = kmean.reshape(b, nb, N_HEADS, HEAD_DIM).transpose(0, 2, 1, 3)
    o_att = _moba(qh, kh, vh, kmean).transpose(0, 2, 1, 3).reshape(b * t, BRANCH_DIM)

    row = lambda a: a.reshape(1, -1)
    p = {
        "mu_r": row(shift_mu[0:BRANCH_DIM]), "mu_k": row(shift_mu[BRANCH_DIM:2 * BRANCH_DIM]),
        "mu_v": row(shift_mu[2 * BRANCH_DIM:3 * BRANCH_DIM]), "mu_l": row(shift_mu[3 * BRANCH_DIM:]),
        "w0": row(decay_w0), "w2": _pad_rows(decay_w2, 0),
        "a0": row(aaa_a0), "a2": _pad_rows(aaa_w2, DECAY_LORA),
        "g2": _pad_rows(gate_w2, DECAY_LORA + AAA_LORA),
        "k_k": row(k_k), "k_a": row(k_a), "r_k": row(r_k), "gn_g": row(gn_g), "gn_b": row(gn_b),
    }
    o_rwkv = _rwkv(mix3, lora3, p)

    return _merge(o_conv.reshape(b * t, BRANCH_DIM), o_att, o_rwkv.reshape(b * t, BRANCH_DIM), gates, x2,
                  w_branch.astype(BF16), w_out.astype(BF16), row(ln_g), row(ln_b))


def kernel(x, w_in, conv_w, shift_mu, decay_w0, decay_w2, aaa_a0, aaa_w2, gate_w2, k_k, k_a, r_k, gn_g, gn_b,
           w_branch, w_out, ln1_g, ln1_b, ln2_g, ln2_b, ffn_w_gate, ffn_w_up, ffn_w_down, router_w,
           moe_w_gate, moe_w_up, moe_w_down):
    b, t, d = x.shape
    x2 = x.reshape(b * t, d)
    cos_t, sin_t = _rope_tables(t)
    row = lambda a: a.reshape(1, -1)
    for l in range(DEPTH):
        x2 = _mixer(x2, b, t, cos_t, sin_t, w_in[l], conv_w[l], shift_mu[l], decay_w0[l], decay_w2[l],
                    aaa_a0[l], aaa_w2[l], gate_w2[l], k_k[l], k_a[l], r_k[l], gn_g[l], gn_b[l],
                    w_branch[l], w_out[l], ln1_g[l], ln1_b[l])
        j = l // 2
        if l % 2 == 0:
            x2 = _ffn(x2, ffn_w_gate[j].astype(BF16), ffn_w_up[j].astype(BF16), ffn_w_down[j].astype(BF16),
                      row(ln2_g[l]), row(ln2_b[l]))
        else:
            x2 = _moe_layer(x2, router_w[j], moe_w_gate[j].astype(BF16), moe_w_up[j].astype(BF16),
                            moe_w_down[j].astype(BF16), row(ln2_g[l]), row(ln2_b[l]))
    return x2.reshape(b, t, d)
```

```python
import functools

import jax
import jax.numpy as jnp
from jax import lax
from jax.experimental import pallas as pl
from jax.experimental.pallas import tpu as pltpu

F32 = jnp.float32
BF16 = jnp.bfloat16

D_MODEL = 1024
HEAD_DIM = 64
BRANCH_DIM = 512
N_HEADS = BRANCH_DIM // HEAD_DIM
DECAY_LORA = 64
AAA_LORA = 64
GATE_LORA = 128
LORA_DIM = DECAY_LORA + AAA_LORA + GATE_LORA
MIX_COLS = 9 * BRANCH_DIM
GATE_COLS = 3 * D_MODEL
MOBA_BLOCK = 256
MOBA_TOPK = 3
ROPE_THETA = 10000.0
D_FF = 2816
N_EXPERTS = 8
EXPERT_TOPK = 2
MOE_ROWS = 256
LN_EPS = 1e-5
GN_EPS = 64e-5
DEPTH = 2
DN_ALPHA = (2 * DEPTH) ** 0.25

VMEM_LIMIT_BYTES = 56 * 1024 * 1024
SUBLANES = 8
RWKV_TILE = 256
RWKV_CHUNK = 64
FF_CHUNK = 256
HEAD_SHIFT = HEAD_DIM.bit_length() - 1
CHUNK_SHIFT = RWKV_CHUNK.bit_length() - 1


def _cparams(*sem):
    return pltpu.CompilerParams(dimension_semantics=sem, vmem_limit_bytes=VMEM_LIMIT_BYTES)


def _vmem_whole():
    return pl.BlockSpec(memory_space=pltpu.VMEM)


def _dot(a, b):
    return jnp.dot(a.astype(BF16), b.astype(BF16), preferred_element_type=F32)


def _dot_nt(a, b):
    return lax.dot_general(a.astype(BF16), b.astype(BF16), (((1,), (1,)), ((), ())),
                           preferred_element_type=F32)


def _dot_tn(a, b):
    return lax.dot_general(a.astype(BF16), b.astype(BF16), (((0,), (0,)), ((), ())),
                           preferred_element_type=F32)


def _dot_f32(a, b):
    return jnp.dot(a, b, preferred_element_type=F32, precision=lax.Precision.HIGHEST)


def _sigmoid(x):
    return 1.0 / (1.0 + jnp.exp(-x))


def _layer_norm(y, g, b):
    mu = jnp.mean(y, axis=-1, keepdims=True)
    d = y - mu
    var = jnp.mean(d * d, axis=-1, keepdims=True)
    return d * lax.rsqrt(var + LN_EPS) * g + b


def _shift_rows(u, halo, n):
    out = pltpu.roll(u, n, axis=0)
    row = lax.broadcasted_iota(jnp.int32, u.shape, 0)
    for r in range(n):
        src = halo[SUBLANES - n + r:SUBLANES - n + r + 1, :]
        out = jnp.where(row == r, src, out)
    return out


def _halo_index(tile_rows):
    step = tile_rows // SUBLANES

    def index(col):
        return lambda b, i: (b, jnp.maximum(i * step - 1, 0), col)
    return index


def _in_proj_kernel(x_ref, wm_ref, wl_ref, wg_ref, mix_ref, lora_ref, gate_ref):
    xb = x_ref[...].astype(BF16)
    for j in range(0, MIX_COLS, BRANCH_DIM):
        mix_ref[:, j:j + BRANCH_DIM] = jnp.dot(xb, wm_ref[:, j:j + BRANCH_DIM],
                                               preferred_element_type=F32)
    lora_ref[...] = jnp.dot(xb, wl_ref[...], preferred_element_type=F32)
    for j in range(0, GATE_COLS, BRANCH_DIM):
        g = jnp.dot(xb, wg_ref[:, j:j + BRANCH_DIM], preferred_element_type=F32)
        gate_ref[:, j:j + BRANCH_DIM] = _sigmoid(g)


def _in_proj(x2, w_mix, w_lora, w_gate, tm=256):
    n = x2.shape[0]
    row = lambda i: (i, 0)
    return pl.pallas_call(
        _in_proj_kernel,
        grid=(n // tm,),
        in_specs=[pl.BlockSpec((tm, D_MODEL), row), _vmem_whole(), _vmem_whole(), _vmem_whole()],
        out_specs=[pl.BlockSpec((tm, MIX_COLS), row), pl.BlockSpec((tm, LORA_DIM), row),
                   pl.BlockSpec((tm, GATE_COLS), row)],
        out_shape=[jax.ShapeDtypeStruct((n, MIX_COLS), F32), jax.ShapeDtypeStruct((n, LORA_DIM), F32),
                   jax.ShapeDtypeStruct((n, GATE_COLS), F32)],
        compiler_params=_cparams("parallel"),
        name="in_proj",
    )(x2, w_mix, w_lora, w_gate)


def _conv_kernel(ch_ref, cb_ref, cc_ref, hh_ref, hc_ref, w_ref, o_ref):
    first = pl.program_id(1) == 0
    u = cc_ref[0] * ch_ref[0]
    halo = jnp.where(first, 0.0, hc_ref[0] * hh_ref[0])
    w = w_ref[...]
    y = w[2:3, :] * u + w[1:2, :] * _shift_rows(u, halo, 1) + w[0:1, :] * _shift_rows(u, halo, 2)
    o_ref[0] = (cb_ref[0] * y).astype(o_ref.dtype)


def _short_conv(mix3, conv_w, tq=512):
    b, t, _ = mix3.shape
    blk = lambda col: pl.BlockSpec((1, tq, BRANCH_DIM), lambda bi, i: (bi, i, col))
    halo = _halo_index(tq)
    hblk = lambda col: pl.BlockSpec((1, SUBLANES, BRANCH_DIM), halo(col))
    return pl.pallas_call(
        _conv_kernel,
        grid=(b, t // tq),
        in_specs=[blk(0), blk(1), blk(2), hblk(0), hblk(2),
                  pl.BlockSpec((3, BRANCH_DIM), lambda bi, i: (0, 0))],
        out_specs=pl.BlockSpec((1, tq, BRANCH_DIM), lambda bi, i: (bi, i, 0)),
        out_shape=jax.ShapeDtypeStruct((b, t, BRANCH_DIM), BF16),
        compiler_params=_cparams("parallel", "parallel"),
        name="short_conv",
    )(mix3, mix3, mix3, mix3, mix3, conv_w)


def _att_prep_kernel(q_ref, k_ref, v_ref, cos_ref, sin_ref, qo_ref, ko_ref, vo_ref, km_ref):
    cos = cos_ref[...]
    sin = sin_ref[...]
    lane = lax.broadcasted_iota(jnp.int32, cos.shape, 1)
    first_half = (lane & (HEAD_DIM - 1)) < (HEAD_DIM // 2)

    def rope(x):
        swapped = jnp.where(first_half, pltpu.roll(x, BRANCH_DIM - HEAD_DIM // 2, axis=1),
                            pltpu.roll(x, HEAD_DIM // 2, axis=1))
        return x * cos + swapped * sin

    kr = rope(k_ref[0])
    qo_ref[0] = rope(q_ref[0])
    ko_ref[0] = kr.astype(BF16)
    vo_ref[0] = v_ref[0].astype(BF16)
    km_ref[0, 0] = jnp.mean(kr, axis=0, keepdims=True)


def _att_prep(mix3, cos_t, sin_t):
    b, t, _ = mix3.shape
    tq = MOBA_BLOCK
    nb = t // tq
    blk = lambda col: pl.BlockSpec((1, tq, BRANCH_DIM), lambda bi, i: (bi, i, col))
    tab = pl.BlockSpec((tq, BRANCH_DIM), lambda bi, i: (i, 0))
    out = pl.BlockSpec((1, tq, BRANCH_DIM), lambda bi, i: (bi, i, 0))
    return pl.pallas_call(
        _att_prep_kernel,
        grid=(b, nb),
        in_specs=[blk(3), blk(4), blk(5), tab, tab],
        out_specs=[out, out, out, pl.BlockSpec((1, 1, 1, BRANCH_DIM), lambda bi, i: (bi, i, 0, 0))],
        out_shape=[jax.ShapeDtypeStruct((b, t, BRANCH_DIM), F32),
                   jax.ShapeDtypeStruct((b, t, BRANCH_DIM), BF16),
                   jax.ShapeDtypeStruct((b, t, BRANCH_DIM), BF16),
                   jax.ShapeDtypeStruct((b, nb, 1, BRANCH_DIM), F32)],
        compiler_params=_cparams("parallel", "parallel"),
        name="att_prep",
    )(mix3, mix3, mix3, cos_t, sin_t)


def _moba_kernel(q_ref, k_ref, v_ref, km_ref, o_ref, *, nb, heads):
    i = pl.program_id(2)
    bs = MOBA_BLOCK
    width = heads * HEAD_DIM
    q = q_ref[0]
    km = km_ref[0]
    head_of_lane = lax.broadcasted_iota(jnp.int32, (1, width), 1) >> HEAD_SHIFT
    blk = lax.broadcasted_iota(jnp.int32, (bs, nb), 1)
    valid = blk < i
    bit_weight = jnp.where(valid, (1 << blk).astype(F32), 0.0)
    scale = HEAD_DIM ** -0.5

    q_heads = []
    picks = []
    for g in range(heads):
        qg = jnp.where(head_of_lane == g, q, 0.0)
        gate = lax.dot_general(qg, km, (((1,), (1,)), ((), ())), preferred_element_type=F32,
                               precision=lax.Precision.HIGHEST)
        gate = jnp.where(valid, gate, -jnp.inf)
        rank = jnp.zeros((bs, nb), jnp.int32)
        for m in range(nb):
            gm = gate[:, m:m + 1]
            beats = (gm > gate) | ((gm == gate) & (blk > m))
            rank = rank + beats.astype(jnp.int32)
        chosen = jnp.where(rank < MOBA_TOPK, bit_weight, 0.0)
        picks.append(jnp.sum(chosen, axis=1, keepdims=True).astype(jnp.int32))
        q_heads.append(qg.astype(BF16))

    def merge(parts):
        out = parts[0]
        for g in range(1, heads):
            out = jnp.where(head_of_lane == g, parts[g], out)
        return out

    def block_update(off, keep, carry):
        kb = k_ref[0, pl.ds(off, bs), :]
        vb = v_ref[0, pl.ds(off, bs), :]
        ms, ls, acc = carry
        new_m, new_l, alphas, pvs = [], [], [], []
        for g in range(heads):
            s = lax.dot_general(q_heads[g], kb, (((1,), (1,)), ((), ())), preferred_element_type=F32) * scale
            s = jnp.where(keep[g], s, -jnp.inf)
            m_new = jnp.maximum(ms[g], jnp.max(s, axis=1, keepdims=True))
            alpha = jnp.exp(ms[g] - m_new)
            p = jnp.exp(s - m_new)
            new_m.append(m_new)
            new_l.append(alpha * ls[g] + jnp.sum(p, axis=1, keepdims=True))
            alphas.append(alpha)
            pvs.append(jnp.dot(p.astype(BF16), vb, preferred_element_type=F32))
        return tuple(new_m), tuple(new_l), merge(alphas) * acc + merge(pvs)

    r_id = lax.broadcasted_iota(jnp.int32, (bs, bs), 0)
    c_id = lax.broadcasted_iota(jnp.int32, (bs, bs), 1)
    init = (tuple(jnp.full((bs, 1), -jnp.inf, F32) for _ in range(heads)),
            tuple(jnp.zeros((bs, 1), F32) for _ in range(heads)),
            jnp.zeros((bs, width), F32))
    carry = block_update(pl.multiple_of(i * bs, bs), [c_id <= r_id] * heads, init)

    def body(j, carry):
        keep = [((picks[g] >> j) & 1) == 1 for g in range(heads)]
        return block_update(pl.multiple_of(j * bs, bs), keep, carry)

    _, ls, acc = lax.fori_loop(0, i, body, carry)
    o_ref[0] = (acc / merge(ls)).astype(o_ref.dtype)


def _moba(q3, k3, v3, kmean, heads=4):
    b, t, _ = q3.shape
    nb = t // MOBA_BLOCK
    width = heads * HEAD_DIM
    seq = pl.BlockSpec((1, t, width), lambda bi, hi, i: (bi, 0, hi))
    qblk = pl.BlockSpec((1, MOBA_BLOCK, width), lambda bi, hi, i: (bi, i, hi))
    return pl.pallas_call(
        functools.partial(_moba_kernel, nb=nb, heads=heads),
        grid=(b, N_HEADS // heads, nb),
        in_specs=[qblk, seq, seq, pl.BlockSpec((1, nb, width), lambda bi, hi, i: (bi, 0, hi))],
        out_specs=qblk,
        out_shape=jax.ShapeDtypeStruct((b, t, BRANCH_DIM), BF16),
        compiler_params=_cparams("parallel", "parallel", "arbitrary"),
        name="moba",
    )(q3, k3, v3, kmean)


def _rwkv_kernel(r_ref, k_ref, v_ref, lo_ref, hr_ref, hk_ref, hv_ref, hl_ref,
                 mu_r_ref, mu_k_ref, mu_v_ref, mu_l_ref, w0_ref, w2_ref, a0_ref, a2_ref, g2_ref,
                 kk_ref, ka_ref, rk_ref, gng_ref, gnb_ref, o_ref, state_ref):
    i = pl.program_id(1)
    first = i == 0
    tq, ch = RWKV_TILE, RWKV_CHUNK
    hd, bd = HEAD_DIM, BRANCH_DIM

    @pl.when(first)
    def _():
        state_ref[...] = jnp.zeros_like(state_ref)

    def lerp(x_ref, h_ref, mu_ref):
        x = x_ref[0]
        halo = jnp.where(first, 0.0, h_ref[0])
        return x + (_shift_rows(x, halo, 1) - x) * mu_ref[...]

    r = lerp(r_ref, hr_ref, mu_r_ref)
    k = lerp(k_ref, hk_ref, mu_k_ref)
    v = lerp(v_ref, hv_ref, mu_v_ref)
    lo = lerp(lo_ref, hl_ref, mu_l_ref)

    z = w0_ref[...] + _dot(jnp.tanh(lo), w2_ref[...])
    ew = jnp.exp(-0.5) * _sigmoid(z)
    a = _sigmoid(a0_ref[...] + _dot(lo, a2_ref[...]))
    g = _dot(_sigmoid(lo), g2_ref[...])

    lane_r = lax.broadcasted_iota(jnp.int32, (bd, bd), 0)
    lane_c = lax.broadcasted_iota(jnp.int32, (bd, bd), 1)
    same_head = (lane_r >> HEAD_SHIFT) == (lane_c >> HEAD_SHIFT)
    head_sum = jnp.where(same_head, 1.0, 0.0)

    kk = k * kk_ref[...]
    kk = kk / jnp.maximum(jnp.sqrt(_dot_f32(kk * kk, head_sum)), 1e-12)
    k2 = k * (1.0 + (a - 1.0) * ka_ref[...])
    bonus = _dot_f32(r * k2 * rk_ref[...], head_sum) * v

    t_r = lax.broadcasted_iota(jnp.int32, (tq, tq), 0)
    t_c = lax.broadcasted_iota(jnp.int32, (tq, tq), 1)
    same_chunk = (t_r >> CHUNK_SHIFT) == (t_c >> CHUNK_SHIFT)
    cs = _dot_f32(jnp.where(same_chunk & (t_c <= t_r), 1.0, 0.0), ew)
    cs_end = _dot_f32(jnp.where(same_chunk, 1.0, 0.0), ew)

    kka = kk * a
    a_t = -kk * jnp.exp(ew - cs)
    b_hat = kka * jnp.exp(cs)
    k_hat = k2 * jnp.exp(cs)
    r_t = r * jnp.exp(-cs)
    b_e = kka * jnp.exp(cs - cs_end)
    k_e = k2 * jnp.exp(cs - cs_end)
    p_c = jnp.exp(-cs_end)

    strict = jnp.where(same_chunk & (t_c < t_r), 1.0, 0.0)
    incl = jnp.where(same_chunk & (t_c <= t_r), 1.0, 0.0)
    eye = jnp.where(t_c == t_r, 1.0, 0.0)
    lane = lax.broadcasted_iota(jnp.int32, (1, bd), 1)

    a_til = jnp.zeros((tq, bd), F32)
    u_0 = jnp.zeros((tq, bd), F32)
    r_til = jnp.zeros((tq, bd), F32)
    y_hat = jnp.zeros((tq, bd), F32)
    for h in range(N_HEADS):
        hm = (lane >> HEAD_SHIFT) == h
        a_h = jnp.where(hm, a_t, 0.0)
        r_h = jnp.where(hm, r_t, 0.0)
        l_ab = strict * _dot_nt(a_h, b_hat)
        l_ak = strict * _dot_nt(a_h, k_hat)
        m_rb = incl * _dot_nt(r_h, b_hat)
        m_rk = incl * _dot_nt(r_h, k_hat)
        t_inv = eye + l_ab
        pw = l_ab
        n = 2
        while n < ch:
            pw = _dot(pw, pw)
            t_inv = t_inv + _dot(t_inv, pw)
            n *= 2
        at_h = _dot(t_inv, a_t)
        u0_h = _dot(t_inv, _dot(l_ak, v))
        rt_h = _dot(m_rb, at_h)
        yh_h = _dot(m_rb, u0_h) + _dot(m_rk, v)
        a_til = jnp.where(hm, at_h, a_til)
        u_0 = jnp.where(hm, u0_h, u_0)
        r_til = jnp.where(hm, rt_h, r_til)
        y_hat = jnp.where(hm, yh_h, y_hat)
    r_til = r_til + r_t

    st = state_ref[...]
    ys = []
    for c in range(tq // ch):
        rows = slice(c * ch, (c + 1) * ch)
        g_c = jnp.where(same_head, _dot_tn(a_til[rows], b_e[rows]), 0.0)
        h_c = jnp.where(same_head, _dot_tn(u_0[rows], b_e[rows]) + _dot_tn(v[rows], k_e[rows]), 0.0)
        ys.append(_dot_nt(r_til[rows], st) + y_hat[rows])
        st = st * p_c[c * ch:c * ch + 1, :] + _dot(st, g_c) + h_c
    state_ref[...] = st
    y = jnp.concatenate(ys, axis=0)

    mu = _dot_f32(y, head_sum) * (1.0 / hd)
    d = y - mu
    var = _dot_f32(d * d, head_sum) * (1.0 / hd)
    y = d * lax.rsqrt(var + GN_EPS) * gng_ref[...] + gnb_ref[...]
    o_ref[0] = ((y + bonus) * g).astype(o_ref.dtype)


def _rwkv(mix3, lora3, p):
    b, t, _ = mix3.shape
    tq = RWKV_TILE
    halo = _halo_index(tq)
    blk = lambda col: pl.BlockSpec((1, tq, BRANCH_DIM), lambda bi, i: (bi, i, col))
    hblk = lambda col: pl.BlockSpec((1, SUBLANES, BRANCH_DIM), halo(col))
    vec = lambda width: pl.BlockSpec((1, width), lambda bi, i: (0, 0))
    mat = lambda rows: pl.BlockSpec((rows, BRANCH_DIM), lambda bi, i: (0, 0))
    return pl.pallas_call(
        _rwkv_kernel,
        grid=(b, t // tq),
        in_specs=[blk(6), blk(7), blk(8), pl.BlockSpec((1, tq, LORA_DIM), lambda bi, i: (bi, i, 0)),
                  hblk(6), hblk(7), hblk(8), pl.BlockSpec((1, SUBLANES, LORA_DIM), halo(0)),
                  vec(BRANCH_DIM), vec(BRANCH_DIM), vec(BRANCH_DIM), vec(LORA_DIM),
                  vec(BRANCH_DIM), mat(LORA_DIM), vec(BRANCH_DIM), mat(LORA_DIM), mat(LORA_DIM),
                  vec(BRANCH_DIM), vec(BRANCH_DIM), vec(BRANCH_DIM), vec(BRANCH_DIM), vec(BRANCH_DIM)],
        out_specs=pl.BlockSpec((1, tq, BRANCH_DIM), lambda bi, i: (bi, i, 0)),
        out_shape=jax.ShapeDtypeStruct((b, t, BRANCH_DIM), BF16),
        scratch_shapes=[pltpu.VMEM((BRANCH_DIM, BRANCH_DIM), F32)],
        compiler_params=_cparams("parallel", "arbitrary"),
        name="rwkv7",
    )(mix3, mix3, mix3, lora3, mix3, mix3, mix3, lora3,
      p["mu_r"], p["mu_k"], p["mu_v"], p["mu_l"], p["w0"], p["w2"], p["a0"], p["a2"], p["g2"],
      p["k_k"], p["k_a"], p["r_k"], p["gn_g"], p["gn_b"])


def _merge_kernel(oc_ref, oa_ref, or_ref, gate_ref, x_ref, wb_ref, wo_ref, g_ref, b_ref, o_ref):
    d = D_MODEL
    bd = BRANCH_DIM
    merged = (gate_ref[:, 0:d] * jnp.dot(oc_ref[...], wb_ref[0:bd, :], preferred_element_type=F32)
              + gate_ref[:, d:2 * d] * jnp.dot(oa_ref[...], wb_ref[bd:2 * bd, :], preferred_element_type=F32)
              + gate_ref[:, 2 * d:3 * d] * jnp.dot(or_ref[...], wb_ref[2 * bd:3 * bd, :],
                                                   preferred_element_type=F32))
    h = jnp.dot(merged.astype(BF16), wo_ref[...], preferred_element_type=F32)
    o_ref[...] = _layer_norm(DN_ALPHA * x_ref[...] + h, g_ref[...], b_ref[...])


def _merge(o_conv, o_att, o_rwkv, gates, x2, w_branch, w_out, ln_g, ln_b, tm=256):
    n = x2.shape[0]
    row = lambda i: (i, 0)
    vec = pl.BlockSpec((1, D_MODEL), lambda i: (0, 0))
    return pl.pallas_call(
        _merge_kernel,
        grid=(n // tm,),
        in_specs=[pl.BlockSpec((tm, BRANCH_DIM), row)] * 3
                 + [pl.BlockSpec((tm, GATE_COLS), row), pl.BlockSpec((tm, D_MODEL), row),
                    _vmem_whole(), _vmem_whole(), vec, vec],
        out_specs=pl.BlockSpec((tm, D_MODEL), row),
        out_shape=jax.ShapeDtypeStruct((n, D_MODEL), F32),
        compiler_params=_cparams("parallel"),
        name="merge_ln",
    )(o_conv, o_att, o_rwkv, gates, x2, w_branch, w_out, ln_g, ln_b)


def _swiglu_rows(xb, wg_ref, wu_ref, wd_ref, lead):
    acc = jnp.zeros((xb.shape[0], D_MODEL), F32)
    for c in range(0, D_FF, FF_CHUNK):
        cols = slice(c, c + FF_CHUNK)
        hg = jnp.dot(xb, wg_ref[lead + (slice(None), cols)], preferred_element_type=F32)
        hu = jnp.dot(xb, wu_ref[lead + (slice(None), cols)], preferred_element_type=F32)
        hidden = hg * _sigmoid(hg) * hu
        acc = acc + jnp.dot(hidden.astype(BF16), wd_ref[lead + (cols, slice(None))],
                            preferred_element_type=F32)
    return acc


def _ffn_kernel(x_ref, wg_ref, wu_ref, wd_ref, g_ref, b_ref, o_ref):
    x = x_ref[...]
    f = _swiglu_rows(x.astype(BF16), wg_ref, wu_ref, wd_ref, ())
    o_ref[...] = _layer_norm(DN_ALPHA * x + f, g_ref[...], b_ref[...])


def _ffn(x2, wg, wu, wd, ln_g, ln_b, tm=256):
    n = x2.shape[0]
    row = lambda i: (i, 0)
    vec = pl.BlockSpec((1, D_MODEL), lambda i: (0, 0))
    return pl.pallas_call(
        _ffn_kernel,
        grid=(n // tm,),
        in_specs=[pl.BlockSpec((tm, D_MODEL), row), _vmem_whole(), _vmem_whole(), _vmem_whole(), vec, vec],
        out_specs=pl.BlockSpec((tm, D_MODEL), row),
        out_shape=jax.ShapeDtypeStruct((n, D_MODEL), F32),
        compiler_params=_cparams("parallel"),
        name="ffn_ln",
    )(x2, wg, wu, wd, ln_g, ln_b)


def _router_kernel(x_ref, w_ref, e_ref, g_ref):
    logits = _dot_f32(x_ref[...], w_ref[...])
    lane = lax.broadcasted_iota(jnp.int32, logits.shape, 1).astype(F32)
    m1 = jnp.max(logits, axis=1, keepdims=True)
    e1 = jnp.min(jnp.where(logits == m1, lane, float(N_EXPERTS)), axis=1, keepdims=True)
    rest = jnp.where(lane == e1, -jnp.inf, logits)
    m2 = jnp.max(rest, axis=1, keepdims=True)
    e2 = jnp.min(jnp.where(rest == m2, lane, float(N_EXPERTS)), axis=1, keepdims=True)
    w2 = jnp.exp(m2 - m1)
    denom = 1.0 + w2
    e_ref[...] = jnp.where(lane == 0.0, e1, e2).astype(jnp.int32)
    g_ref[...] = jnp.where(lane == 0.0, 1.0 / denom, w2 / denom)


def _router(x2, router_w, tm=512):
    n = x2.shape[0]
    row = lambda i: (i, 0)
    return pl.pallas_call(
        _router_kernel,
        grid=(n // tm,),
        in_specs=[pl.BlockSpec((tm, D_MODEL), row), pl.BlockSpec((D_MODEL, N_EXPERTS), lambda i: (0, 0))],
        out_specs=[pl.BlockSpec((tm, N_EXPERTS), row), pl.BlockSpec((tm, N_EXPERTS), row)],
        out_shape=[jax.ShapeDtypeStruct((n, N_EXPERTS), jnp.int32),
                   jax.ShapeDtypeStruct((n, N_EXPERTS), F32)],
        compiler_params=_cparams("parallel"),
        name="router",
    )(x2, router_w)


ROW_TILES = D_MODEL // 128


def _to_row_tiles(dst_ref, lead, x):
    for c in range(ROW_TILES):
        dst_ref[lead + (slice(None), c, slice(None))] = x[:, c * 128:(c + 1) * 128]


def _from_row_tiles(src_ref, dtype):
    return jnp.concatenate([src_ref[:, c, :].astype(dtype) for c in range(ROW_TILES)], axis=1)


def _scatter_step(step, n_steps, copies, start_this_step):
    @pl.when(step >= 2)
    def _():
        copies(step - 2, lambda cp: cp.wait())

    start_this_step()

    @pl.when(step == n_steps - 1)
    def _():
        @pl.when(step >= 1)
        def _():
            copies(step - 1, lambda cp: cp.wait())

        copies(step, lambda cp: cp.wait())


def _dispatch_kernel(dest_ref, x_ref, init_ref, xb_ref, buf, sems, *, tm):
    del init_ref
    i = pl.program_id(0)

    def copies(step, fn):
        slot = step % 2

        def one(r, carry):
            for k in range(EXPERT_TOPK):
                dst = dest_ref[(step * tm + r) * EXPERT_TOPK + k]
                fn(pltpu.make_async_copy(buf.at[slot, r], xb_ref.at[dst], sems.at[slot]))
            return carry
        lax.fori_loop(0, tm, one, 0)

    def start():
        _to_row_tiles(buf, (i % 2,), x_ref[...])
        copies(i, lambda cp: cp.start())

    _scatter_step(i, pl.num_programs(0), copies, start)


def _dispatch(x2, dest, n_rows, tm=256):
    n = x2.shape[0]
    init = jnp.zeros((n_rows, ROW_TILES, 128), F32)
    return pl.pallas_call(
        functools.partial(_dispatch_kernel, tm=tm),
        grid_spec=pltpu.PrefetchScalarGridSpec(
            num_scalar_prefetch=1,
            grid=(n // tm,),
            in_specs=[pl.BlockSpec((tm, D_MODEL), lambda i, d: (i, 0)), pl.BlockSpec(memory_space=pl.ANY)],
            out_specs=pl.BlockSpec(memory_space=pl.ANY),
            scratch_shapes=[pltpu.VMEM((2, tm, ROW_TILES, 128), F32), pltpu.SemaphoreType.DMA((2,))],
        ),
        out_shape=jax.ShapeDtypeStruct((n_rows, ROW_TILES, 128), F32),
        input_output_aliases={2: 0},
        compiler_params=_cparams("arbitrary"),
        name="moe_dispatch",
    )(dest, x2, init)


def _moe_kernel(blk_e_ref, n_valid_ref, row_dst_ref, x_ref, wg_ref, wu_ref, wd_ref, y_ref, xs, ybuf, sems):
    del blk_e_ref
    i = pl.program_id(0)

    def copies(step, fn):
        slot = step % 2

        def one(r, carry):
            dst = row_dst_ref[step * MOE_ROWS + r]
            fn(pltpu.make_async_copy(ybuf.at[slot, r], y_ref.at[dst], sems.at[slot]))
            return carry
        lax.fori_loop(0, n_valid_ref[step], one, 0)

    def start():
        @pl.when(n_valid_ref[i] > 0)
        def _():
            xs[...] = _from_row_tiles(x_ref, BF16)
            _to_row_tiles(ybuf, (i % 2,), _swiglu_rows(xs[...], wg_ref, wu_ref, wd_ref, (0,)))
            copies(i, lambda cp: cp.start())

    _scatter_step(i, pl.num_programs(0), copies, start)


def _moe_experts(xb, blk_e, n_valid, row_dst, wg, wu, wd, n_out_rows):
    n_blocks = xb.shape[0] // MOE_ROWS
    weight = lambda shape: pl.BlockSpec((1,) + shape, lambda i, be, nv, rd: (be[i], 0, 0))
    return pl.pallas_call(
        _moe_kernel,
        grid_spec=pltpu.PrefetchScalarGridSpec(
            num_scalar_prefetch=3,
            grid=(n_blocks,),
            in_specs=[pl.BlockSpec((MOE_ROWS, ROW_TILES, 128), lambda i, be, nv, rd: (i, 0, 0)),
                      weight((D_MODEL, D_FF)), weight((D_MODEL, D_FF)), weight((D_FF, D_MODEL))],
            out_specs=pl.BlockSpec(memory_space=pl.ANY),
            scratch_shapes=[pltpu.VMEM((MOE_ROWS, D_MODEL), BF16),
                            pltpu.VMEM((2, MOE_ROWS, ROW_TILES, 128), F32),
                            pltpu.SemaphoreType.DMA((2,))],
        ),
        out_shape=jax.ShapeDtypeStruct((n_out_rows, ROW_TILES, 128), F32),
        compiler_params=_cparams("arbitrary"),
        name="moe_experts",
    )(blk_e, n_valid, row_dst, xb, wg, wu, wd)


def _combine_kernel(x_ref, y0_ref, y1_ref, gate_ref, g_ref, b_ref, o_ref):
    gate = gate_ref[...]
    f = gate[:, 0:1] * _from_row_tiles(y0_ref, F32) + gate[:, 1:2] * _from_row_tiles(y1_ref, F32)
    o_ref[...] = _layer_norm(DN_ALPHA * x_ref[...] + f, g_ref[...], b_ref[...])


def _combine(x2, y_slots, gate, ln_g, ln_b, tm=512):
    n = x2.shape[0]
    row = lambda i: (i, 0)
    vec = pl.BlockSpec((1, D_MODEL), lambda i: (0, 0))
    return pl.pallas_call(
        _combine_kernel,
        grid=(n // tm,),
        in_specs=[pl.BlockSpec((tm, D_MODEL), row),
                  pl.BlockSpec((tm, ROW_TILES, 128), lambda i: (i, 0, 0)),
                  pl.BlockSpec((tm, ROW_TILES, 128), lambda i: (n // tm + i, 0, 0)),
                  pl.BlockSpec((tm, EXPERT_TOPK), row), vec, vec],
        out_specs=pl.BlockSpec((tm, D_MODEL), row),
        out_shape=jax.ShapeDtypeStruct((n, D_MODEL), F32),
        compiler_params=_cparams("parallel"),
        name="combine_ln",
    )(x2, y_slots, y_slots, gate, ln_g, ln_b)


def _moe_layer(x2, router_w, wg, wu, wd, ln_g, ln_b):
    n = x2.shape[0]
    nk = n * EXPERT_TOPK
    top_e, gate = _router(x2, router_w)
    top_e = top_e[:, :EXPERT_TOPK]
    gate = gate[:, :EXPERT_TOPK]
    flat_e = top_e.reshape(-1)
    onehot = (flat_e[:, None] == jnp.arange(N_EXPERTS, dtype=jnp.int32)[None, :]).astype(jnp.int32)
    counts = jnp.sum(onehot, axis=0)
    rank = jnp.sum((jnp.cumsum(onehot, axis=0) - onehot) * onehot, axis=1)
    padded = (counts + MOE_ROWS - 1) // MOE_ROWS * MOE_ROWS
    pad_end = jnp.cumsum(padded)
    pad_start = pad_end - padded
    dest = (pad_start[flat_e] + rank).astype(jnp.int32)
    n_blocks = -(-nk // MOE_ROWS) + N_EXPERTS
    n_rows = n_blocks * MOE_ROWS
    blk_start = jnp.arange(n_blocks, dtype=jnp.int32) * MOE_ROWS
    blk_e = jnp.sum((blk_start[:, None] >= pad_end[None, :]).astype(jnp.int32), axis=1)
    valid_end = jnp.concatenate([pad_start + counts, jnp.zeros((1,), jnp.int32)])
    n_valid = jnp.clip(valid_end[blk_e] - blk_start, 0, MOE_ROWS).astype(jnp.int32)
    blk_e = jnp.minimum(blk_e, N_EXPERTS - 1).astype(jnp.int32)
    j = jnp.arange(nk, dtype=jnp.int32)
    row_dst = jnp.zeros((n_rows,), jnp.int32).at[dest].set((j % EXPERT_TOPK) * n + j // EXPERT_TOPK)

    xb = _dispatch(x2, dest, n_rows)
    y_slots = _moe_experts(xb, blk_e, n_valid, row_dst, wg, wu, wd, nk)
    return _combine(x2, y_slots, gate, ln_g, ln_b)


def _rope_tables(t):
    half = HEAD_DIM // 2
    inv = ROPE_THETA ** (-jnp.arange(half, dtype=F32) / half)
    ang = jnp.arange(t).astype(F32)[:, None] * inv[None, :]
    cos = jnp.cos(ang)
    sin = jnp.sin(ang)
    cos_t = jnp.tile(jnp.concatenate([cos, cos], -1), (1, N_HEADS))
    sin_t = jnp.tile(jnp.concatenate([-sin, sin], -1), (1, N_HEADS))
    return cos_t, sin_t


def _pad_rows(w, start):
    out = jnp.zeros((LORA_DIM, BRANCH_DIM), F32)
    return lax.dynamic_update_slice(out, w, (start, 0)).astype(BF16)


def _mixer(x2, b, t, cos_t, sin_t, w_in, conv_w, shift_mu, decay_w0, decay_w2, aaa_a0, aaa_w2, gate_w2,
           k_k, k_a, r_k, gn_g, gn_b, w_branch, w_out, ln_g, ln_b):
    w_in = w_in.astype(BF16)
    mix, lora, gates = _in_proj(x2, w_in[:, :MIX_COLS], w_in[:, MIX_COLS:MIX_COLS + LORA_DIM],
                                w_in[:, MIX_COLS + LORA_DIM:])
    mix3 = mix.reshape(b, t, MIX_COLS)
    lora3 = lora.reshape(b, t, LORA_DIM)

    o_conv = _short_conv(mix3, conv_w)

    q3, k3, v3, kmean = _att_prep(mix3, cos_t, sin_t)
    o_att = _moba(q3, k3, v3, kmean.reshape(b, t // MOBA_BLOCK, BRANCH_DIM)).reshape(b * t, BRANCH_DIM)

    row = lambda a: a.reshape(1, -1)
    p = {
        "mu_r": row(shift_mu[0:BRANCH_DIM]), "mu_k": row(shift_mu[BRANCH_DIM:2 * BRANCH_DIM]),
        "mu_v": row(shift_mu[2 * BRANCH_DIM:3 * BRANCH_DIM]), "mu_l": row(shift_mu[3 * BRANCH_DIM:]),
        "w0": row(decay_w0), "w2": _pad_rows(decay_w2, 0),
        "a0": row(aaa_a0), "a2": _pad_rows(aaa_w2, DECAY_LORA),
        "g2": _pad_rows(gate_w2, DECAY_LORA + AAA_LORA),
        "k_k": row(k_k), "k_a": row(k_a), "r_k": row(r_k), "gn_g": row(gn_g), "gn_b": row(gn_b),
    }
    o_rwkv = _rwkv(mix3, lora3, p)

    return _merge(o_conv.reshape(b * t, BRANCH_DIM), o_att, o_rwkv.reshape(b * t, BRANCH_DIM), gates, x2,
                  w_branch.astype(BF16), w_out.astype(BF16), row(ln_g), row(ln_b))


def kernel(x, w_in, conv_w, shift_mu, decay_w0, decay_w2, aaa_a0, aaa_w2, gate_w2, k_k, k_a, r_k, gn_g, gn_b,
           w_branch, w_out, ln1_g, ln1_b, ln2_g, ln2_b, ffn_w_gate, ffn_w_up, ffn_w_down, router_w,
           moe_w_gate, moe_w_up, moe_w_down):
    b, t, d = x.shape
    x2 = x.reshape(b * t, d)
    cos_t, sin_t = _rope_tables(t)
    row = lambda a: a.reshape(1, -1)
    for l in range(DEPTH):
        x2 = _mixer(x2, b, t, cos_t, sin_t, w_in[l], conv_w[l], shift_mu[l], decay_w0[l], decay_w2[l],
                    aaa_a0[l], aaa_w2[l], gate_w2[l], k_k[l], k_a[l], r_k[l], gn_g[l], gn_b[l],
                    w_branch[l], w_out[l], ln1_g[l], ln1_b[l])
        j = l // 2
        if l % 2 == 0:
            x2 = _ffn(x2, ffn_w_gate[j].astype(BF16), ffn_w_up[j].astype(BF16), ffn_w_down[j].astype(BF16),
                      row(ln2_g[l]), row(ln2_b[l]))
        else:
            x2 = _moe_layer(x2, router_w[j], moe_w_gate[j].astype(BF16), moe_w_up[j].astype(BF16),
                            moe_w_down[j].astype(BF16), row(ln2_g[l]), row(ln2_b[l]))
    return x2.reshape(b, t, d)
```

```python
import functools

import jax
import jax.numpy as jnp
from jax import lax
from jax.experimental import pallas as pl
from jax.experimental.pallas import tpu as pltpu

F32 = jnp.float32
BF16 = jnp.bfloat16

D_MODEL = 1024
HEAD_DIM = 64
BRANCH_DIM = 512
N_HEADS = BRANCH_DIM // HEAD_DIM
DECAY_LORA = 64
AAA_LORA = 64
GATE_LORA = 128
LORA_DIM = DECAY_LORA + AAA_LORA + GATE_LORA
MIX_COLS = 9 * BRANCH_DIM
GATE_COLS = 3 * D_MODEL
MOBA_BLOCK = 256
MOBA_TOPK = 3
ROPE_THETA = 10000.0
D_FF = 2816
N_EXPERTS = 8
EXPERT_TOPK = 2
MOE_ROWS = 256
LN_EPS = 1e-5
GN_EPS = 64e-5
DEPTH = 2
DN_ALPHA = (2 * DEPTH) ** 0.25
LOG2_E = 1.4426950408889634

VMEM_LIMIT_BYTES = 56 * 1024 * 1024
SUBLANES = 8
RWKV_TILE = 256
RWKV_CHUNK = 64
FF_CHUNK = 256
PAIR_WIDTH = 2 * HEAD_DIM
HEAD_SHIFT = HEAD_DIM.bit_length() - 1
CHUNK_SHIFT = RWKV_CHUNK.bit_length() - 1


def _cparams(*sem):
    return pltpu.CompilerParams(dimension_semantics=sem, vmem_limit_bytes=VMEM_LIMIT_BYTES)


def _vmem_whole():
    return pl.BlockSpec(memory_space=pltpu.VMEM)


def _dot(a, b):
    return jnp.dot(a.astype(BF16), b.astype(BF16), preferred_element_type=F32)


def _dot_nt(a, b):
    return lax.dot_general(a.astype(BF16), b.astype(BF16), (((1,), (1,)), ((), ())),
                           preferred_element_type=F32)


def _dot_tn(a, b):
    return lax.dot_general(a.astype(BF16), b.astype(BF16), (((0,), (0,)), ((), ())),
                           preferred_element_type=F32)


def _dot_f32(a, b):
    return jnp.dot(a, b, preferred_element_type=F32, precision=lax.Precision.HIGHEST)


def _split_bf16(x):
    hi = x.astype(BF16)
    return hi, (x - hi.astype(F32)).astype(BF16)


def _dot_split(a, b, *, data_on_left):
    hi, lo = _split_bf16(a if data_on_left else b)
    if data_on_left:
        return jnp.dot(hi, b, preferred_element_type=F32) + jnp.dot(lo, b, preferred_element_type=F32)
    return jnp.dot(a, hi, preferred_element_type=F32) + jnp.dot(a, lo, preferred_element_type=F32)


def _sigmoid(x):
    return 1.0 / (1.0 + jnp.exp(-x))


def _layer_norm(y, g, b):
    mu = jnp.mean(y, axis=-1, keepdims=True)
    d = y - mu
    var = jnp.mean(d * d, axis=-1, keepdims=True)
    return d * lax.rsqrt(var + LN_EPS) * g + b


def _shift_rows(u, halo, n):
    out = pltpu.roll(u, n, axis=0)
    row = lax.broadcasted_iota(jnp.int32, u.shape, 0)
    for r in range(n):
        src = halo[SUBLANES - n + r:SUBLANES - n + r + 1, :]
        out = jnp.where(row == r, src, out)
    return out


def _halo_index(tile_rows):
    step = tile_rows // SUBLANES

    def index(col):
        return lambda b, i: (b, jnp.maximum(i * step - 1, 0), col)
    return index


def _in_proj_kernel(x_ref, wm_ref, wl_ref, wg_ref, mix_ref, lora_ref, gate_ref):
    xb = x_ref[...].astype(BF16)
    for j in range(0, MIX_COLS, BRANCH_DIM):
        mix_ref[:, j:j + BRANCH_DIM] = jnp.dot(xb, wm_ref[:, j:j + BRANCH_DIM],
                                               preferred_element_type=F32)
    lora_ref[...] = jnp.dot(xb, wl_ref[...], preferred_element_type=F32)
    for j in range(0, GATE_COLS, BRANCH_DIM):
        g = jnp.dot(xb, wg_ref[:, j:j + BRANCH_DIM], preferred_element_type=F32)
        gate_ref[:, j:j + BRANCH_DIM] = _sigmoid(g).astype(gate_ref.dtype)


def _in_proj(x2, w_mix, w_lora, w_gate, tm=256):
    n = x2.shape[0]
    row = lambda i: (i, 0)
    return pl.pallas_call(
        _in_proj_kernel,
        grid=(n // tm,),
        in_specs=[pl.BlockSpec((tm, D_MODEL), row), _vmem_whole(), _vmem_whole(), _vmem_whole()],
        out_specs=[pl.BlockSpec((tm, MIX_COLS), row), pl.BlockSpec((tm, LORA_DIM), row),
                   pl.BlockSpec((tm, GATE_COLS), row)],
        out_shape=[jax.ShapeDtypeStruct((n, MIX_COLS), F32), jax.ShapeDtypeStruct((n, LORA_DIM), F32),
                   jax.ShapeDtypeStruct((n, GATE_COLS), BF16)],
        compiler_params=_cparams("parallel"),
        name="in_proj",
    )(x2, w_mix, w_lora, w_gate)


def _conv_kernel(ch_ref, cb_ref, cc_ref, hh_ref, hc_ref, w_ref, o_ref):
    first = pl.program_id(1) == 0
    u = cc_ref[0] * ch_ref[0]
    halo = jnp.where(first, 0.0, hc_ref[0] * hh_ref[0])
    w = w_ref[...]
    y = w[2:3, :] * u + w[1:2, :] * _shift_rows(u, halo, 1) + w[0:1, :] * _shift_rows(u, halo, 2)
    o_ref[0] = (cb_ref[0] * y).astype(o_ref.dtype)


def _short_conv(mix3, conv_w, tq=512):
    b, t, _ = mix3.shape
    blk = lambda col: pl.BlockSpec((1, tq, BRANCH_DIM), lambda bi, i: (bi, i, col))
    halo = _halo_index(tq)
    hblk = lambda col: pl.BlockSpec((1, SUBLANES, BRANCH_DIM), halo(col))
    return pl.pallas_call(
        _conv_kernel,
        grid=(b, t // tq),
        in_specs=[blk(0), blk(1), blk(2), hblk(0), hblk(2),
                  pl.BlockSpec((3, BRANCH_DIM), lambda bi, i: (0, 0))],
        out_specs=pl.BlockSpec((1, tq, BRANCH_DIM), lambda bi, i: (bi, i, 0)),
        out_shape=jax.ShapeDtypeStruct((b, t, BRANCH_DIM), BF16),
        compiler_params=_cparams("parallel", "parallel"),
        name="short_conv",
    )(mix3, mix3, mix3, mix3, mix3, conv_w)


def _att_prep_kernel(q_ref, k_ref, v_ref, cos_ref, sin_ref, qo_ref, ko_ref, vo_ref, km_ref):
    cos = cos_ref[...]
    sin = sin_ref[...]
    lane = lax.broadcasted_iota(jnp.int32, cos.shape, 1)
    first_half = (lane & (HEAD_DIM - 1)) < (HEAD_DIM // 2)

    def rope(x):
        swapped = jnp.where(first_half, pltpu.roll(x, BRANCH_DIM - HEAD_DIM // 2, axis=1),
                            pltpu.roll(x, HEAD_DIM // 2, axis=1))
        return x * cos + swapped * sin

    kr = rope(k_ref[0])
    qo_ref[0] = rope(q_ref[0])
    ko_ref[0] = kr.astype(BF16)
    vo_ref[0, 0] = v_ref[0].T.astype(BF16)
    km_ref[0, 0] = jnp.mean(kr, axis=0, keepdims=True)


def _att_prep(mix3, cos_t, sin_t):
    b, t, _ = mix3.shape
    tq = MOBA_BLOCK
    nb = t // tq
    blk = lambda col: pl.BlockSpec((1, tq, BRANCH_DIM), lambda bi, i: (bi, i, col))
    tab = pl.BlockSpec((tq, BRANCH_DIM), lambda bi, i: (i, 0))
    out = pl.BlockSpec((1, tq, BRANCH_DIM), lambda bi, i: (bi, i, 0))
    return pl.pallas_call(
        _att_prep_kernel,
        grid=(b, nb),
        in_specs=[blk(3), blk(4), blk(5), tab, tab],
        out_specs=[out, out, pl.BlockSpec((1, 1, BRANCH_DIM, tq), lambda bi, i: (bi, i, 0, 0)),
                   pl.BlockSpec((1, 1, 1, BRANCH_DIM), lambda bi, i: (bi, i, 0, 0))],
        out_shape=[jax.ShapeDtypeStruct((b, t, BRANCH_DIM), F32),
                   jax.ShapeDtypeStruct((b, t, BRANCH_DIM), BF16),
                   jax.ShapeDtypeStruct((b, nb, BRANCH_DIM, tq), BF16),
                   jax.ShapeDtypeStruct((b, nb, 1, BRANCH_DIM), F32)],
        compiler_params=_cparams("parallel", "parallel"),
        name="att_prep",
    )(mix3, mix3, mix3, cos_t, sin_t)


def _moba_kernel(q_ref, k_ref, vt_ref, km_ref, o_ref, *, nb, heads):
    i = pl.program_id(2)
    bs = MOBA_BLOCK
    width = heads * HEAD_DIM
    q = q_ref[0]
    km = km_ref[0]
    head_of_lane = lax.broadcasted_iota(jnp.int32, (1, width), 1) >> HEAD_SHIFT
    q_heads = jnp.concatenate([jnp.where(head_of_lane == g, q, 0.0) for g in range(heads)], axis=0)
    q_all = (q_heads * (HEAD_DIM ** -0.5 * LOG2_E)).astype(BF16)

    blk = lax.broadcasted_iota(jnp.int32, (nb, heads * bs), 0)
    valid = blk < i
    km_hi, km_lo = _split_bf16(km)
    q_hi, q_lo = _split_bf16(q_heads)
    hi_part = _dot_nt(jnp.concatenate([km_hi, km_lo], axis=0), q_hi)
    gate = hi_part[:nb] + hi_part[nb:] + _dot_nt(km_hi, q_lo)
    gate = jnp.where(valid, gate, -jnp.inf)
    rank = jnp.zeros((nb, heads * bs), jnp.int32)
    for m in range(nb):
        gm = gate[m:m + 1, :]
        beats = (gm > gate) | ((gm == gate) & (blk > m))
        rank = rank + beats.astype(jnp.int32)
    chosen = jnp.where(valid & (rank < MOBA_TOPK), (1 << blk).astype(F32), 0.0)
    picks = jnp.sum(chosen, axis=0, keepdims=True).astype(jnp.int32)

    def block_update(j, keep, carry):
        kb = k_ref[0, pl.ds(pl.multiple_of(j * bs, bs), bs), :]
        m, l, accs = carry
        s = lax.dot_general(kb, q_all, (((1,), (1,)), ((), ())), preferred_element_type=F32)
        s = jnp.where(keep, s, -jnp.inf)
        m_new = jnp.maximum(m, jnp.max(s, axis=0, keepdims=True))
        alpha = jnp.exp2(m - m_new)
        p = jnp.exp2(s - m_new)
        l_new = alpha * l + jnp.sum(p, axis=0, keepdims=True)
        p = p.astype(BF16)
        new_acc = []
        for g in range(heads):
            cols = slice(g * bs, (g + 1) * bs)
            vt = vt_ref[0, j, g * HEAD_DIM:(g + 1) * HEAD_DIM, :]
            new_acc.append(alpha[:, cols] * accs[g] + jnp.dot(vt, p[:, cols], preferred_element_type=F32))
        return m_new, l_new, tuple(new_acc)

    key_id = lax.broadcasted_iota(jnp.int32, (bs, heads * bs), 0)
    query_id = lax.broadcasted_iota(jnp.int32, (bs, heads * bs), 1) & (bs - 1)
    init = (jnp.full((1, heads * bs), -jnp.inf, F32), jnp.zeros((1, heads * bs), F32),
            tuple(jnp.zeros((HEAD_DIM, bs), F32) for _ in range(heads)))
    carry = block_update(i, key_id <= query_id, init)

    def body(j, carry):
        return block_update(j, ((picks >> j) & 1) == 1, carry)

    _, l, accs = lax.fori_loop(0, i, body, carry)
    out_t = jnp.concatenate([accs[g] / l[:, g * bs:(g + 1) * bs] for g in range(heads)], axis=0)
    o_ref[0] = out_t.T.astype(o_ref.dtype)


def _moba(q3, k3, vt4, kmean, heads=4):
    b, t, _ = q3.shape
    nb = t // MOBA_BLOCK
    width = heads * HEAD_DIM
    qblk = pl.BlockSpec((1, MOBA_BLOCK, width), lambda bi, hi, i: (bi, i, hi))
    return pl.pallas_call(
        functools.partial(_moba_kernel, nb=nb, heads=heads),
        grid=(b, N_HEADS // heads, nb),
        in_specs=[qblk, pl.BlockSpec((1, t, width), lambda bi, hi, i: (bi, 0, hi)),
                  pl.BlockSpec((1, nb, width, MOBA_BLOCK), lambda bi, hi, i: (bi, 0, hi, 0)),
                  pl.BlockSpec((1, nb, width), lambda bi, hi, i: (bi, 0, hi))],
        out_specs=qblk,
        out_shape=jax.ShapeDtypeStruct((b, t, BRANCH_DIM), BF16),
        compiler_params=_cparams("parallel", "parallel", "arbitrary"),
        name="moba",
    )(q3, k3, vt4, kmean)


def _rwkv_kernel(r_ref, k_ref, v_ref, lo_ref, hr_ref, hk_ref, hv_ref, hl_ref,
                 mu_r_ref, mu_k_ref, mu_v_ref, mu_l_ref, w0_ref, w2_ref, a0_ref, a2_ref, g2_ref,
                 kk_ref, ka_ref, rk_ref, gng_ref, gnb_ref, o_ref, state_ref):
    i = pl.program_id(1)
    first = i == 0
    tq, ch = RWKV_TILE, RWKV_CHUNK
    hd, bd = HEAD_DIM, BRANCH_DIM

    @pl.when(first)
    def _():
        state_ref[...] = jnp.zeros_like(state_ref)

    def lerp(x_ref, h_ref, mu_ref):
        x = x_ref[0]
        halo = jnp.where(first, 0.0, h_ref[0])
        return x + (_shift_rows(x, halo, 1) - x) * mu_ref[...]

    r = lerp(r_ref, hr_ref, mu_r_ref)
    k = lerp(k_ref, hk_ref, mu_k_ref)
    v = lerp(v_ref, hv_ref, mu_v_ref)
    lo = lerp(lo_ref, hl_ref, mu_l_ref)

    z = w0_ref[...] + _dot(jnp.tanh(lo), w2_ref[...])
    ew = jnp.exp(-0.5) * _sigmoid(z)
    a = _sigmoid(a0_ref[...] + _dot(lo, a2_ref[...]))
    g = _dot(_sigmoid(lo), g2_ref[...])

    half = bd // 2
    lane_r = lax.broadcasted_iota(jnp.int32, (half, half), 0)
    lane_c = lax.broadcasted_iota(jnp.int32, (half, half), 1)
    ones_blocks = jnp.where((lane_r >> HEAD_SHIFT) == (lane_c >> HEAD_SHIFT), 1.0, 0.0).astype(BF16)

    def head_sum(x):
        return jnp.concatenate([_dot_split(x[:, :half], ones_blocks, data_on_left=True),
                                _dot_split(x[:, half:], ones_blocks, data_on_left=True)], axis=1)

    kk = k * kk_ref[...]
    kk = kk / jnp.maximum(jnp.sqrt(head_sum(kk * kk)), 1e-12)
    k2 = k * (1.0 + (a - 1.0) * ka_ref[...])
    bonus = head_sum(r * k2 * rk_ref[...]) * v

    t_r = lax.broadcasted_iota(jnp.int32, (tq, tq), 0)
    t_c = lax.broadcasted_iota(jnp.int32, (tq, tq), 1)
    same_chunk = (t_r >> CHUNK_SHIFT) == (t_c >> CHUNK_SHIFT)
    cs = _dot_split(jnp.where(same_chunk & (t_c <= t_r), 1.0, 0.0).astype(BF16), ew, data_on_left=False)
    cs_end = jnp.concatenate(
        [jnp.broadcast_to(cs[c * ch + ch - 1:c * ch + ch, :], (ch, bd)) for c in range(tq // ch)], axis=0)

    kka = kk * a
    a_t = -kk * jnp.exp(ew - cs)
    b_hat = (kka * jnp.exp(cs)).astype(BF16)
    k_hat = (k2 * jnp.exp(cs)).astype(BF16)
    r_t = r * jnp.exp(-cs)
    b_e = (kka * jnp.exp(cs - cs_end)).astype(BF16)
    k_e = (k2 * jnp.exp(cs - cs_end)).astype(BF16)
    p_c = jnp.exp(-cs_end)
    v_b = v.astype(BF16)

    strict = jnp.where(same_chunk & (t_c < t_r), 1.0, 0.0)
    incl = jnp.where(same_chunk & (t_c <= t_r), 1.0, 0.0)
    eye = jnp.where(t_c == t_r, 1.0, 0.0)
    pw = PAIR_WIDTH
    second = (lax.broadcasted_iota(jnp.int32, (1, pw), 1) >> HEAD_SHIFT) == 1
    same_head = ((lax.broadcasted_iota(jnp.int32, (pw, pw), 0) >> HEAD_SHIFT)
                 == (lax.broadcasted_iota(jnp.int32, (pw, pw), 1) >> HEAD_SHIFT))

    y_pairs = []
    for p in range(N_HEADS // 2):
        sl = slice(p * pw, (p + 1) * pw)
        a_p, r_p, bh_p, kh_p, v_p = a_t[:, sl], r_t[:, sl], b_hat[:, sl], k_hat[:, sl], v_b[:, sl]
        solved, applied = [], []
        for hm in (~second, second):
            a_h = jnp.where(hm, a_p, 0.0).astype(BF16)
            r_h = jnp.where(hm, r_p, 0.0).astype(BF16)
            l_ab = strict * _dot_nt(a_h, bh_p)
            l_ak = strict * _dot_nt(a_h, kh_p)
            m_rb = incl * _dot_nt(r_h, bh_p)
            m_rk = incl * _dot_nt(r_h, kh_p)
            t_inv = eye + l_ab
            power = l_ab
            n = 2
            while n < ch:
                power = _dot(power, power)
                t_inv = t_inv + _dot(t_inv, power)
                n *= 2
            sol = _dot(t_inv, jnp.concatenate([a_p.astype(BF16), _dot(l_ak, v_p).astype(BF16)], axis=1))
            app = _dot(m_rb, sol)
            solved.append(sol)
            applied.append(jnp.concatenate([app[:, :pw], app[:, pw:] + _dot(m_rk, v_p)], axis=1))
        a_til = jnp.where(second, solved[1][:, :pw], solved[0][:, :pw]).astype(BF16)
        u_0 = jnp.where(second, solved[1][:, pw:], solved[0][:, pw:]).astype(BF16)
        r_til = (jnp.where(second, applied[1][:, :pw], applied[0][:, :pw]) + r_p).astype(BF16)
        y_hat = jnp.where(second, applied[1][:, pw:], applied[0][:, pw:])

        st = state_ref[p]
        ys = []
        for c in range(tq // ch):
            rows = slice(c * ch, (c + 1) * ch)
            be_c = b_e[rows, sl]
            g_c = jnp.where(same_head, _dot_tn(a_til[rows], be_c), 0.0)
            h_c = jnp.where(same_head, _dot_tn(u_0[rows], be_c) + _dot_tn(v_p[rows], k_e[rows, sl]), 0.0)
            ys.append(_dot_nt(r_til[rows], st) + y_hat[rows])
            st = st * p_c[c * ch:c * ch + 1, sl] + _dot(st, g_c) + h_c
        state_ref[p] = st
        y_pairs.append(jnp.concatenate(ys, axis=0))
    y = jnp.concatenate(y_pairs, axis=1)

    mu = head_sum(y) * (1.0 / hd)
    d = y - mu
    var = head_sum(d * d) * (1.0 / hd)
    y = d * lax.rsqrt(var + GN_EPS) * gng_ref[...] + gnb_ref[...]
    o_ref[0] = ((y + bonus) * g).astype(o_ref.dtype)


def _rwkv(mix3, lora3, p):
    b, t, _ = mix3.shape
    tq = RWKV_TILE
    halo = _halo_index(tq)
    blk = lambda col: pl.BlockSpec((1, tq, BRANCH_DIM), lambda bi, i: (bi, i, col))
    hblk = lambda col: pl.BlockSpec((1, SUBLANES, BRANCH_DIM), halo(col))
    vec = lambda width: pl.BlockSpec((1, width), lambda bi, i: (0, 0))
    mat = lambda rows: pl.BlockSpec((rows, BRANCH_DIM), lambda bi, i: (0, 0))
    return pl.pallas_call(
        _rwkv_kernel,
        grid=(b, t // tq),
        in_specs=[blk(6), blk(7), blk(8), pl.BlockSpec((1, tq, LORA_DIM), lambda bi, i: (bi, i, 0)),
                  hblk(6), hblk(7), hblk(8), pl.BlockSpec((1, SUBLANES, LORA_DIM), halo(0)),
                  vec(BRANCH_DIM), vec(BRANCH_DIM), vec(BRANCH_DIM), vec(LORA_DIM),
                  vec(BRANCH_DIM), mat(LORA_DIM), vec(BRANCH_DIM), mat(LORA_DIM), mat(LORA_DIM),
                  vec(BRANCH_DIM), vec(BRANCH_DIM), vec(BRANCH_DIM), vec(BRANCH_DIM), vec(BRANCH_DIM)],
        out_specs=pl.BlockSpec((1, tq, BRANCH_DIM), lambda bi, i: (bi, i, 0)),
        out_shape=jax.ShapeDtypeStruct((b, t, BRANCH_DIM), BF16),
        scratch_shapes=[pltpu.VMEM((N_HEADS // 2, PAIR_WIDTH, PAIR_WIDTH), F32)],
        compiler_params=_cparams("parallel", "arbitrary"),
        name="rwkv7",
    )(mix3, mix3, mix3, lora3, mix3, mix3, mix3, lora3,
      p["mu_r"], p["mu_k"], p["mu_v"], p["mu_l"], p["w0"], p["w2"], p["a0"], p["a2"], p["g2"],
      p["k_k"], p["k_a"], p["r_k"], p["gn_g"], p["gn_b"])


def _merge_kernel(oc_ref, oa_ref, or_ref, gate_ref, x_ref, wb_ref, wo_ref, g_ref, b_ref, o_ref):
    d = D_MODEL
    bd = BRANCH_DIM
    merged = (gate_ref[:, 0:d] * jnp.dot(oc_ref[...], wb_ref[0:bd, :], preferred_element_type=F32)
              + gate_ref[:, d:2 * d] * jnp.dot(oa_ref[...], wb_ref[bd:2 * bd, :], preferred_element_type=F32)
              + gate_ref[:, 2 * d:3 * d] * jnp.dot(or_ref[...], wb_ref[2 * bd:3 * bd, :],
                                                   preferred_element_type=F32))
    h = jnp.dot(merged.astype(BF16), wo_ref[...], preferred_element_type=F32)
    o_ref[...] = _layer_norm(DN_ALPHA * x_ref[...] + h, g_ref[...], b_ref[...])


def _merge(o_conv, o_att, o_rwkv, gates, x2, w_branch, w_out, ln_g, ln_b, tm=256):
    n = x2.shape[0]
    row = lambda i: (i, 0)
    vec = pl.BlockSpec((1, D_MODEL), lambda i: (0, 0))
    return pl.pallas_call(
        _merge_kernel,
        grid=(n // tm,),
        in_specs=[pl.BlockSpec((tm, BRANCH_DIM), row)] * 3
                 + [pl.BlockSpec((tm, GATE_COLS), row), pl.BlockSpec((tm, D_MODEL), row),
                    _vmem_whole(), _vmem_whole(), vec, vec],
        out_specs=pl.BlockSpec((tm, D_MODEL), row),
        out_shape=jax.ShapeDtypeStruct((n, D_MODEL), F32),
        compiler_params=_cparams("parallel"),
        name="merge_ln",
    )(o_conv, o_att, o_rwkv, gates, x2, w_branch, w_out, ln_g, ln_b)


def _swiglu_rows(xb, wg_ref, wu_ref, wd_ref, lead):
    acc = jnp.zeros((xb.shape[0], D_MODEL), F32)
    for c in range(0, D_FF, FF_CHUNK):
        cols = slice(c, c + FF_CHUNK)
        hg = jnp.dot(xb, wg_ref[lead + (slice(None), cols)], preferred_element_type=F32)
        hu = jnp.dot(xb, wu_ref[lead + (slice(None), cols)], preferred_element_type=F32)
        hidden = hg * _sigmoid(hg) * hu
        acc = acc + jnp.dot(hidden.astype(BF16), wd_ref[lead + (cols, slice(None))],
                            preferred_element_type=F32)
    return acc


def _ffn_kernel(x_ref, wg_ref, wu_ref, wd_ref, g_ref, b_ref, o_ref):
    x = x_ref[...]
    f = _swiglu_rows(x.astype(BF16), wg_ref, wu_ref, wd_ref, ())
    o_ref[...] = _layer_norm(DN_ALPHA * x + f, g_ref[...], b_ref[...])


def _ffn(x2, wg, wu, wd, ln_g, ln_b, tm=256):
    n = x2.shape[0]
    row = lambda i: (i, 0)
    vec = pl.BlockSpec((1, D_MODEL), lambda i: (0, 0))
    return pl.pallas_call(
        _ffn_kernel,
        grid=(n // tm,),
        in_specs=[pl.BlockSpec((tm, D_MODEL), row), _vmem_whole(), _vmem_whole(), _vmem_whole(), vec, vec],
        out_specs=pl.BlockSpec((tm, D_MODEL), row),
        out_shape=jax.ShapeDtypeStruct((n, D_MODEL), F32),
        compiler_params=_cparams("parallel"),
        name="ffn_ln",
    )(x2, wg, wu, wd, ln_g, ln_b)


def _router_kernel(x_ref, w_ref, e_ref, g_ref):
    logits = _dot_f32(x_ref[...], w_ref[...])
    lane = lax.broadcasted_iota(jnp.int32, logits.shape, 1).astype(F32)
    m1 = jnp.max(logits, axis=1, keepdims=True)
    e1 = jnp.min(jnp.where(logits == m1, lane, float(N_EXPERTS)), axis=1, keepdims=True)
    rest = jnp.where(lane == e1, -jnp.inf, logits)
    m2 = jnp.max(rest, axis=1, keepdims=True)
    e2 = jnp.min(jnp.where(rest == m2, lane, float(N_EXPERTS)), axis=1, keepdims=True)
    w2 = jnp.exp(m2 - m1)
    denom = 1.0 + w2
    e_ref[...] = jnp.where(lane == 0.0, e1, e2).astype(jnp.int32)
    g_ref[...] = jnp.where(lane == 0.0, 1.0 / denom, w2 / denom)


def _router(x2, router_w, tm=512):
    n = x2.shape[0]
    row = lambda i: (i, 0)
    return pl.pallas_call(
        _router_kernel,
        grid=(n // tm,),
        in_specs=[pl.BlockSpec((tm, D_MODEL), row), pl.BlockSpec((D_MODEL, N_EXPERTS), lambda i: (0, 0))],
        out_specs=[pl.BlockSpec((tm, N_EXPERTS), row), pl.BlockSpec((tm, N_EXPERTS), row)],
        out_shape=[jax.ShapeDtypeStruct((n, N_EXPERTS), jnp.int32),
                   jax.ShapeDtypeStruct((n, N_EXPERTS), F32)],
        compiler_params=_cparams("parallel"),
        name="router",
    )(x2, router_w)


ROW_TILES = D_MODEL // 128


def _to_row_tiles(dst_ref, lead, x):
    for c in range(ROW_TILES):
        dst_ref[lead + (slice(None), c, slice(None))] = x[:, c * 128:(c + 1) * 128]


def _from_row_tiles(src_ref, dtype):
    return jnp.concatenate([src_ref[:, c, :].astype(dtype) for c in range(ROW_TILES)], axis=1)


def _scatter_step(step, n_steps, copies, start_this_step):
    @pl.when(step >= 2)
    def _():
        copies(step - 2, lambda cp: cp.wait())

    start_this_step()

    @pl.when(step == n_steps - 1)
    def _():
        @pl.when(step >= 1)
        def _():
            copies(step - 1, lambda cp: cp.wait())

        copies(step, lambda cp: cp.wait())


def _dispatch_kernel(dest_ref, x_ref, init_ref, xb_ref, buf, sems, *, tm):
    del init_ref
    i = pl.program_id(0)

    def copies(step, fn):
        slot = step % 2

        def one(r, carry):
            for k in range(EXPERT_TOPK):
                dst = dest_ref[(step * tm + r) * EXPERT_TOPK + k]
                fn(pltpu.make_async_copy(buf.at[slot, r], xb_ref.at[dst], sems.at[slot]))
            return carry
        lax.fori_loop(0, tm, one, 0)

    def start():
        _to_row_tiles(buf, (i % 2,), x_ref[...])
        copies(i, lambda cp: cp.start())

    _scatter_step(i, pl.num_programs(0), copies, start)


def _dispatch(x2, dest, n_rows, tm=256):
    n = x2.shape[0]
    init = jnp.zeros((n_rows, ROW_TILES, 128), F32)
    return pl.pallas_call(
        functools.partial(_dispatch_kernel, tm=tm),
        grid_spec=pltpu.PrefetchScalarGridSpec(
            num_scalar_prefetch=1,
            grid=(n // tm,),
            in_specs=[pl.BlockSpec((tm, D_MODEL), lambda i, d: (i, 0)), pl.BlockSpec(memory_space=pl.ANY)],
            out_specs=pl.BlockSpec(memory_space=pl.ANY),
            scratch_shapes=[pltpu.VMEM((2, tm, ROW_TILES, 128), F32), pltpu.SemaphoreType.DMA((2,))],
        ),
        out_shape=jax.ShapeDtypeStruct((n_rows, ROW_TILES, 128), F32),
        input_output_aliases={2: 0},
        compiler_params=_cparams("arbitrary"),
        name="moe_dispatch",
    )(dest, x2, init)


def _moe_kernel(blk_e_ref, n_valid_ref, row_dst_ref, x_ref, wg_ref, wu_ref, wd_ref, y_ref, xs, ybuf, sems):
    del blk_e_ref
    i = pl.program_id(0)

    def copies(step, fn):
        slot = step % 2

        def one(r, carry):
            dst = row_dst_ref[step * MOE_ROWS + r]
            fn(pltpu.make_async_copy(ybuf.at[slot, r], y_ref.at[dst], sems.at[slot]))
            return carry
        lax.fori_loop(0, n_valid_ref[step], one, 0)

    def start():
        @pl.when(n_valid_ref[i] > 0)
        def _():
            xs[...] = _from_row_tiles(x_ref, BF16)
            _to_row_tiles(ybuf, (i % 2,), _swiglu_rows(xs[...], wg_ref, wu_ref, wd_ref, (0,)))
            copies(i, lambda cp: cp.start())

    _scatter_step(i, pl.num_programs(0), copies, start)


def _moe_experts(xb, blk_e, n_valid, row_dst, wg, wu, wd, n_out_rows):
    n_blocks = xb.shape[0] // MOE_ROWS
    weight = lambda shape: pl.BlockSpec((1,) + shape, lambda i, be, nv, rd: (be[i], 0, 0))
    return pl.pallas_call(
        _moe_kernel,
        grid_spec=pltpu.PrefetchScalarGridSpec(
            num_scalar_prefetch=3,
            grid=(n_blocks,),
            in_specs=[pl.BlockSpec((MOE_ROWS, ROW_TILES, 128), lambda i, be, nv, rd: (i, 0, 0)),
                      weight((D_MODEL, D_FF)), weight((D_MODEL, D_FF)), weight((D_FF, D_MODEL))],
            out_specs=pl.BlockSpec(memory_space=pl.ANY),
            scratch_shapes=[pltpu.VMEM((MOE_ROWS, D_MODEL), BF16),
                            pltpu.VMEM((2, MOE_ROWS, ROW_TILES, 128), F32),
                            pltpu.SemaphoreType.DMA((2,))],
        ),
        out_shape=jax.ShapeDtypeStruct((n_out_rows, ROW_TILES, 128), F32),
        compiler_params=_cparams("arbitrary"),
        name="moe_experts",
    )(blk_e, n_valid, row_dst, xb, wg, wu, wd)


def _combine_kernel(x_ref, y0_ref, y1_ref, gate_ref, g_ref, b_ref, o_ref):
    gate = gate_ref[...]
    f = gate[:, 0:1] * _from_row_tiles(y0_ref, F32) + gate[:, 1:2] * _from_row_tiles(y1_ref, F32)
    o_ref[...] = _layer_norm(DN_ALPHA * x_ref[...] + f, g_ref[...], b_ref[...])


def _combine(x2, y_slots, gate, ln_g, ln_b, tm=512):
    n = x2.shape[0]
    row = lambda i: (i, 0)
    vec = pl.BlockSpec((1, D_MODEL), lambda i: (0, 0))
    return pl.pallas_call(
        _combine_kernel,
        grid=(n // tm,),
        in_specs=[pl.BlockSpec((tm, D_MODEL), row),
                  pl.BlockSpec((tm, ROW_TILES, 128), lambda i: (i, 0, 0)),
                  pl.BlockSpec((tm, ROW_TILES, 128), lambda i: (n // tm + i, 0, 0)),
                  pl.BlockSpec((tm, EXPERT_TOPK), row), vec, vec],
        out_specs=pl.BlockSpec((tm, D_MODEL), row),
        out_shape=jax.ShapeDtypeStruct((n, D_MODEL), F32),
        compiler_params=_cparams("parallel"),
        name="combine_ln",
    )(x2, y_slots, y_slots, gate, ln_g, ln_b)


def _moe_layer(x2, router_w, wg, wu, wd, ln_g, ln_b):
    n = x2.shape[0]
    nk = n * EXPERT_TOPK
    top_e, gate = _router(x2, router_w)
    top_e = top_e[:, :EXPERT_TOPK]
    gate = gate[:, :EXPERT_TOPK]
    flat_e = top_e.reshape(-1)
    onehot = (flat_e[:, None] == jnp.arange(N_EXPERTS, dtype=jnp.int32)[None, :]).astype(jnp.int32)
    counts = jnp.sum(onehot, axis=0)
    rank = jnp.sum((jnp.cumsum(onehot, axis=0) - onehot) * onehot, axis=1)
    padded = (counts + MOE_ROWS - 1) // MOE_ROWS * MOE_ROWS
    pad_end = jnp.cumsum(padded)
    pad_start = pad_end - padded
    dest = (pad_start[flat_e] + rank).astype(jnp.int32)
    n_blocks = -(-nk // MOE_ROWS) + N_EXPERTS
    n_rows = n_blocks * MOE_ROWS
    blk_start = jnp.arange(n_blocks, dtype=jnp.int32) * MOE_ROWS
    blk_e = jnp.sum((blk_start[:, None] >= pad_end[None, :]).astype(jnp.int32), axis=1)
    valid_end = jnp.concatenate([pad_start + counts, jnp.zeros((1,), jnp.int32)])
    n_valid = jnp.clip(valid_end[blk_e] - blk_start, 0, MOE_ROWS).astype(jnp.int32)
    blk_e = jnp.minimum(blk_e, N_EXPERTS - 1).astype(jnp.int32)
    j = jnp.arange(nk, dtype=jnp.int32)
    row_dst = jnp.zeros((n_rows,), jnp.int32).at[dest].set((j % EXPERT_TOPK) * n + j // EXPERT_TOPK)

    xb = _dispatch(x2, dest, n_rows)
    y_slots = _moe_experts(xb, blk_e, n_valid, row_dst, wg, wu, wd, nk)
    return _combine(x2, y_slots, gate, ln_g, ln_b)


def _rope_tables(t):
    half = HEAD_DIM // 2
    inv = ROPE_THETA ** (-jnp.arange(half, dtype=F32) / half)
    ang = jnp.arange(t).astype(F32)[:, None] * inv[None, :]
    cos = jnp.cos(ang)
    sin = jnp.sin(ang)
    cos_t = jnp.tile(jnp.concatenate([cos, cos], -1), (1, N_HEADS))
    sin_t = jnp.tile(jnp.concatenate([-sin, sin], -1), (1, N_HEADS))
    return cos_t, sin_t


def _pad_rows(w, start):
    out = jnp.zeros((LORA_DIM, BRANCH_DIM), F32)
    return lax.dynamic_update_slice(out, w, (start, 0)).astype(BF16)


def _mixer(x2, b, t, cos_t, sin_t, w_in, conv_w, shift_mu, decay_w0, decay_w2, aaa_a0, aaa_w2, gate_w2,
           k_k, k_a, r_k, gn_g, gn_b, w_branch, w_out, ln_g, ln_b):
    w_in = w_in.astype(BF16)
    mix, lora, gates = _in_proj(x2, w_in[:, :MIX_COLS], w_in[:, MIX_COLS:MIX_COLS + LORA_DIM],
                                w_in[:, MIX_COLS + LORA_DIM:])
    mix3 = mix.reshape(b, t, MIX_COLS)
    lora3 = lora.reshape(b, t, LORA_DIM)

    o_conv = _short_conv(mix3, conv_w)

    q3, k3, v3, kmean = _att_prep(mix3, cos_t, sin_t)
    o_att = _moba(q3, k3, v3, kmean.reshape(b, t // MOBA_BLOCK, BRANCH_DIM)).reshape(b * t, BRANCH_DIM)

    row = lambda a: a.reshape(1, -1)
    p = {
        "mu_r": row(shift_mu[0:BRANCH_DIM]), "mu_k": row(shift_mu[BRANCH_DIM:2 * BRANCH_DIM]),
        "mu_v": row(shift_mu[2 * BRANCH_DIM:3 * BRANCH_DIM]), "mu_l": row(shift_mu[3 * BRANCH_DIM:]),
        "w0": row(decay_w0), "w2": _pad_rows(decay_w2, 0),
        "a0": row(aaa_a0), "a2": _pad_rows(aaa_w2, DECAY_LORA),
        "g2": _pad_rows(gate_w2, DECAY_LORA + AAA_LORA),
        "k_k": row(k_k), "k_a": row(k_a), "r_k": row(r_k), "gn_g": row(gn_g), "gn_b": row(gn_b),
    }
    o_rwkv = _rwkv(mix3, lora3, p)

    return _merge(o_conv.reshape(b * t, BRANCH_DIM), o_att, o_rwkv.reshape(b * t, BRANCH_DIM), gates, x2,
                  w_branch.astype(BF16), w_out.astype(BF16), row(ln_g), row(ln_b))


def kernel(x, w_in, conv_w, shift_mu, decay_w0, decay_w2, aaa_a0, aaa_w2, gate_w2, k_k, k_a, r_k, gn_g, gn_b,
           w_branch, w_out, ln1_g, ln1_b, ln2_g, ln2_b, ffn_w_gate, ffn_w_up, ffn_w_down, router_w,
           moe_w_gate, moe_w_up, moe_w_down):
    b, t, d = x.shape
    x2 = x.reshape(b * t, d)
    cos_t, sin_t = _rope_tables(t)
    row = lambda a: a.reshape(1, -1)
    for l in range(DEPTH):
        x2 = _mixer(x2, b, t, cos_t, sin_t, w_in[l], conv_w[l], shift_mu[l], decay_w0[l], decay_w2[l],
                    aaa_a0[l], aaa_w2[l], gate_w2[l], k_k[l], k_a[l], r_k[l], gn_g[l], gn_b[l],
                    w_branch[l], w_out[l], ln1_g[l], ln1_b[l])
        j = l // 2
        if l % 2 == 0:
            x2 = _ffn(x2, ffn_w_gate[j].astype(BF16), ffn_w_up[j].astype(BF16), ffn_w_down[j].astype(BF16),
                      row(ln2_g[l]), row(ln2_b[l]))
        else:
            x2 = _moe_layer(x2, router_w[j], moe_w_gate[j].astype(BF16), moe_w_up[j].astype(BF16),
                            moe_w_down[j].astype(BF16), row(ln2_g[l]), row(ln2_b[l]))
    return x2.reshape(b, t, d)
```

```python
import functools

import jax
import jax.numpy as jnp
from jax import lax
from jax.experimental import pallas as pl
from jax.experimental.pallas import tpu as pltpu

F32 = jnp.float32
BF16 = jnp.bfloat16

D_MODEL = 1024
HEAD_DIM = 64
BRANCH_DIM = 512
N_HEADS = BRANCH_DIM // HEAD_DIM
DECAY_LORA = 64
AAA_LORA = 64
GATE_LORA = 128
LORA_DIM = DECAY_LORA + AAA_LORA + GATE_LORA
MIX_COLS = 9 * BRANCH_DIM
GATE_COLS = 3 * D_MODEL
MOBA_BLOCK = 256
MOBA_TOPK = 3
ROPE_THETA = 10000.0
D_FF = 2816
N_EXPERTS = 8
EXPERT_TOPK = 2
MOE_ROWS = 256
LN_EPS = 1e-5
GN_EPS = 64e-5
DEPTH = 2
DN_ALPHA = (2 * DEPTH) ** 0.25
LOG2_E = 1.4426950408889634

VMEM_LIMIT_BYTES = 56 * 1024 * 1024
SUBLANES = 8
RWKV_TILE = 256
RWKV_CHUNK = 64
FF_CHUNK = 256
PAIR_WIDTH = 2 * HEAD_DIM
HEAD_SHIFT = HEAD_DIM.bit_length() - 1
CHUNK_SHIFT = RWKV_CHUNK.bit_length() - 1


def _cparams(*sem):
    return pltpu.CompilerParams(dimension_semantics=sem, vmem_limit_bytes=VMEM_LIMIT_BYTES)


def _vmem_whole():
    return pl.BlockSpec(memory_space=pltpu.VMEM)


def _dot(a, b):
    return jnp.dot(a.astype(BF16), b.astype(BF16), preferred_element_type=F32)


def _dot_nt(a, b):
    return lax.dot_general(a.astype(BF16), b.astype(BF16), (((1,), (1,)), ((), ())),
                           preferred_element_type=F32)


def _dot_tn(a, b):
    return lax.dot_general(a.astype(BF16), b.astype(BF16), (((0,), (0,)), ((), ())),
                           preferred_element_type=F32)


def _dot_f32(a, b):
    return jnp.dot(a, b, preferred_element_type=F32, precision=lax.Precision.HIGHEST)


def _split_bf16(x):
    hi = x.astype(BF16)
    return hi, (x - hi.astype(F32)).astype(BF16)


def _dot_split(a, b, *, data_on_left):
    hi, lo = _split_bf16(a if data_on_left else b)
    if data_on_left:
        return jnp.dot(hi, b, preferred_element_type=F32) + jnp.dot(lo, b, preferred_element_type=F32)
    return jnp.dot(a, hi, preferred_element_type=F32) + jnp.dot(a, lo, preferred_element_type=F32)


def _sigmoid(x):
    return 1.0 / (1.0 + jnp.exp(-x))


def _layer_norm(y, g, b):
    mu = jnp.mean(y, axis=-1, keepdims=True)
    d = y - mu
    var = jnp.mean(d * d, axis=-1, keepdims=True)
    return d * lax.rsqrt(var + LN_EPS) * g + b


def _shift_rows(u, halo, n):
    out = pltpu.roll(u, n, axis=0)
    row = lax.broadcasted_iota(jnp.int32, u.shape, 0)
    for r in range(n):
        src = halo[SUBLANES - n + r:SUBLANES - n + r + 1, :]
        out = jnp.where(row == r, src, out)
    return out


def _halo_index(tile_rows):
    step = tile_rows // SUBLANES

    def index(col):
        return lambda b, i: (b, jnp.maximum(i * step - 1, 0), col)
    return index


def _in_proj_kernel(x_ref, wm_ref, wl_ref, wg_ref, mix_ref, lora_ref, gate_ref):
    xb = x_ref[...].astype(BF16)
    for j in range(0, MIX_COLS, BRANCH_DIM):
        mix_ref[:, j:j + BRANCH_DIM] = jnp.dot(xb, wm_ref[:, j:j + BRANCH_DIM],
                                               preferred_element_type=F32)
    lora_ref[...] = jnp.dot(xb, wl_ref[...], preferred_element_type=F32)
    for j in range(0, GATE_COLS, BRANCH_DIM):
        g = jnp.dot(xb, wg_ref[:, j:j + BRANCH_DIM], preferred_element_type=F32)
        gate_ref[:, j:j + BRANCH_DIM] = _sigmoid(g).astype(gate_ref.dtype)


def _in_proj(x2, w_mix, w_lora, w_gate, tm=256):
    n = x2.shape[0]
    row = lambda i: (i, 0)
    return pl.pallas_call(
        _in_proj_kernel,
        grid=(n // tm,),
        in_specs=[pl.BlockSpec((tm, D_MODEL), row), _vmem_whole(), _vmem_whole(), _vmem_whole()],
        out_specs=[pl.BlockSpec((tm, MIX_COLS), row), pl.BlockSpec((tm, LORA_DIM), row),
                   pl.BlockSpec((tm, GATE_COLS), row)],
        out_shape=[jax.ShapeDtypeStruct((n, MIX_COLS), F32), jax.ShapeDtypeStruct((n, LORA_DIM), F32),
                   jax.ShapeDtypeStruct((n, GATE_COLS), BF16)],
        compiler_params=_cparams("parallel"),
        name="in_proj",
    )(x2, w_mix, w_lora, w_gate)


def _conv_kernel(ch_ref, cb_ref, cc_ref, hh_ref, hc_ref, w_ref, o_ref):
    first = pl.program_id(1) == 0
    u = cc_ref[0] * ch_ref[0]
    halo = jnp.where(first, 0.0, hc_ref[0] * hh_ref[0])
    w = w_ref[...]
    y = w[2:3, :] * u + w[1:2, :] * _shift_rows(u, halo, 1) + w[0:1, :] * _shift_rows(u, halo, 2)
    o_ref[0] = (cb_ref[0] * y).astype(o_ref.dtype)


def _short_conv(mix3, conv_w, tq=512):
    b, t, _ = mix3.shape
    blk = lambda col: pl.BlockSpec((1, tq, BRANCH_DIM), lambda bi, i: (bi, i, col))
    halo = _halo_index(tq)
    hblk = lambda col: pl.BlockSpec((1, SUBLANES, BRANCH_DIM), halo(col))
    return pl.pallas_call(
        _conv_kernel,
        grid=(b, t // tq),
        in_specs=[blk(0), blk(1), blk(2), hblk(0), hblk(2),
                  pl.BlockSpec((3, BRANCH_DIM), lambda bi, i: (0, 0))],
        out_specs=pl.BlockSpec((1, tq, BRANCH_DIM), lambda bi, i: (bi, i, 0)),
        out_shape=jax.ShapeDtypeStruct((b, t, BRANCH_DIM), BF16),
        compiler_params=_cparams("parallel", "parallel"),
        name="short_conv",
    )(mix3, mix3, mix3, mix3, mix3, conv_w)


def _att_prep_kernel(q_ref, k_ref, v_ref, cos_ref, sin_ref, qo_ref, ko_ref, vo_ref, km_ref):
    cos = cos_ref[...]
    sin = sin_ref[...]
    lane = lax.broadcasted_iota(jnp.int32, cos.shape, 1)
    first_half = (lane & (HEAD_DIM - 1)) < (HEAD_DIM // 2)

    def rope(x):
        swapped = jnp.where(first_half, pltpu.roll(x, BRANCH_DIM - HEAD_DIM // 2, axis=1),
                            pltpu.roll(x, HEAD_DIM // 2, axis=1))
        return x * cos + swapped * sin

    kr = rope(k_ref[0])
    qo_ref[0] = rope(q_ref[0])
    ko_ref[0] = kr.astype(BF16)
    vo_ref[0, 0] = v_ref[0].T.astype(BF16)
    km_ref[0, 0] = jnp.mean(kr, axis=0, keepdims=True)


def _att_prep(mix3, cos_t, sin_t):
    b, t, _ = mix3.shape
    tq = MOBA_BLOCK
    nb = t // tq
    blk = lambda col: pl.BlockSpec((1, tq, BRANCH_DIM), lambda bi, i: (bi, i, col))
    tab = pl.BlockSpec((tq, BRANCH_DIM), lambda bi, i: (i, 0))
    out = pl.BlockSpec((1, tq, BRANCH_DIM), lambda bi, i: (bi, i, 0))
    return pl.pallas_call(
        _att_prep_kernel,
        grid=(b, nb),
        in_specs=[blk(3), blk(4), blk(5), tab, tab],
        out_specs=[out, out, pl.BlockSpec((1, 1, BRANCH_DIM, tq), lambda bi, i: (bi, i, 0, 0)),
                   pl.BlockSpec((1, 1, 1, BRANCH_DIM), lambda bi, i: (bi, i, 0, 0))],
        out_shape=[jax.ShapeDtypeStruct((b, t, BRANCH_DIM), F32),
                   jax.ShapeDtypeStruct((b, t, BRANCH_DIM), BF16),
                   jax.ShapeDtypeStruct((b, nb, BRANCH_DIM, tq), BF16),
                   jax.ShapeDtypeStruct((b, nb, 1, BRANCH_DIM), F32)],
        compiler_params=_cparams("parallel", "parallel"),
        name="att_prep",
    )(mix3, mix3, mix3, cos_t, sin_t)


def _moba_kernel(q_ref, k_ref, vt_ref, km_ref, o_ref, *, nb, heads):
    i = pl.program_id(2)
    bs = MOBA_BLOCK
    width = heads * HEAD_DIM
    q = q_ref[0]
    km = km_ref[0]
    head_of_lane = lax.broadcasted_iota(jnp.int32, (1, width), 1) >> HEAD_SHIFT
    q_heads = jnp.concatenate([jnp.where(head_of_lane == g, q, 0.0) for g in range(heads)], axis=0)
    q_all = (q_heads * (HEAD_DIM ** -0.5 * LOG2_E)).astype(BF16)

    blk = lax.broadcasted_iota(jnp.int32, (nb, heads * bs), 0)
    valid = blk < i
    km_hi, km_lo = _split_bf16(km)
    q_hi, q_lo = _split_bf16(q_heads)
    hi_part = _dot_nt(jnp.concatenate([km_hi, km_lo], axis=0), q_hi)
    gate = hi_part[:nb] + hi_part[nb:] + _dot_nt(km_hi, q_lo)
    gate = jnp.where(valid, gate, -jnp.inf)
    rank = jnp.zeros((nb, heads * bs), jnp.int32)
    for m in range(nb):
        gm = gate[m:m + 1, :]
        beats = (gm > gate) | ((gm == gate) & (blk > m))
        rank = rank + beats.astype(jnp.int32)
    chosen = jnp.where(valid & (rank < MOBA_TOPK), (1 << blk).astype(F32), 0.0)
    picks = jnp.sum(chosen, axis=0, keepdims=True).astype(jnp.int32)

    def block_update(j, n_blocks, keep, carry):
        kb = k_ref[0, pl.ds(pl.multiple_of(j * bs, bs), n_blocks * bs), :]
        m, l, accs = carry
        s = lax.dot_general(kb, q_all, (((1,), (1,)), ((), ())), preferred_element_type=F32)
        s = jnp.where(keep, s, -jnp.inf)
        m_new = jnp.maximum(m, jnp.max(s, axis=0, keepdims=True))
        alpha = jnp.exp2(m - m_new)
        p = jnp.exp2(s - m_new)
        l_new = alpha * l + jnp.sum(p, axis=0, keepdims=True)
        p = p.astype(BF16)
        new_acc = []
        for g in range(heads):
            cols = slice(g * bs, (g + 1) * bs)
            acc = alpha[:, cols] * accs[g]
            for n in range(n_blocks):
                vt = vt_ref[0, j + n, g * HEAD_DIM:(g + 1) * HEAD_DIM, :]
                acc = acc + jnp.dot(vt, p[n * bs:(n + 1) * bs, cols], preferred_element_type=F32)
            new_acc.append(acc)
        return m_new, l_new, tuple(new_acc)

    key_id = lax.broadcasted_iota(jnp.int32, (bs, heads * bs), 0)
    query_id = lax.broadcasted_iota(jnp.int32, (bs, heads * bs), 1) & (bs - 1)
    init = (jnp.full((1, heads * bs), -jnp.inf, F32), jnp.zeros((1, heads * bs), F32),
            tuple(jnp.zeros((HEAD_DIM, bs), F32) for _ in range(heads)))
    carry = block_update(i, 1, key_id <= query_id, init)

    second_block = lax.broadcasted_iota(jnp.int32, (2 * bs, heads * bs), 0) >= bs

    def body(jj, carry):
        j = 2 * jj
        bits = picks >> j
        keep = (jnp.where(second_block, bits >> 1, bits) & 1) == 1
        return block_update(j, 2, keep, carry)

    _, l, accs = lax.fori_loop(0, (i + 1) // 2, body, carry)
    out_t = jnp.concatenate([accs[g] / l[:, g * bs:(g + 1) * bs] for g in range(heads)], axis=0)
    o_ref[0] = out_t.T.astype(o_ref.dtype)


def _moba(q3, k3, vt4, kmean, heads=4):
    b, t, _ = q3.shape
    nb = t // MOBA_BLOCK
    width = heads * HEAD_DIM
    qblk = pl.BlockSpec((1, MOBA_BLOCK, width), lambda bi, hi, i: (bi, i, hi))
    return pl.pallas_call(
        functools.partial(_moba_kernel, nb=nb, heads=heads),
        grid=(b, N_HEADS // heads, nb),
        in_specs=[qblk, pl.BlockSpec((1, t, width), lambda bi, hi, i: (bi, 0, hi)),
                  pl.BlockSpec((1, nb, width, MOBA_BLOCK), lambda bi, hi, i: (bi, 0, hi, 0)),
                  pl.BlockSpec((1, nb, width), lambda bi, hi, i: (bi, 0, hi))],
        out_specs=qblk,
        out_shape=jax.ShapeDtypeStruct((b, t, BRANCH_DIM), BF16),
        compiler_params=_cparams("parallel", "parallel", "arbitrary"),
        name="moba",
    )(q3, k3, vt4, kmean)


def _rwkv_kernel(r_ref, k_ref, v_ref, lo_ref, hr_ref, hk_ref, hv_ref, hl_ref,
                 mu_r_ref, mu_k_ref, mu_v_ref, mu_l_ref, w0_ref, w2_ref, a0_ref, a2_ref, g2_ref,
                 kk_ref, ka_ref, rk_ref, gng_ref, gnb_ref, o_ref, state_ref):
    i = pl.program_id(1)
    first = i == 0
    tq, ch = RWKV_TILE, RWKV_CHUNK
    hd, bd = HEAD_DIM, BRANCH_DIM

    @pl.when(first)
    def _():
        state_ref[...] = jnp.zeros_like(state_ref)

    def lerp(x_ref, h_ref, mu_ref):
        x = x_ref[0]
        halo = jnp.where(first, 0.0, h_ref[0])
        return x + (_shift_rows(x, halo, 1) - x) * mu_ref[...]

    r = lerp(r_ref, hr_ref, mu_r_ref)
    k = lerp(k_ref, hk_ref, mu_k_ref)
    v = lerp(v_ref, hv_ref, mu_v_ref)
    lo = lerp(lo_ref, hl_ref, mu_l_ref)

    z = w0_ref[...] + _dot(jnp.tanh(lo), w2_ref[...])
    ew = jnp.exp(-0.5) * _sigmoid(z)
    a = _sigmoid(a0_ref[...] + _dot(lo, a2_ref[...]))
    g = _dot(_sigmoid(lo), g2_ref[...])

    half = bd // 2
    lane_r = lax.broadcasted_iota(jnp.int32, (half, half), 0)
    lane_c = lax.broadcasted_iota(jnp.int32, (half, half), 1)
    ones_blocks = jnp.where((lane_r >> HEAD_SHIFT) == (lane_c >> HEAD_SHIFT), 1.0, 0.0).astype(BF16)

    def head_sum(x):
        return jnp.concatenate([_dot_split(x[:, :half], ones_blocks, data_on_left=True),
                                _dot_split(x[:, half:], ones_blocks, data_on_left=True)], axis=1)

    kk = k * kk_ref[...]
    kk = kk / jnp.maximum(jnp.sqrt(head_sum(kk * kk)), 1e-12)
    k2 = k * (1.0 + (a - 1.0) * ka_ref[...])
    bonus = head_sum(r * k2 * rk_ref[...]) * v

    t_r = lax.broadcasted_iota(jnp.int32, (tq, tq), 0)
    t_c = lax.broadcasted_iota(jnp.int32, (tq, tq), 1)
    same_chunk = (t_r >> CHUNK_SHIFT) == (t_c >> CHUNK_SHIFT)
    cs = _dot_split(jnp.where(same_chunk & (t_c <= t_r), 1.0, 0.0).astype(BF16), ew, data_on_left=False)
    cs_end = jnp.concatenate(
        [jnp.broadcast_to(cs[c * ch + ch - 1:c * ch + ch, :], (ch, bd)) for c in range(tq // ch)], axis=0)

    kka = kk * a
    a_t = -kk * jnp.exp(ew - cs)
    b_hat = (kka * jnp.exp(cs)).astype(BF16)
    k_hat = (k2 * jnp.exp(cs)).astype(BF16)
    r_t = r * jnp.exp(-cs)
    b_e = (kka * jnp.exp(cs - cs_end)).astype(BF16)
    k_e = (k2 * jnp.exp(cs - cs_end)).astype(BF16)
    p_c = jnp.exp(-cs_end)
    v_b = v.astype(BF16)

    strict = jnp.where(same_chunk & (t_c < t_r), 1.0, 0.0)
    incl = jnp.where(same_chunk & (t_c <= t_r), 1.0, 0.0)
    eye = jnp.where(t_c == t_r, 1.0, 0.0)
    pw = PAIR_WIDTH
    second = (lax.broadcasted_iota(jnp.int32, (1, pw), 1) >> HEAD_SHIFT) == 1
    same_head = ((lax.broadcasted_iota(jnp.int32, (pw, pw), 0) >> HEAD_SHIFT)
                 == (lax.broadcasted_iota(jnp.int32, (pw, pw), 1) >> HEAD_SHIFT))

    n_pairs = N_HEADS // 2
    pair = lambda x, p: x[:, p * pw:(p + 1) * pw]
    head_list = [(p, hm) for p in range(n_pairs) for hm in (~second, second)]
    a_hs = [jnp.where(hm, pair(a_t, p), 0.0).astype(BF16) for p, hm in head_list]
    l_abs = [strict * _dot_nt(a_h, pair(b_hat, p)) for a_h, (p, _) in zip(a_hs, head_list)]
    t_invs = [eye + x for x in l_abs]
    powers = l_abs
    n = 2
    while n < ch:
        powers = [_dot(x, x) for x in powers]
        t_invs = [t + _dot(t, x) for t, x in zip(t_invs, powers)]
        n *= 2
    l_aks = [strict * _dot_nt(a_h, pair(k_hat, p)) for a_h, (p, _) in zip(a_hs, head_list)]
    lakvs = [_dot(l_ak, pair(v_b, p)) for l_ak, (p, _) in zip(l_aks, head_list)]
    sols = [_dot(t, jnp.concatenate([pair(a_t, p).astype(BF16), lakv.astype(BF16)], axis=1))
            for t, lakv, (p, _) in zip(t_invs, lakvs, head_list)]
    r_hs = [jnp.where(hm, pair(r_t, p), 0.0).astype(BF16) for p, hm in head_list]
    m_rbs = [incl * _dot_nt(r_h, pair(b_hat, p)) for r_h, (p, _) in zip(r_hs, head_list)]
    m_rks = [incl * _dot_nt(r_h, pair(k_hat, p)) for r_h, (p, _) in zip(r_hs, head_list)]
    apps = [_dot(m_rb, sol) for m_rb, sol in zip(m_rbs, sols)]
    mrkvs = [_dot(m_rk, pair(v_b, p)) for m_rk, (p, _) in zip(m_rks, head_list)]

    def both(xs, p, cols):
        return jnp.where(second, xs[2 * p + 1][:, cols], xs[2 * p][:, cols])

    left, right = slice(0, pw), slice(pw, 2 * pw)
    a_til = [both(sols, p, left).astype(BF16) for p in range(n_pairs)]
    u_0 = [both(sols, p, right).astype(BF16) for p in range(n_pairs)]
    r_til = [(both(apps, p, left) + pair(r_t, p)).astype(BF16) for p in range(n_pairs)]
    y_hat = [both(apps, p, right) + both(mrkvs, p, left) for p in range(n_pairs)]

    states = [state_ref[p] for p in range(n_pairs)]
    y_rows = []
    for c in range(tq // ch):
        rows = slice(c * ch, (c + 1) * ch)
        g_cs = [jnp.where(same_head, _dot_tn(a_til[p][rows], pair(b_e, p)[rows]), 0.0) for p in range(n_pairs)]
        h_cs = [jnp.where(same_head, _dot_tn(u_0[p][rows], pair(b_e, p)[rows])
                          + _dot_tn(pair(v_b, p)[rows], pair(k_e, p)[rows]), 0.0) for p in range(n_pairs)]
        y_rows.append(jnp.concatenate(
            [_dot_nt(r_til[p][rows], states[p]) + y_hat[p][rows] for p in range(n_pairs)], axis=1))
        states = [states[p] * pair(p_c, p)[c * ch:c * ch + 1] + _dot(states[p], g_cs[p]) + h_cs[p]
                  for p in range(n_pairs)]
    for p in range(n_pairs):
        state_ref[p] = states[p]
    y = jnp.concatenate(y_rows, axis=0)

    mu = head_sum(y) * (1.0 / hd)
    d = y - mu
    var = head_sum(d * d) * (1.0 / hd)
    y = d * lax.rsqrt(var + GN_EPS) * gng_ref[...] + gnb_ref[...]
    o_ref[0] = ((y + bonus) * g).astype(o_ref.dtype)


def _rwkv(mix3, lora3, p):
    b, t, _ = mix3.shape
    tq = RWKV_TILE
    halo = _halo_index(tq)
    blk = lambda col: pl.BlockSpec((1, tq, BRANCH_DIM), lambda bi, i: (bi, i, col))
    hblk = lambda col: pl.BlockSpec((1, SUBLANES, BRANCH_DIM), halo(col))
    vec = lambda width: pl.BlockSpec((1, width), lambda bi, i: (0, 0))
    mat = lambda rows: pl.BlockSpec((rows, BRANCH_DIM), lambda bi, i: (0, 0))
    return pl.pallas_call(
        _rwkv_kernel,
        grid=(b, t // tq),
        in_specs=[blk(6), blk(7), blk(8), pl.BlockSpec((1, tq, LORA_DIM), lambda bi, i: (bi, i, 0)),
                  hblk(6), hblk(7), hblk(8), pl.BlockSpec((1, SUBLANES, LORA_DIM), halo(0)),
                  vec(BRANCH_DIM), vec(BRANCH_DIM), vec(BRANCH_DIM), vec(LORA_DIM),
                  vec(BRANCH_DIM), mat(LORA_DIM), vec(BRANCH_DIM), mat(LORA_DIM), mat(LORA_DIM),
                  vec(BRANCH_DIM), vec(BRANCH_DIM), vec(BRANCH_DIM), vec(BRANCH_DIM), vec(BRANCH_DIM)],
        out_specs=pl.BlockSpec((1, tq, BRANCH_DIM), lambda bi, i: (bi, i, 0)),
        out_shape=jax.ShapeDtypeStruct((b, t, BRANCH_DIM), BF16),
        scratch_shapes=[pltpu.VMEM((N_HEADS // 2, PAIR_WIDTH, PAIR_WIDTH), F32)],
        compiler_params=_cparams("parallel", "arbitrary"),
        name="rwkv7",
    )(mix3, mix3, mix3, lora3, mix3, mix3, mix3, lora3,
      p["mu_r"], p["mu_k"], p["mu_v"], p["mu_l"], p["w0"], p["w2"], p["a0"], p["a2"], p["g2"],
      p["k_k"], p["k_a"], p["r_k"], p["gn_g"], p["gn_b"])


def _merge_kernel(oc_ref, oa_ref, or_ref, gate_ref, x_ref, wb_ref, wo_ref, g_ref, b_ref, o_ref):
    d = D_MODEL
    bd = BRANCH_DIM
    merged = (gate_ref[:, 0:d] * jnp.dot(oc_ref[...], wb_ref[0:bd, :], preferred_element_type=F32)
              + gate_ref[:, d:2 * d] * jnp.dot(oa_ref[...], wb_ref[bd:2 * bd, :], preferred_element_type=F32)
              + gate_ref[:, 2 * d:3 * d] * jnp.dot(or_ref[...], wb_ref[2 * bd:3 * bd, :],
                                                   preferred_element_type=F32))
    h = jnp.dot(merged.astype(BF16), wo_ref[...], preferred_element_type=F32)
    o_ref[...] = _layer_norm(DN_ALPHA * x_ref[...] + h, g_ref[...], b_ref[...])


def _merge(o_conv, o_att, o_rwkv, gates, x2, w_branch, w_out, ln_g, ln_b, tm=256):
    n = x2.shape[0]
    row = lambda i: (i, 0)
    vec = pl.BlockSpec((1, D_MODEL), lambda i: (0, 0))
    return pl.pallas_call(
        _merge_kernel,
        grid=(n // tm,),
        in_specs=[pl.BlockSpec((tm, BRANCH_DIM), row)] * 3
                 + [pl.BlockSpec((tm, GATE_COLS), row), pl.BlockSpec((tm, D_MODEL), row),
                    _vmem_whole(), _vmem_whole(), vec, vec],
        out_specs=pl.BlockSpec((tm, D_MODEL), row),
        out_shape=jax.ShapeDtypeStruct((n, D_MODEL), F32),
        compiler_params=_cparams("parallel"),
        name="merge_ln",
    )(o_conv, o_att, o_rwkv, gates, x2, w_branch, w_out, ln_g, ln_b)


def _swiglu_rows(xb, wg_ref, wu_ref, wd_ref, lead):
    acc = jnp.zeros((xb.shape[0], D_MODEL), F32)
    for c in range(0, D_FF, FF_CHUNK):
        cols = slice(c, c + FF_CHUNK)
        hg = jnp.dot(xb, wg_ref[lead + (slice(None), cols)], preferred_element_type=F32)
        hu = jnp.dot(xb, wu_ref[lead + (slice(None), cols)], preferred_element_type=F32)
        hidden = hg * _sigmoid(hg) * hu
        acc = acc + jnp.dot(hidden.astype(BF16), wd_ref[lead + (cols, slice(None))],
                            preferred_element_type=F32)
    return acc


def _ffn_kernel(x_ref, wg_ref, wu_ref, wd_ref, g_ref, b_ref, o_ref):
    x = x_ref[...]
    f = _swiglu_rows(x.astype(BF16), wg_ref, wu_ref, wd_ref, ())
    o_ref[...] = _layer_norm(DN_ALPHA * x + f, g_ref[...], b_ref[...])


def _ffn(x2, wg, wu, wd, ln_g, ln_b, tm=256):
    n = x2.shape[0]
    row = lambda i: (i, 0)
    vec = pl.BlockSpec((1, D_MODEL), lambda i: (0, 0))
    return pl.pallas_call(
        _ffn_kernel,
        grid=(n // tm,),
        in_specs=[pl.BlockSpec((tm, D_MODEL), row), _vmem_whole(), _vmem_whole(), _vmem_whole(), vec, vec],
        out_specs=pl.BlockSpec((tm, D_MODEL), row),
        out_shape=jax.ShapeDtypeStruct((n, D_MODEL), F32),
        compiler_params=_cparams("parallel"),
        name="ffn_ln",
    )(x2, wg, wu, wd, ln_g, ln_b)


def _router_kernel(x_ref, w_ref, e_ref, g_ref):
    logits = _dot_f32(x_ref[...], w_ref[...])
    lane = lax.broadcasted_iota(jnp.int32, logits.shape, 1).astype(F32)
    m1 = jnp.max(logits, axis=1, keepdims=True)
    e1 = jnp.min(jnp.where(logits == m1, lane, float(N_EXPERTS)), axis=1, keepdims=True)
    rest = jnp.where(lane == e1, -jnp.inf, logits)
    m2 = jnp.max(rest, axis=1, keepdims=True)
    e2 = jnp.min(jnp.where(rest == m2, lane, float(N_EXPERTS)), axis=1, keepdims=True)
    w2 = jnp.exp(m2 - m1)
    denom = 1.0 + w2
    e_ref[...] = jnp.where(lane == 0.0, e1, e2).astype(jnp.int32)
    g_ref[...] = jnp.where(lane == 0.0, 1.0 / denom, w2 / denom)


def _router(x2, router_w, tm=512):
    n = x2.shape[0]
    row = lambda i: (i, 0)
    return pl.pallas_call(
        _router_kernel,
        grid=(n // tm,),
        in_specs=[pl.BlockSpec((tm, D_MODEL), row), pl.BlockSpec((D_MODEL, N_EXPERTS), lambda i: (0, 0))],
        out_specs=[pl.BlockSpec((tm, N_EXPERTS), row), pl.BlockSpec((tm, N_EXPERTS), row)],
        out_shape=[jax.ShapeDtypeStruct((n, N_EXPERTS), jnp.int32),
                   jax.ShapeDtypeStruct((n, N_EXPERTS), F32)],
        compiler_params=_cparams("parallel"),
        name="router",
    )(x2, router_w)


ROW_TILES = D_MODEL // 128


def _to_row_tiles(dst_ref, lead, x):
    for c in range(ROW_TILES):
        dst_ref[lead + (slice(None), c, slice(None))] = x[:, c * 128:(c + 1) * 128]


def _from_row_tiles(src_ref, dtype):
    return jnp.concatenate([src_ref[:, c, :].astype(dtype) for c in range(ROW_TILES)], axis=1)


def _scatter_step(step, n_steps, wait_step, start_this_step):
    @pl.when(step >= 2)
    def _():
        wait_step(step - 2)

    start_this_step()

    @pl.when(step == n_steps - 1)
    def _():
        @pl.when(step >= 1)
        def _():
            wait_step(step - 1)

        wait_step(step)


def _wait_rows(buf_rows, hbm_ref, n_rows, sem):
    pltpu.make_async_copy(buf_rows.at[pl.ds(0, n_rows)], hbm_ref.at[pl.ds(0, n_rows)], sem).wait()


def _dispatch_kernel(dest_ref, x_ref, init_ref, xb_ref, buf, sems, *, tm):
    del init_ref
    i = pl.program_id(0)

    def wait_step(step):
        for _ in range(EXPERT_TOPK):
            _wait_rows(buf.at[step % 2], xb_ref, tm, sems.at[step % 2])

    def start():
        slot = i % 2
        _to_row_tiles(buf, (slot,), x_ref[...])

        def one(r, carry):
            for k in range(EXPERT_TOPK):
                dst = dest_ref[(i * tm + r) * EXPERT_TOPK + k]
                pltpu.make_async_copy(buf.at[slot, r], xb_ref.at[dst], sems.at[slot]).start()
            return carry
        lax.fori_loop(0, tm, one, 0, unroll=4)

    _scatter_step(i, pl.num_programs(0), wait_step, start)


def _dispatch(x2, dest, n_rows, tm=256):
    n = x2.shape[0]
    init = jnp.zeros((n_rows, ROW_TILES, 128), F32)
    return pl.pallas_call(
        functools.partial(_dispatch_kernel, tm=tm),
        grid_spec=pltpu.PrefetchScalarGridSpec(
            num_scalar_prefetch=1,
            grid=(n // tm,),
            in_specs=[pl.BlockSpec((tm, D_MODEL), lambda i, d: (i, 0)), pl.BlockSpec(memory_space=pl.ANY)],
            out_specs=pl.BlockSpec(memory_space=pl.ANY),
            scratch_shapes=[pltpu.VMEM((2, tm, ROW_TILES, 128), F32), pltpu.SemaphoreType.DMA((2,))],
        ),
        out_shape=jax.ShapeDtypeStruct((n_rows, ROW_TILES, 128), F32),
        input_output_aliases={2: 0},
        compiler_params=_cparams("arbitrary"),
        name="moe_dispatch",
    )(dest, x2, init)


def _moe_kernel(blk_e_ref, n_valid_ref, row_dst_ref, x_ref, wg_ref, wu_ref, wd_ref, y_ref, xs, ybuf, sems):
    del blk_e_ref
    i = pl.program_id(0)

    def wait_step(step):
        @pl.when(n_valid_ref[step] > 0)
        def _():
            _wait_rows(ybuf.at[step % 2], y_ref, n_valid_ref[step], sems.at[step % 2])

    def start():
        @pl.when(n_valid_ref[i] > 0)
        def _():
            slot = i % 2
            xs[...] = _from_row_tiles(x_ref, BF16)
            _to_row_tiles(ybuf, (slot,), _swiglu_rows(xs[...], wg_ref, wu_ref, wd_ref, (0,)))

            def one(r, carry):
                dst = row_dst_ref[i * MOE_ROWS + r]
                pltpu.make_async_copy(ybuf.at[slot, r], y_ref.at[dst], sems.at[slot]).start()
                return carry
            lax.fori_loop(0, n_valid_ref[i], one, 0)

    _scatter_step(i, pl.num_programs(0), wait_step, start)


def _moe_experts(xb, blk_e, n_valid, row_dst, wg, wu, wd, n_out_rows):
    n_blocks = xb.shape[0] // MOE_ROWS
    weight = lambda shape: pl.BlockSpec((1,) + shape, lambda i, be, nv, rd: (be[i], 0, 0))
    return pl.pallas_call(
        _moe_kernel,
        grid_spec=pltpu.PrefetchScalarGridSpec(
            num_scalar_prefetch=3,
            grid=(n_blocks,),
            in_specs=[pl.BlockSpec((MOE_ROWS, ROW_TILES, 128), lambda i, be, nv, rd: (i, 0, 0)),
                      weight((D_MODEL, D_FF)), weight((D_MODEL, D_FF)), weight((D_FF, D_MODEL))],
            out_specs=pl.BlockSpec(memory_space=pl.ANY),
            scratch_shapes=[pltpu.VMEM((MOE_ROWS, D_MODEL), BF16),
                            pltpu.VMEM((2, MOE_ROWS, ROW_TILES, 128), F32),
                            pltpu.SemaphoreType.DMA((2,))],
        ),
        out_shape=jax.ShapeDtypeStruct((n_out_rows, ROW_TILES, 128), F32),
        compiler_params=_cparams("arbitrary"),
        name="moe_experts",
    )(blk_e, n_valid, row_dst, xb, wg, wu, wd)


def _combine_kernel(x_ref, y0_ref, y1_ref, gate_ref, g_ref, b_ref, o_ref):
    gate = gate_ref[...]
    f = gate[:, 0:1] * _from_row_tiles(y0_ref, F32) + gate[:, 1:2] * _from_row_tiles(y1_ref, F32)
    o_ref[...] = _layer_norm(DN_ALPHA * x_ref[...] + f, g_ref[...], b_ref[...])


def _combine(x2, y_slots, gate, ln_g, ln_b, tm=512):
    n = x2.shape[0]
    row = lambda i: (i, 0)
    vec = pl.BlockSpec((1, D_MODEL), lambda i: (0, 0))
    return pl.pallas_call(
        _combine_kernel,
        grid=(n // tm,),
        in_specs=[pl.BlockSpec((tm, D_MODEL), row),
                  pl.BlockSpec((tm, ROW_TILES, 128), lambda i: (i, 0, 0)),
                  pl.BlockSpec((tm, ROW_TILES, 128), lambda i: (n // tm + i, 0, 0)),
                  pl.BlockSpec((tm, EXPERT_TOPK), row), vec, vec],
        out_specs=pl.BlockSpec((tm, D_MODEL), row),
        out_shape=jax.ShapeDtypeStruct((n, D_MODEL), F32),
        compiler_params=_cparams("parallel"),
        name="combine_ln",
    )(x2, y_slots, y_slots, gate, ln_g, ln_b)


def _moe_layer(x2, router_w, wg, wu, wd, ln_g, ln_b):
    n = x2.shape[0]
    nk = n * EXPERT_TOPK
    top_e, gate = _router(x2, router_w)
    top_e = top_e[:, :EXPERT_TOPK]
    gate = gate[:, :EXPERT_TOPK]
    flat_e = top_e.reshape(-1)
    onehot = (flat_e[:, None] == jnp.arange(N_EXPERTS, dtype=jnp.int32)[None, :]).astype(jnp.int32)
    counts = jnp.sum(onehot, axis=0)
    rank = jnp.sum((jnp.cumsum(onehot, axis=0) - onehot) * onehot, axis=1)
    padded = (counts + MOE_ROWS - 1) // MOE_ROWS * MOE_ROWS
    pad_end = jnp.cumsum(padded)
    pad_start = pad_end - padded
    dest = (pad_start[flat_e] + rank).astype(jnp.int32)
    n_blocks = -(-nk // MOE_ROWS) + N_EXPERTS
    n_rows = n_blocks * MOE_ROWS
    blk_start = jnp.arange(n_blocks, dtype=jnp.int32) * MOE_ROWS
    blk_e = jnp.sum((blk_start[:, None] >= pad_end[None, :]).astype(jnp.int32), axis=1)
    valid_end = jnp.concatenate([pad_start + counts, jnp.zeros((1,), jnp.int32)])
    n_valid = jnp.clip(valid_end[blk_e] - blk_start, 0, MOE_ROWS).astype(jnp.int32)
    blk_e = jnp.minimum(blk_e, N_EXPERTS - 1).astype(jnp.int32)
    j = jnp.arange(nk, dtype=jnp.int32)
    row_dst = jnp.zeros((n_rows,), jnp.int32).at[dest].set((j % EXPERT_TOPK) * n + j // EXPERT_TOPK)

    xb = _dispatch(x2, dest, n_rows)
    y_slots = _moe_experts(xb, blk_e, n_valid, row_dst, wg, wu, wd, nk)
    return _combine(x2, y_slots, gate, ln_g, ln_b)


def _rope_tables(t):
    half = HEAD_DIM // 2
    inv = ROPE_THETA ** (-jnp.arange(half, dtype=F32) / half)
    ang = jnp.arange(t).astype(F32)[:, None] * inv[None, :]
    cos = jnp.cos(ang)
    sin = jnp.sin(ang)
    cos_t = jnp.tile(jnp.concatenate([cos, cos], -1), (1, N_HEADS))
    sin_t = jnp.tile(jnp.concatenate([-sin, sin], -1), (1, N_HEADS))
    return cos_t, sin_t


def _pad_rows(w, start):
    out = jnp.zeros((LORA_DIM, BRANCH_DIM), F32)
    return lax.dynamic_update_slice(out, w, (start, 0)).astype(BF16)


def _mixer(x2, b, t, cos_t, sin_t, w_in, conv_w, shift_mu, decay_w0, decay_w2, aaa_a0, aaa_w2, gate_w2,
           k_k, k_a, r_k, gn_g, gn_b, w_branch, w_out, ln_g, ln_b):
    w_in = w_in.astype(BF16)
    mix, lora, gates = _in_proj(x2, w_in[:, :MIX_COLS], w_in[:, MIX_COLS:MIX_COLS + LORA_DIM],
                                w_in[:, MIX_COLS + LORA_DIM:])
    mix3 = mix.reshape(b, t, MIX_COLS)
    lora3 = lora.reshape(b, t, LORA_DIM)

    o_conv = _short_conv(mix3, conv_w)

    q3, k3, v3, kmean = _att_prep(mix3, cos_t, sin_t)
    o_att = _moba(q3, k3, v3, kmean.reshape(b, t // MOBA_BLOCK, BRANCH_DIM)).reshape(b * t, BRANCH_DIM)

    row = lambda a: a.reshape(1, -1)
    p = {
        "mu_r": row(shift_mu[0:BRANCH_DIM]), "mu_k": row(shift_mu[BRANCH_DIM:2 * BRANCH_DIM]),
        "mu_v": row(shift_mu[2 * BRANCH_DIM:3 * BRANCH_DIM]), "mu_l": row(shift_mu[3 * BRANCH_DIM:]),
        "w0": row(decay_w0), "w2": _pad_rows(decay_w2, 0),
        "a0": row(aaa_a0), "a2": _pad_rows(aaa_w2, DECAY_LORA),
        "g2": _pad_rows(gate_w2, DECAY_LORA + AAA_LORA),
        "k_k": row(k_k), "k_a": row(k_a), "r_k": row(r_k), "gn_g": row(gn_g), "gn_b": row(gn_b),
    }
    o_rwkv = _rwkv(mix3, lora3, p)

    return _merge(o_conv.reshape(b * t, BRANCH_DIM), o_att, o_rwkv.reshape(b * t, BRANCH_DIM), gates, x2,
                  w_branch.astype(BF16), w_out.astype(BF16), row(ln_g), row(ln_b))


def kernel(x, w_in, conv_w, shift_mu, decay_w0, decay_w2, aaa_a0, aaa_w2, gate_w2, k_k, k_a, r_k, gn_g, gn_b,
           w_branch, w_out, ln1_g, ln1_b, ln2_g, ln2_b, ffn_w_gate, ffn_w_up, ffn_w_down, router_w,
           moe_w_gate, moe_w_up, moe_w_down):
    b, t, d = x.shape
    x2 = x.reshape(b * t, d)
    cos_t, sin_t = _rope_tables(t)
    row = lambda a: a.reshape(1, -1)
    for l in range(DEPTH):
        x2 = _mixer(x2, b, t, cos_t, sin_t, w_in[l], conv_w[l], shift_mu[l], decay_w0[l], decay_w2[l],
                    aaa_a0[l], aaa_w2[l], gate_w2[l], k_k[l], k_a[l], r_k[l], gn_g[l], gn_b[l],
                    w_branch[l], w_out[l], ln1_g[l], ln1_b[l])
        j = l // 2
        if l % 2 == 0:
            x2 = _ffn(x2, ffn_w_gate[j].astype(BF16), ffn_w_up[j].astype(BF16), ffn_w_down[j].astype(BF16),
                      row(ln2_g[l]), row(ln2_b[l]))
        else:
            x2 = _moe_layer(x2, router_w[j], moe_w_gate[j].astype(BF16), moe_w_up[j].astype(BF16),
                            moe_w_down[j].astype(BF16), row(ln2_g[l]), row(ln2_b[l]))
    return x2.reshape(b, t, d)
```

```python
import functools

import jax
import jax.numpy as jnp
from jax import lax
from jax.experimental import pallas as pl
from jax.experimental.pallas import tpu as pltpu

F32 = jnp.float32
BF16 = jnp.bfloat16

D_MODEL = 1024
HEAD_DIM = 64
BRANCH_DIM = 512
N_HEADS = BRANCH_DIM // HEAD_DIM
DECAY_LORA = 64
AAA_LORA = 64
GATE_LORA = 128
LORA_DIM = DECAY_LORA + AAA_LORA + GATE_LORA
MIX_COLS = 9 * BRANCH_DIM
GATE_COLS = 3 * D_MODEL
MOBA_BLOCK = 256
MOBA_TOPK = 3
ROPE_THETA = 10000.0
D_FF = 2816
N_EXPERTS = 8
EXPERT_TOPK = 2
MOE_ROWS = 256
LN_EPS = 1e-5
GN_EPS = 64e-5
DEPTH = 2
DN_ALPHA = (2 * DEPTH) ** 0.25
LOG2_E = 1.4426950408889634

VMEM_LIMIT_BYTES = 56 * 1024 * 1024
SUBLANES = 8
RWKV_TILE = 256
RWKV_CHUNK = 64
FF_CHUNK = 256
PAIR_WIDTH = 2 * HEAD_DIM
HEAD_SHIFT = HEAD_DIM.bit_length() - 1
CHUNK_SHIFT = RWKV_CHUNK.bit_length() - 1


def _cparams(*sem):
    return pltpu.CompilerParams(dimension_semantics=sem, vmem_limit_bytes=VMEM_LIMIT_BYTES)


def _vmem_whole():
    return pl.BlockSpec(memory_space=pltpu.VMEM)


def _dot(a, b):
    return jnp.dot(a.astype(BF16), b.astype(BF16), preferred_element_type=F32)


def _dot_nt(a, b):
    return lax.dot_general(a.astype(BF16), b.astype(BF16), (((1,), (1,)), ((), ())),
                           preferred_element_type=F32)


def _dot_tn(a, b):
    return lax.dot_general(a.astype(BF16), b.astype(BF16), (((0,), (0,)), ((), ())),
                           preferred_element_type=F32)


def _split_bf16(x):
    hi = x.astype(BF16)
    return hi, (x - hi.astype(F32)).astype(BF16)


def _dot_split(a, b, *, data_on_left):
    hi, lo = _split_bf16(a if data_on_left else b)
    if data_on_left:
        return jnp.dot(hi, b, preferred_element_type=F32) + jnp.dot(lo, b, preferred_element_type=F32)
    return jnp.dot(a, hi, preferred_element_type=F32) + jnp.dot(a, lo, preferred_element_type=F32)


def _sigmoid(x):
    return 1.0 / (1.0 + jnp.exp(-x))


def _layer_norm(y, g, b):
    mu = jnp.mean(y, axis=-1, keepdims=True)
    d = y - mu
    var = jnp.mean(d * d, axis=-1, keepdims=True)
    return d * lax.rsqrt(var + LN_EPS) * g + b


def _shift_rows(u, halo, n):
    out = pltpu.roll(u, n, axis=0)
    row = lax.broadcasted_iota(jnp.int32, u.shape, 0)
    for r in range(n):
        src = halo[SUBLANES - n + r:SUBLANES - n + r + 1, :]
        out = jnp.where(row == r, src, out)
    return out


def _halo_index(tile_rows):
    step = tile_rows // SUBLANES

    def index(col):
        return lambda b, i: (b, jnp.maximum(i * step - 1, 0), col)
    return index


def _in_proj_kernel(x_ref, wm_ref, wl_ref, wg_ref, convw_ref, cos_ref, sin_ref,
                    conv_ref, q_ref, k_ref, vt_ref, km_ref, rkv_ref, lora_ref, gate_ref, halo_ref):
    first = pl.program_id(1) == 0
    bd = BRANCH_DIM
    xb = x_ref[0].astype(BF16)
    proj = lambda g: jnp.dot(xb, wm_ref[:, g * bd:(g + 1) * bd], preferred_element_type=F32)

    u = proj(2) * proj(0)
    halo = jnp.where(first, 0.0, halo_ref[...])
    w = convw_ref[...]
    y = w[2:3, :] * u + w[1:2, :] * _shift_rows(u, halo, 1) + w[0:1, :] * _shift_rows(u, halo, 2)
    conv_ref[0] = (proj(1) * y).astype(conv_ref.dtype)
    halo_ref[...] = u[u.shape[0] - SUBLANES:, :]

    cos = cos_ref[...]
    sin = sin_ref[...]
    lane = lax.broadcasted_iota(jnp.int32, cos.shape, 1)
    first_half = (lane & (HEAD_DIM - 1)) < (HEAD_DIM // 2)

    def rope(t):
        swapped = jnp.where(first_half, pltpu.roll(t, bd - HEAD_DIM // 2, axis=1),
                            pltpu.roll(t, HEAD_DIM // 2, axis=1))
        return t * cos + swapped * sin

    kr = rope(proj(4))
    q_ref[0] = rope(proj(3))
    k_ref[0] = kr.astype(BF16)
    km_ref[0, 0] = jnp.mean(kr, axis=0, keepdims=True)
    vt_ref[0, 0] = proj(5).astype(BF16).T

    for g in range(3):
        rkv_ref[0, :, g * bd:(g + 1) * bd] = proj(6 + g)
    lora_ref[0] = jnp.dot(xb, wl_ref[...], preferred_element_type=F32)
    for j in range(0, GATE_COLS, bd):
        g = jnp.dot(xb, wg_ref[:, j:j + bd], preferred_element_type=F32)
        gate_ref[0, :, j:j + bd] = _sigmoid(g).astype(gate_ref.dtype)


def _in_proj(x3, w_mix, w_lora, w_gate, conv_w, cos_t, sin_t):
    b, t, _ = x3.shape
    tm = MOBA_BLOCK
    nb = t // tm
    blk = lambda width: pl.BlockSpec((1, tm, width), lambda bi, i: (bi, i, 0))
    tab = pl.BlockSpec((tm, BRANCH_DIM), lambda bi, i: (i, 0))
    shape3 = lambda width, dtype: jax.ShapeDtypeStruct((b, t, width), dtype)
    return pl.pallas_call(
        _in_proj_kernel,
        grid=(b, nb),
        in_specs=[blk(D_MODEL), _vmem_whole(), _vmem_whole(), _vmem_whole(),
                  pl.BlockSpec((3, BRANCH_DIM), lambda bi, i: (0, 0)), tab, tab],
        out_specs=[blk(BRANCH_DIM), blk(BRANCH_DIM), blk(BRANCH_DIM),
                   pl.BlockSpec((1, 1, BRANCH_DIM, tm), lambda bi, i: (bi, i, 0, 0)),
                   pl.BlockSpec((1, 1, 1, BRANCH_DIM), lambda bi, i: (bi, i, 0, 0)),
                   blk(3 * BRANCH_DIM), blk(LORA_DIM), blk(GATE_COLS)],
        out_shape=[shape3(BRANCH_DIM, BF16), shape3(BRANCH_DIM, F32), shape3(BRANCH_DIM, BF16),
                   jax.ShapeDtypeStruct((b, nb, BRANCH_DIM, tm), BF16),
                   jax.ShapeDtypeStruct((b, nb, 1, BRANCH_DIM), F32),
                   shape3(3 * BRANCH_DIM, F32), shape3(LORA_DIM, F32), shape3(GATE_COLS, BF16)],
        scratch_shapes=[pltpu.VMEM((SUBLANES, BRANCH_DIM), F32)],
        compiler_params=_cparams("parallel", "arbitrary"),
        name="in_proj",
    )(x3, w_mix, w_lora, w_gate, conv_w, cos_t, sin_t)


def _moba_kernel(q_ref, k_ref, vt_ref, km_ref, o_ref, *, nb, heads):
    i = pl.program_id(2)
    bs = MOBA_BLOCK
    width = heads * HEAD_DIM
    q = q_ref[0]
    km = km_ref[0]
    head_of_lane = lax.broadcasted_iota(jnp.int32, (1, width), 1) >> HEAD_SHIFT
    q_heads = jnp.concatenate([jnp.where(head_of_lane == g, q, 0.0) for g in range(heads)], axis=0)
    q_all = (q_heads * (HEAD_DIM ** -0.5 * LOG2_E)).astype(BF16)

    blk = lax.broadcasted_iota(jnp.int32, (nb, heads * bs), 0)
    valid = blk < i
    km_hi, km_lo = _split_bf16(km)
    q_hi, q_lo = _split_bf16(q_heads)
    hi_part = _dot_nt(jnp.concatenate([km_hi, km_lo], axis=0), q_hi)
    gate = hi_part[:nb] + hi_part[nb:] + _dot_nt(km_hi, q_lo)
    gate = jnp.where(valid, gate, -jnp.inf)
    rank = jnp.zeros((nb, heads * bs), jnp.int32)
    for m in range(nb):
        gm = gate[m:m + 1, :]
        beats = (gm > gate) | ((gm == gate) & (blk > m))
        rank = rank + beats.astype(jnp.int32)
    chosen = jnp.where(valid & (rank < MOBA_TOPK), (1 << blk).astype(F32), 0.0)
    picks = jnp.sum(chosen, axis=0, keepdims=True).astype(jnp.int32)

    def block_update(j, n_blocks, keep, carry):
        kb = k_ref[0, pl.ds(pl.multiple_of(j * bs, bs), n_blocks * bs), :]
        m, l, accs = carry
        s = lax.dot_general(kb, q_all, (((1,), (1,)), ((), ())), preferred_element_type=F32)
        s = jnp.where(keep, s, -jnp.inf)
        m_new = jnp.maximum(m, jnp.max(s, axis=0, keepdims=True))
        alpha = jnp.exp2(m - m_new)
        p = jnp.exp2(s - m_new)
        l_new = alpha * l + jnp.sum(p, axis=0, keepdims=True)
        p = p.astype(BF16)
        new_acc = []
        for g in range(heads):
            cols = slice(g * bs, (g + 1) * bs)
            acc = alpha[:, cols] * accs[g]
            for n in range(n_blocks):
                vt = vt_ref[0, j + n, g * HEAD_DIM:(g + 1) * HEAD_DIM, :]
                acc = acc + jnp.dot(vt, p[n * bs:(n + 1) * bs, cols], preferred_element_type=F32)
            new_acc.append(acc)
        return m_new, l_new, tuple(new_acc)

    key_id = lax.broadcasted_iota(jnp.int32, (bs, heads * bs), 0)
    query_id = lax.broadcasted_iota(jnp.int32, (bs, heads * bs), 1) & (bs - 1)
    init = (jnp.full((1, heads * bs), -jnp.inf, F32), jnp.zeros((1, heads * bs), F32),
            tuple(jnp.zeros((HEAD_DIM, bs), F32) for _ in range(heads)))
    carry = block_update(i, 1, key_id <= query_id, init)

    second_block = lax.broadcasted_iota(jnp.int32, (2 * bs, heads * bs), 0) >= bs

    def body(jj, carry):
        j = 2 * jj
        bits = picks >> j
        keep = (jnp.where(second_block, bits >> 1, bits) & 1) == 1
        return block_update(j, 2, keep, carry)

    _, l, accs = lax.fori_loop(0, (i + 1) // 2, body, carry)
    out_t = jnp.concatenate([accs[g] / l[:, g * bs:(g + 1) * bs] for g in range(heads)], axis=0)
    o_ref[0] = out_t.T.astype(o_ref.dtype)


def _moba(q3, k3, vt4, kmean, heads=4):
    b, t, _ = q3.shape
    nb = t // MOBA_BLOCK
    width = heads * HEAD_DIM
    qblk = pl.BlockSpec((1, MOBA_BLOCK, width), lambda bi, hi, i: (bi, i, hi))
    return pl.pallas_call(
        functools.partial(_moba_kernel, nb=nb, heads=heads),
        grid=(b, N_HEADS // heads, nb),
        in_specs=[qblk, pl.BlockSpec((1, t, width), lambda bi, hi, i: (bi, 0, hi)),
                  pl.BlockSpec((1, nb, width, MOBA_BLOCK), lambda bi, hi, i: (bi, 0, hi, 0)),
                  pl.BlockSpec((1, nb, width), lambda bi, hi, i: (bi, 0, hi))],
        out_specs=qblk,
        out_shape=jax.ShapeDtypeStruct((b, t, BRANCH_DIM), BF16),
        compiler_params=_cparams("parallel", "parallel", "arbitrary"),
        name="moba",
    )(q3, k3, vt4, kmean)


def _rwkv_kernel(r_ref, k_ref, v_ref, lo_ref, hr_ref, hk_ref, hv_ref, hl_ref,
                 mu_r_ref, mu_k_ref, mu_v_ref, mu_l_ref, w0_ref, w2_ref, a0_ref, a2_ref, g2_ref,
                 kk_ref, ka_ref, rk_ref, gng_ref, gnb_ref, o_ref, state_ref):
    i = pl.program_id(1)
    first = i == 0
    tq, ch = RWKV_TILE, RWKV_CHUNK
    hd, bd = HEAD_DIM, BRANCH_DIM

    @pl.when(first)
    def _():
        state_ref[...] = jnp.zeros_like(state_ref)

    def lerp(x_ref, h_ref, mu_ref):
        x = x_ref[0]
        halo = jnp.where(first, 0.0, h_ref[0])
        return x + (_shift_rows(x, halo, 1) - x) * mu_ref[...]

    r = lerp(r_ref, hr_ref, mu_r_ref)
    k = lerp(k_ref, hk_ref, mu_k_ref)
    v = lerp(v_ref, hv_ref, mu_v_ref)
    lo = lerp(lo_ref, hl_ref, mu_l_ref)

    z = w0_ref[...] + _dot(jnp.tanh(lo), w2_ref[...])
    ew = jnp.exp(-0.5) * _sigmoid(z)
    a = _sigmoid(a0_ref[...] + _dot(lo, a2_ref[...]))
    g = _dot(_sigmoid(lo), g2_ref[...])

    half = bd // 2
    lane_r = lax.broadcasted_iota(jnp.int32, (half, half), 0)
    lane_c = lax.broadcasted_iota(jnp.int32, (half, half), 1)
    ones_blocks = jnp.where((lane_r >> HEAD_SHIFT) == (lane_c >> HEAD_SHIFT), 1.0, 0.0).astype(BF16)

    def head_sum(x):
        return jnp.concatenate([_dot_split(x[:, :half], ones_blocks, data_on_left=True),
                                _dot_split(x[:, half:], ones_blocks, data_on_left=True)], axis=1)

    kk = k * kk_ref[...]
    kk = kk / jnp.maximum(jnp.sqrt(head_sum(kk * kk)), 1e-12)
    k2 = k * (1.0 + (a - 1.0) * ka_ref[...])
    bonus = head_sum(r * k2 * rk_ref[...]) * v

    t_r = lax.broadcasted_iota(jnp.int32, (tq, tq), 0)
    t_c = lax.broadcasted_iota(jnp.int32, (tq, tq), 1)
    same_chunk = (t_r >> CHUNK_SHIFT) == (t_c >> CHUNK_SHIFT)
    cs = _dot_split(jnp.where(same_chunk & (t_c <= t_r), 1.0, 0.0).astype(BF16), ew, data_on_left=False)
    cs_end = jnp.concatenate(
        [jnp.broadcast_to(cs[c * ch + ch - 1:c * ch + ch, :], (ch, bd)) for c in range(tq // ch)], axis=0)

    kka = kk * a
    a_t = -kk * jnp.exp(ew - cs)
    b_hat = (kka * jnp.exp(cs)).astype(BF16)
    k_hat = (k2 * jnp.exp(cs)).astype(BF16)
    r_t = r * jnp.exp(-cs)
    b_e = (kka * jnp.exp(cs - cs_end)).astype(BF16)
    k_e = (k2 * jnp.exp(cs - cs_end)).astype(BF16)
    p_c = jnp.exp(-cs_end)
    v_b = v.astype(BF16)

    strict = jnp.where(same_chunk & (t_c < t_r), 1.0, 0.0)
    incl = jnp.where(same_chunk & (t_c <= t_r), 1.0, 0.0)
    eye = jnp.where(t_c == t_r, 1.0, 0.0)
    pw = PAIR_WIDTH
    second = (lax.broadcasted_iota(jnp.int32, (1, pw), 1) >> HEAD_SHIFT) == 1
    same_head = ((lax.broadcasted_iota(jnp.int32, (pw, pw), 0) >> HEAD_SHIFT)
                 == (lax.broadcasted_iota(jnp.int32, (pw, pw), 1) >> HEAD_SHIFT))

    n_pairs = N_HEADS // 2
    pair = lambda x, p: x[:, p * pw:(p + 1) * pw]
    head_list = [(p, hm) for p in range(n_pairs) for hm in (~second, second)]
    a_hs = [jnp.where(hm, pair(a_t, p), 0.0).astype(BF16) for p, hm in head_list]
    l_abs = [strict * _dot_nt(a_h, pair(b_hat, p)) for a_h, (p, _) in zip(a_hs, head_list)]
    t_invs = [eye + x for x in l_abs]
    powers = l_abs
    n = 2
    while n < ch:
        powers = [_dot(x, x) for x in powers]
        t_invs = [t + _dot(t, x) for t, x in zip(t_invs, powers)]
        n *= 2
    l_aks = [strict * _dot_nt(a_h, pair(k_hat, p)) for a_h, (p, _) in zip(a_hs, head_list)]
    lakvs = [_dot(l_ak, pair(v_b, p)) for l_ak, (p, _) in zip(l_aks, head_list)]
    sols = [_dot(t, jnp.concatenate([pair(a_t, p).astype(BF16), lakv.astype(BF16)], axis=1))
            for t, lakv, (p, _) in zip(t_invs, lakvs, head_list)]
    r_hs = [jnp.where(hm, pair(r_t, p), 0.0).astype(BF16) for p, hm in head_list]
    m_rbs = [incl * _dot_nt(r_h, pair(b_hat, p)) for r_h, (p, _) in zip(r_hs, head_list)]
    m_rks = [incl * _dot_nt(r_h, pair(k_hat, p)) for r_h, (p, _) in zip(r_hs, head_list)]
    apps = [_dot(m_rb, sol) for m_rb, sol in zip(m_rbs, sols)]
    mrkvs = [_dot(m_rk, pair(v_b, p)) for m_rk, (p, _) in zip(m_rks, head_list)]

    def both(xs, p, cols):
        return jnp.where(second, xs[2 * p + 1][:, cols], xs[2 * p][:, cols])

    left, right = slice(0, pw), slice(pw, 2 * pw)
    a_til = [both(sols, p, left).astype(BF16) for p in range(n_pairs)]
    u_0 = [both(sols, p, right).astype(BF16) for p in range(n_pairs)]
    r_til = [(both(apps, p, left) + pair(r_t, p)).astype(BF16) for p in range(n_pairs)]
    y_hat = [both(apps, p, right) + both(mrkvs, p, left) for p in range(n_pairs)]

    states = [state_ref[p] for p in range(n_pairs)]
    y_rows = []
    for c in range(tq // ch):
        rows = slice(c * ch, (c + 1) * ch)
        g_cs = [jnp.where(same_head, _dot_tn(a_til[p][rows], pair(b_e, p)[rows]), 0.0) for p in range(n_pairs)]
        h_cs = [jnp.where(same_head, _dot_tn(u_0[p][rows], pair(b_e, p)[rows])
                          + _dot_tn(pair(v_b, p)[rows], pair(k_e, p)[rows]), 0.0) for p in range(n_pairs)]
        y_rows.append(jnp.concatenate(
            [_dot_nt(r_til[p][rows], states[p]) + y_hat[p][rows] for p in range(n_pairs)], axis=1))
        states = [states[p] * pair(p_c, p)[c * ch:c * ch + 1] + _dot(states[p], g_cs[p]) + h_cs[p]
                  for p in range(n_pairs)]
    for p in range(n_pairs):
        state_ref[p] = states[p]
    y = jnp.concatenate(y_rows, axis=0)

    mu = head_sum(y) * (1.0 / hd)
    d = y - mu
    var = head_sum(d * d) * (1.0 / hd)
    y = d * lax.rsqrt(var + GN_EPS) * gng_ref[...] + gnb_ref[...]
    o_ref[0] = ((y + bonus) * g).astype(o_ref.dtype)


def _rwkv(rkv3, lora3, p):
    b, t, _ = rkv3.shape
    tq = RWKV_TILE
    halo = _halo_index(tq)
    blk = lambda col: pl.BlockSpec((1, tq, BRANCH_DIM), lambda bi, i: (bi, i, col))
    hblk = lambda col: pl.BlockSpec((1, SUBLANES, BRANCH_DIM), halo(col))
    vec = lambda width: pl.BlockSpec((1, width), lambda bi, i: (0, 0))
    mat = lambda rows: pl.BlockSpec((rows, BRANCH_DIM), lambda bi, i: (0, 0))
    return pl.pallas_call(
        _rwkv_kernel,
        grid=(b, t // tq),
        in_specs=[blk(0), blk(1), blk(2), pl.BlockSpec((1, tq, LORA_DIM), lambda bi, i: (bi, i, 0)),
                  hblk(0), hblk(1), hblk(2), pl.BlockSpec((1, SUBLANES, LORA_DIM), halo(0)),
                  vec(BRANCH_DIM), vec(BRANCH_DIM), vec(BRANCH_DIM), vec(LORA_DIM),
                  vec(BRANCH_DIM), mat(LORA_DIM), vec(BRANCH_DIM), mat(LORA_DIM), mat(LORA_DIM),
                  vec(BRANCH_DIM), vec(BRANCH_DIM), vec(BRANCH_DIM), vec(BRANCH_DIM), vec(BRANCH_DIM)],
        out_specs=pl.BlockSpec((1, tq, BRANCH_DIM), lambda bi, i: (bi, i, 0)),
        out_shape=jax.ShapeDtypeStruct((b, t, BRANCH_DIM), BF16),
        scratch_shapes=[pltpu.VMEM((N_HEADS // 2, PAIR_WIDTH, PAIR_WIDTH), F32)],
        compiler_params=_cparams("parallel", "arbitrary"),
        name="rwkv7",
    )(rkv3, rkv3, rkv3, lora3, rkv3, rkv3, rkv3, lora3,
      p["mu_r"], p["mu_k"], p["mu_v"], p["mu_l"], p["w0"], p["w2"], p["a0"], p["a2"], p["g2"],
      p["k_k"], p["k_a"], p["r_k"], p["gn_g"], p["gn_b"])


def _merge_kernel(oc_ref, oa_ref, or_ref, gate_ref, x_ref, wb_ref, wo_ref, g_ref, b_ref, o_ref):
    d = D_MODEL
    bd = BRANCH_DIM
    merged = (gate_ref[:, 0:d] * jnp.dot(oc_ref[...], wb_ref[0:bd, :], preferred_element_type=F32)
              + gate_ref[:, d:2 * d] * jnp.dot(oa_ref[...], wb_ref[bd:2 * bd, :], preferred_element_type=F32)
              + gate_ref[:, 2 * d:3 * d] * jnp.dot(or_ref[...], wb_ref[2 * bd:3 * bd, :],
                                                   preferred_element_type=F32))
    h = jnp.dot(merged.astype(BF16), wo_ref[...], preferred_element_type=F32)
    o_ref[...] = _layer_norm(DN_ALPHA * x_ref[...] + h, g_ref[...], b_ref[...])


def _merge(o_conv, o_att, o_rwkv, gates, x2, w_branch, w_out, ln_g, ln_b, tm=512):
    n = x2.shape[0]
    row = lambda i: (i, 0)
    vec = pl.BlockSpec((1, D_MODEL), lambda i: (0, 0))
    return pl.pallas_call(
        _merge_kernel,
        grid=(n // tm,),
        in_specs=[pl.BlockSpec((tm, BRANCH_DIM), row)] * 3
                 + [pl.BlockSpec((tm, GATE_COLS), row), pl.BlockSpec((tm, D_MODEL), row),
                    _vmem_whole(), _vmem_whole(), vec, vec],
        out_specs=pl.BlockSpec((tm, D_MODEL), row),
        out_shape=jax.ShapeDtypeStruct((n, D_MODEL), F32),
        compiler_params=_cparams("parallel"),
        name="merge_ln",
    )(o_conv, o_att, o_rwkv, gates, x2, w_branch, w_out, ln_g, ln_b)


def _swiglu_rows(xb, wg_ref, wu_ref, wd_ref, lead):
    hidden = []
    for c in range(0, D_FF, FF_CHUNK):
        cols = slice(c, c + FF_CHUNK)
        hg = jnp.dot(xb, wg_ref[lead + (slice(None), cols)], preferred_element_type=F32)
        hu = jnp.dot(xb, wu_ref[lead + (slice(None), cols)], preferred_element_type=F32)
        hidden.append((hg * _sigmoid(hg) * hu).astype(BF16))
    return jnp.dot(jnp.concatenate(hidden, axis=1), wd_ref[lead + (slice(None), slice(None))],
                   preferred_element_type=F32)


def _ffn_kernel(x_ref, wg_ref, wu_ref, wd_ref, g_ref, b_ref, o_ref):
    x = x_ref[...]
    f = _swiglu_rows(x.astype(BF16), wg_ref, wu_ref, wd_ref, ())
    o_ref[...] = _layer_norm(DN_ALPHA * x + f, g_ref[...], b_ref[...])


def _ffn(x2, wg, wu, wd, ln_g, ln_b, tm=256):
    n = x2.shape[0]
    row = lambda i: (i, 0)
    vec = pl.BlockSpec((1, D_MODEL), lambda i: (0, 0))
    return pl.pallas_call(
        _ffn_kernel,
        grid=(n // tm,),
        in_specs=[pl.BlockSpec((tm, D_MODEL), row), _vmem_whole(), _vmem_whole(), _vmem_whole(), vec, vec],
        out_specs=pl.BlockSpec((tm, D_MODEL), row),
        out_shape=jax.ShapeDtypeStruct((n, D_MODEL), F32),
        compiler_params=_cparams("parallel"),
        name="ffn_ln",
    )(x2, wg, wu, wd, ln_g, ln_b)


def _router_kernel(x_ref, w_ref, e_ref, g_ref):
    x_hi, x_lo = _split_bf16(x_ref[...])
    w_hi, w_lo = _split_bf16(w_ref[...])
    tm = x_hi.shape[0]
    hi_part = _dot(jnp.concatenate([x_hi, x_lo], axis=0), w_hi)
    logits = hi_part[:tm] + hi_part[tm:] + _dot(x_hi, w_lo)
    lane = lax.broadcasted_iota(jnp.int32, logits.shape, 1).astype(F32)
    m1 = jnp.max(logits, axis=1, keepdims=True)
    e1 = jnp.min(jnp.where(logits == m1, lane, float(N_EXPERTS)), axis=1, keepdims=True)
    rest = jnp.where(lane == e1, -jnp.inf, logits)
    m2 = jnp.max(rest, axis=1, keepdims=True)
    e2 = jnp.min(jnp.where(rest == m2, lane, float(N_EXPERTS)), axis=1, keepdims=True)
    w2 = jnp.exp(m2 - m1)
    denom = 1.0 + w2
    e_ref[...] = jnp.where(lane == 0.0, e1, e2).astype(jnp.int32)
    g_ref[...] = jnp.where(lane == 0.0, 1.0 / denom, w2 / denom)


def _router(x2, router_w, tm=512):
    n = x2.shape[0]
    row = lambda i: (i, 0)
    return pl.pallas_call(
        _router_kernel,
        grid=(n // tm,),
        in_specs=[pl.BlockSpec((tm, D_MODEL), row), pl.BlockSpec((D_MODEL, N_EXPERTS), lambda i: (0, 0))],
        out_specs=[pl.BlockSpec((tm, N_EXPERTS), row), pl.BlockSpec((tm, N_EXPERTS), row)],
        out_shape=[jax.ShapeDtypeStruct((n, N_EXPERTS), jnp.int32),
                   jax.ShapeDtypeStruct((n, N_EXPERTS), F32)],
        compiler_params=_cparams("parallel"),
        name="router",
    )(x2, router_w)


ROW_TILES = D_MODEL // 128


def _to_row_tiles(dst_ref, lead, x):
    for c in range(ROW_TILES):
        dst_ref[lead + (slice(None), c, slice(None))] = x[:, c * 128:(c + 1) * 128]


def _from_row_tiles(src_ref, dtype):
    return jnp.concatenate([src_ref[:, c, :].astype(dtype) for c in range(ROW_TILES)], axis=1)


def _scatter_step(step, n_steps, wait_step, start_this_step):
    @pl.when(step >= 2)
    def _():
        wait_step(step - 2)

    start_this_step()

    @pl.when(step == n_steps - 1)
    def _():
        @pl.when(step >= 1)
        def _():
            wait_step(step - 1)

        wait_step(step)


def _wait_rows(buf_rows, hbm_ref, n_rows, sem):
    pltpu.make_async_copy(buf_rows.at[pl.ds(0, n_rows)], hbm_ref.at[pl.ds(0, n_rows)], sem).wait()


def _dispatch_kernel(dest_ref, x_ref, init_ref, xb_ref, back_ref, buf, sems, *, tm, n_tokens, n_rows):
    del init_ref
    i = pl.program_id(0)

    @pl.when(i == 0)
    def _():
        def clear(r, carry):
            back_ref[r] = 0
            return carry
        lax.fori_loop(0, n_rows, clear, 0, unroll=8)

    def wait_step(step):
        for _ in range(EXPERT_TOPK):
            _wait_rows(buf.at[step % 2], xb_ref, tm, sems.at[step % 2])

    def start():
        slot = i % 2
        _to_row_tiles(buf, (slot,), x_ref[...])

        def one(r, carry):
            for k in range(EXPERT_TOPK):
                dst = dest_ref[(i * tm + r) * EXPERT_TOPK + k]
                back_ref[dst] = k * n_tokens + i * tm + r
                pltpu.make_async_copy(buf.at[slot, r], xb_ref.at[dst], sems.at[slot]).start()
            return carry
        lax.fori_loop(0, tm, one, 0, unroll=4)

    _scatter_step(i, pl.num_programs(0), wait_step, start)


def _dispatch(x2, dest, n_rows, tm=256):
    n = x2.shape[0]
    init = jnp.zeros((n_rows, ROW_TILES, 128), F32)
    return pl.pallas_call(
        functools.partial(_dispatch_kernel, tm=tm, n_tokens=n, n_rows=n_rows),
        grid_spec=pltpu.PrefetchScalarGridSpec(
            num_scalar_prefetch=1,
            grid=(n // tm,),
            in_specs=[pl.BlockSpec((tm, D_MODEL), lambda i, d: (i, 0)), pl.BlockSpec(memory_space=pl.ANY)],
            out_specs=[pl.BlockSpec(memory_space=pl.ANY), pl.BlockSpec(memory_space=pltpu.SMEM)],
            scratch_shapes=[pltpu.VMEM((2, tm, ROW_TILES, 128), F32), pltpu.SemaphoreType.DMA((2,))],
        ),
        out_shape=[jax.ShapeDtypeStruct((n_rows, ROW_TILES, 128), F32),
                   jax.ShapeDtypeStruct((n_rows,), jnp.int32)],
        input_output_aliases={2: 0},
        compiler_params=_cparams("arbitrary"),
        name="moe_dispatch",
    )(dest, x2, init)


def _moe_kernel(blk_e_ref, n_valid_ref, row_dst_ref, x_ref, wg_ref, wu_ref, wd_ref, y_ref, xs, ybuf, sems):
    del blk_e_ref
    i = pl.program_id(0)

    def wait_step(step):
        @pl.when(n_valid_ref[step] > 0)
        def _():
            _wait_rows(ybuf.at[step % 2], y_ref, n_valid_ref[step], sems.at[step % 2])

    def start():
        @pl.when(n_valid_ref[i] > 0)
        def _():
            slot = i % 2
            xs[...] = _from_row_tiles(x_ref, BF16)
            _to_row_tiles(ybuf, (slot,), _swiglu_rows(xs[...], wg_ref, wu_ref, wd_ref, (0,)))

            def one(r, carry):
                dst = row_dst_ref[i * MOE_ROWS + r]
                pltpu.make_async_copy(ybuf.at[slot, r], y_ref.at[dst], sems.at[slot]).start()
                return carry
            lax.fori_loop(0, n_valid_ref[i], one, 0)

    _scatter_step(i, pl.num_programs(0), wait_step, start)


def _moe_experts(xb, blk_e, n_valid, row_dst, wg, wu, wd, n_out_rows):
    n_blocks = xb.shape[0] // MOE_ROWS
    weight = lambda shape: pl.BlockSpec((1,) + shape, lambda i, be, nv, rd: (be[i], 0, 0))
    return pl.pallas_call(
        _moe_kernel,
        grid_spec=pltpu.PrefetchScalarGridSpec(
            num_scalar_prefetch=3,
            grid=(n_blocks,),
            in_specs=[pl.BlockSpec((MOE_ROWS, ROW_TILES, 128), lambda i, be, nv, rd: (i, 0, 0)),
                      weight((D_MODEL, D_FF)), weight((D_MODEL, D_FF)), weight((D_FF, D_MODEL))],
            out_specs=pl.BlockSpec(memory_space=pl.ANY),
            scratch_shapes=[pltpu.VMEM((MOE_ROWS, D_MODEL), BF16),
                            pltpu.VMEM((2, MOE_ROWS, ROW_TILES, 128), F32),
                            pltpu.SemaphoreType.DMA((2,))],
        ),
        out_shape=jax.ShapeDtypeStruct((n_out_rows, ROW_TILES, 128), F32),
        compiler_params=_cparams("arbitrary"),
        name="moe_experts",
    )(blk_e, n_valid, row_dst, xb, wg, wu, wd)


def _combine_kernel(x_ref, y0_ref, y1_ref, gate_ref, g_ref, b_ref, o_ref):
    gate = gate_ref[...]
    f = gate[:, 0:1] * _from_row_tiles(y0_ref, F32) + gate[:, 1:2] * _from_row_tiles(y1_ref, F32)
    o_ref[...] = _layer_norm(DN_ALPHA * x_ref[...] + f, g_ref[...], b_ref[...])


def _combine(x2, y_slots, gate, ln_g, ln_b, tm=512):
    n = x2.shape[0]
    row = lambda i: (i, 0)
    vec = pl.BlockSpec((1, D_MODEL), lambda i: (0, 0))
    return pl.pallas_call(
        _combine_kernel,
        grid=(n // tm,),
        in_specs=[pl.BlockSpec((tm, D_MODEL), row),
                  pl.BlockSpec((tm, ROW_TILES, 128), lambda i: (i, 0, 0)),
                  pl.BlockSpec((tm, ROW_TILES, 128), lambda i: (n // tm + i, 0, 0)),
                  pl.BlockSpec((tm, EXPERT_TOPK), row), vec, vec],
        out_specs=pl.BlockSpec((tm, D_MODEL), row),
        out_shape=jax.ShapeDtypeStruct((n, D_MODEL), F32),
        compiler_params=_cparams("parallel"),
        name="combine_ln",
    )(x2, y_slots, y_slots, gate, ln_g, ln_b)


def _moe_layer(x2, router_w, wg, wu, wd, ln_g, ln_b):
    n = x2.shape[0]
    nk = n * EXPERT_TOPK
    top_e, gate = _router(x2, router_w)
    top_e = top_e[:, :EXPERT_TOPK]
    gate = gate[:, :EXPERT_TOPK]
    flat_e = top_e.reshape(-1)
    onehot = (flat_e[:, None] == jnp.arange(N_EXPERTS, dtype=jnp.int32)[None, :]).astype(jnp.int32)
    counts = jnp.sum(onehot, axis=0)
    rank = jnp.sum((jnp.cumsum(onehot, axis=0) - onehot) * onehot, axis=1)
    padded = (counts + MOE_ROWS - 1) // MOE_ROWS * MOE_ROWS
    pad_end = jnp.cumsum(padded)
    pad_start = pad_end - padded
    dest = (pad_start[flat_e] + rank).astype(jnp.int32)
    n_blocks = -(-nk // MOE_ROWS) + N_EXPERTS
    n_rows = n_blocks * MOE_ROWS
    blk_start = jnp.arange(n_blocks, dtype=jnp.int32) * MOE_ROWS
    blk_e = jnp.sum((blk_start[:, None] >= pad_end[None, :]).astype(jnp.int32), axis=1)
    valid_end = jnp.concatenate([pad_start + counts, jnp.zeros((1,), jnp.int32)])
    n_valid = jnp.clip(valid_end[blk_e] - blk_start, 0, MOE_ROWS).astype(jnp.int32)
    blk_e = jnp.minimum(blk_e, N_EXPERTS - 1).astype(jnp.int32)
    xb, row_dst = _dispatch(x2, dest, n_rows)
    y_slots = _moe_experts(xb, blk_e, n_valid, row_dst, wg, wu, wd, nk)
    return _combine(x2, y_slots, gate, ln_g, ln_b)


def _rope_tables(t):
    half = HEAD_DIM // 2
    inv = ROPE_THETA ** (-jnp.arange(half, dtype=F32) / half)
    ang = jnp.arange(t).astype(F32)[:, None] * inv[None, :]
    cos = jnp.cos(ang)
    sin = jnp.sin(ang)
    cos_t = jnp.tile(jnp.concatenate([cos, cos], -1), (1, N_HEADS))
    sin_t = jnp.tile(jnp.concatenate([-sin, sin], -1), (1, N_HEADS))
    return cos_t, sin_t


def _pad_rows(w, start):
    out = jnp.zeros((LORA_DIM, BRANCH_DIM), F32)
    return lax.dynamic_update_slice(out, w, (start, 0)).astype(BF16)


def _mixer(x2, b, t, cos_t, sin_t, w_in, conv_w, shift_mu, decay_w0, decay_w2, aaa_a0, aaa_w2, gate_w2,
           k_k, k_a, r_k, gn_g, gn_b, w_branch, w_out, ln_g, ln_b):
    w_in = w_in.astype(BF16)
    o_conv, q3, k3, vt4, kmean, rkv3, lora3, gates = _in_proj(
        x2.reshape(b, t, D_MODEL), w_in[:, :MIX_COLS], w_in[:, MIX_COLS:MIX_COLS + LORA_DIM],
        w_in[:, MIX_COLS + LORA_DIM:], conv_w, cos_t, sin_t)
    o_att = _moba(q3, k3, vt4, kmean.reshape(b, t // MOBA_BLOCK, BRANCH_DIM)).reshape(b * t, BRANCH_DIM)

    row = lambda a: a.reshape(1, -1)
    p = {
        "mu_r": row(shift_mu[0:BRANCH_DIM]), "mu_k": row(shift_mu[BRANCH_DIM:2 * BRANCH_DIM]),
        "mu_v": row(shift_mu[2 * BRANCH_DIM:3 * BRANCH_DIM]), "mu_l": row(shift_mu[3 * BRANCH_DIM:]),
        "w0": row(decay_w0), "w2": _pad_rows(decay_w2, 0),
        "a0": row(aaa_a0), "a2": _pad_rows(aaa_w2, DECAY_LORA),
        "g2": _pad_rows(gate_w2, DECAY_LORA + AAA_LORA),
        "k_k": row(k_k), "k_a": row(k_a), "r_k": row(r_k), "gn_g": row(gn_g), "gn_b": row(gn_b),
    }
    o_rwkv = _rwkv(rkv3, lora3, p)

    return _merge(o_conv.reshape(b * t, BRANCH_DIM), o_att, o_rwkv.reshape(b * t, BRANCH_DIM),
                  gates.reshape(b * t, GATE_COLS), x2,
                  w_branch.astype(BF16), w_out.astype(BF16), row(ln_g), row(ln_b))


def kernel(x, w_in, conv_w, shift_mu, decay_w0, decay_w2, aaa_a0, aaa_w2, gate_w2, k_k, k_a, r_k, gn_g, gn_b,
           w_branch, w_out, ln1_g, ln1_b, ln2_g, ln2_b, ffn_w_gate, ffn_w_up, ffn_w_down, router_w,
           moe_w_gate, moe_w_up, moe_w_down):
    b, t, d = x.shape
    x2 = x.reshape(b * t, d)
    cos_t, sin_t = _rope_tables(t)
    row = lambda a: a.reshape(1, -1)
    for l in range(DEPTH):
        x2 = _mixer(x2, b, t, cos_t, sin_t, w_in[l], conv_w[l], shift_mu[l], decay_w0[l], decay_w2[l],
                    aaa_a0[l], aaa_w2[l], gate_w2[l], k_k[l], k_a[l], r_k[l], gn_g[l], gn_b[l],
                    w_branch[l], w_out[l], ln1_g[l], ln1_b[l])
        j = l // 2
        if l % 2 == 0:
            x2 = _ffn(x2, ffn_w_gate[j].astype(BF16), ffn_w_up[j].astype(BF16), ffn_w_down[j].astype(BF16),
                      row(ln2_g[l]), row(ln2_b[l]))
        else:
            x2 = _moe_layer(x2, router_w[j], moe_w_gate[j].astype(BF16), moe_w_up[j].astype(BF16),
                            moe_w_down[j].astype(BF16), row(ln2_g[l]), row(ln2_b[l]))
    return x2.reshape(b, t, d)
```

```python
import functools

import jax
import jax.numpy as jnp
from jax import lax
from jax.experimental import pallas as pl
from jax.experimental.pallas import tpu as pltpu

F32 = jnp.float32
BF16 = jnp.bfloat16

D_MODEL = 1024
HEAD_DIM = 64
BRANCH_DIM = 512
N_HEADS = BRANCH_DIM // HEAD_DIM
DECAY_LORA = 64
AAA_LORA = 64
GATE_LORA = 128
LORA_DIM = DECAY_LORA + AAA_LORA + GATE_LORA
MIX_COLS = 9 * BRANCH_DIM
GATE_COLS = 3 * D_MODEL
MOBA_BLOCK = 256
MOBA_TOPK = 3
ROPE_THETA = 10000.0
D_FF = 2816
N_EXPERTS = 8
EXPERT_TOPK = 2
MOE_ROWS = 256
LN_EPS = 1e-5
GN_EPS = 64e-5
DEPTH = 2
DN_ALPHA = (2 * DEPTH) ** 0.25
LOG2_E = 1.4426950408889634

VMEM_LIMIT_BYTES = 56 * 1024 * 1024
SUBLANES = 8
RWKV_TILE = 256
RWKV_CHUNK = 64
FF_CHUNK = 256
PAIR_WIDTH = 2 * HEAD_DIM
HEAD_SHIFT = HEAD_DIM.bit_length() - 1
CHUNK_SHIFT = RWKV_CHUNK.bit_length() - 1


def _cparams(*sem):
    return pltpu.CompilerParams(dimension_semantics=sem, vmem_limit_bytes=VMEM_LIMIT_BYTES)


def _vmem_whole():
    return pl.BlockSpec(memory_space=pltpu.VMEM)


def _dot(a, b):
    return jnp.dot(a.astype(BF16), b.astype(BF16), preferred_element_type=F32)


def _dot_nt(a, b):
    return lax.dot_general(a.astype(BF16), b.astype(BF16), (((1,), (1,)), ((), ())),
                           preferred_element_type=F32)


def _dot_tn(a, b):
    return lax.dot_general(a.astype(BF16), b.astype(BF16), (((0,), (0,)), ((), ())),
                           preferred_element_type=F32)


def _split_bf16(x):
    hi = x.astype(BF16)
    return hi, (x - hi.astype(F32)).astype(BF16)


def _dot_split(a, b, *, data_on_left):
    hi, lo = _split_bf16(a if data_on_left else b)
    if data_on_left:
        return jnp.dot(hi, b, preferred_element_type=F32) + jnp.dot(lo, b, preferred_element_type=F32)
    return jnp.dot(a, hi, preferred_element_type=F32) + jnp.dot(a, lo, preferred_element_type=F32)


def _sigmoid(x):
    return 1.0 / (1.0 + jnp.exp(-x))


def _layer_norm(y, g, b):
    mu = jnp.mean(y, axis=-1, keepdims=True)
    d = y - mu
    var = jnp.mean(d * d, axis=-1, keepdims=True)
    return d * lax.rsqrt(var + LN_EPS) * g + b


def _shift_rows(u, halo, n):
    out = pltpu.roll(u, n, axis=0)
    row = lax.broadcasted_iota(jnp.int32, u.shape, 0)
    for r in range(n):
        src = halo[SUBLANES - n + r:SUBLANES - n + r + 1, :]
        out = jnp.where(row == r, src, out)
    return out


def _halo_index(tile_rows):
    step = tile_rows // SUBLANES

    def index(col):
        return lambda b, i: (b, jnp.maximum(i * step - 1, 0), col)
    return index


def _in_proj_kernel(x_ref, wm_ref, wl_ref, wg_ref, convw_ref, cos_ref, sin_ref,
                    conv_ref, q_ref, k_ref, vt_ref, km_ref, rkv_ref, lora_ref, gate_ref, halo_ref):
    first = pl.program_id(1) == 0
    bd = BRANCH_DIM
    xb = x_ref[0].astype(BF16)
    proj = lambda g: jnp.dot(xb, wm_ref[:, g * bd:(g + 1) * bd], preferred_element_type=F32)

    u = proj(2) * proj(0)
    halo = jnp.where(first, 0.0, halo_ref[...])
    w = convw_ref[...]
    y = w[2:3, :] * u + w[1:2, :] * _shift_rows(u, halo, 1) + w[0:1, :] * _shift_rows(u, halo, 2)
    conv_ref[0] = (proj(1) * y).astype(conv_ref.dtype)
    halo_ref[...] = u[u.shape[0] - SUBLANES:, :]

    cos = cos_ref[...]
    sin = sin_ref[...]
    lane = lax.broadcasted_iota(jnp.int32, cos.shape, 1)
    first_half = (lane & (HEAD_DIM - 1)) < (HEAD_DIM // 2)

    def rope(t):
        swapped = jnp.where(first_half, pltpu.roll(t, bd - HEAD_DIM // 2, axis=1),
                            pltpu.roll(t, HEAD_DIM // 2, axis=1))
        return t * cos + swapped * sin

    kr = rope(proj(4))
    q_ref[0] = rope(proj(3))
    k_ref[0] = kr.astype(BF16)
    km_ref[0, 0] = jnp.mean(kr, axis=0, keepdims=True)
    vt_ref[0, 0] = proj(5).astype(BF16).T

    for g in range(3):
        rkv_ref[0, :, g * bd:(g + 1) * bd] = proj(6 + g)
    lora_ref[0] = jnp.dot(xb, wl_ref[...], preferred_element_type=F32)
    for j in range(0, GATE_COLS, bd):
        g = jnp.dot(xb, wg_ref[:, j:j + bd], preferred_element_type=F32)
        gate_ref[0, :, j:j + bd] = _sigmoid(g).astype(gate_ref.dtype)


def _in_proj(x3, w_mix, w_lora, w_gate, conv_w, cos_t, sin_t):
    b, t, _ = x3.shape
    tm = MOBA_BLOCK
    nb = t // tm
    blk = lambda width: pl.BlockSpec((1, tm, width), lambda bi, i: (bi, i, 0))
    tab = pl.BlockSpec((tm, BRANCH_DIM), lambda bi, i: (i, 0))
    shape3 = lambda width, dtype: jax.ShapeDtypeStruct((b, t, width), dtype)
    return pl.pallas_call(
        _in_proj_kernel,
        grid=(b, nb),
        in_specs=[blk(D_MODEL), _vmem_whole(), _vmem_whole(), _vmem_whole(),
                  pl.BlockSpec((3, BRANCH_DIM), lambda bi, i: (0, 0)), tab, tab],
        out_specs=[blk(BRANCH_DIM), blk(BRANCH_DIM), blk(BRANCH_DIM),
                   pl.BlockSpec((1, 1, BRANCH_DIM, tm), lambda bi, i: (bi, i, 0, 0)),
                   pl.BlockSpec((1, 1, 1, BRANCH_DIM), lambda bi, i: (bi, i, 0, 0)),
                   blk(3 * BRANCH_DIM), blk(LORA_DIM), blk(GATE_COLS)],
        out_shape=[shape3(BRANCH_DIM, BF16), shape3(BRANCH_DIM, F32), shape3(BRANCH_DIM, BF16),
                   jax.ShapeDtypeStruct((b, nb, BRANCH_DIM, tm), BF16),
                   jax.ShapeDtypeStruct((b, nb, 1, BRANCH_DIM), F32),
                   shape3(3 * BRANCH_DIM, F32), shape3(LORA_DIM, F32), shape3(GATE_COLS, BF16)],
        scratch_shapes=[pltpu.VMEM((SUBLANES, BRANCH_DIM), F32)],
        compiler_params=_cparams("parallel", "arbitrary"),
        name="in_proj",
    )(x3, w_mix, w_lora, w_gate, conv_w, cos_t, sin_t)


def _moba_kernel(q_ref, k_ref, vt_ref, km_ref, o_ref, *, nb, heads):
    i = pl.program_id(2)
    bs = MOBA_BLOCK
    width = heads * HEAD_DIM
    q = q_ref[0]
    km = km_ref[0]
    head_of_lane = lax.broadcasted_iota(jnp.int32, (1, width), 1) >> HEAD_SHIFT
    q_heads = jnp.concatenate([jnp.where(head_of_lane == g, q, 0.0) for g in range(heads)], axis=0)
    q_all = (q_heads * (HEAD_DIM ** -0.5 * LOG2_E)).astype(BF16)

    blk = lax.broadcasted_iota(jnp.int32, (nb, heads * bs), 0)
    valid = blk < i
    km_heads = jnp.concatenate([jnp.where(head_of_lane == g, km, 0.0) for g in range(heads)], axis=0)
    km_hi, km_lo = _split_bf16(km_heads)
    q_hi, q_lo = _split_bf16(q)
    hi_part = _dot_nt(jnp.concatenate([km_hi, km_lo], axis=0), q_hi)
    gate = hi_part[:heads * nb] + hi_part[heads * nb:] + _dot_nt(km_hi, q_lo)
    gate = jnp.concatenate([gate[g * nb:(g + 1) * nb] for g in range(heads)], axis=1)
    gate = jnp.where(valid, gate, -jnp.inf)
    rank = jnp.zeros((nb, heads * bs), jnp.int32)
    for m in range(nb):
        gm = gate[m:m + 1, :]
        beats = (gm > gate) | ((gm == gate) & (blk > m))
        rank = rank + beats.astype(jnp.int32)
    chosen = jnp.where(valid & (rank < MOBA_TOPK), (1 << blk).astype(F32), 0.0)
    picks = jnp.sum(chosen, axis=0, keepdims=True).astype(jnp.int32)

    ones_rows = jnp.ones((SUBLANES, bs), BF16)

    def scores(j, n_blocks):
        kb = k_ref[0, pl.ds(pl.multiple_of(j * bs, bs), n_blocks * bs), :]
        return lax.dot_general(kb, q_all, (((1,), (1,)), ((), ())), preferred_element_type=F32)

    def block_update(j, s, chosen, causal, carry):
        n_blocks = len(chosen)
        m, accs = carry
        if causal is not None:
            s = jnp.where(causal, s, -jnp.inf)
        parts = [s[n * bs:(n + 1) * bs] for n in range(n_blocks)]
        tops = [jnp.max(part, axis=0, keepdims=True) for part in parts]
        tops = [t if c is None else jnp.where(c, t, -jnp.inf) for t, c in zip(tops, chosen)]
        m_new = functools.reduce(jnp.maximum, tops, m)
        alpha = jnp.exp2(m - m_new)
        shifts = [m_new if c is None else jnp.where(c, m_new, jnp.inf) for c in chosen]
        ps = [jnp.exp2(part - shift).astype(BF16) for part, shift in zip(parts, shifts)]
        new_acc = []
        for g in range(heads):
            cols = slice(g * bs, (g + 1) * bs)
            acc = alpha[:, cols] * accs[g]
            for n in range(n_blocks):
                vt = jnp.concatenate([vt_ref[0, j + n, g * HEAD_DIM:(g + 1) * HEAD_DIM, :], ones_rows], axis=0)
                acc = acc + jnp.dot(vt, ps[n][:, cols], preferred_element_type=F32)
            new_acc.append(acc)
        return m_new, tuple(new_acc)

    key_id = lax.broadcasted_iota(jnp.int32, (bs, heads * bs), 0)
    query_id = lax.broadcasted_iota(jnp.int32, (bs, heads * bs), 1) & (bs - 1)
    init = (jnp.full((1, heads * bs), -jnp.inf, F32),
            tuple(jnp.zeros((HEAD_DIM + SUBLANES, bs), F32) for _ in range(heads)))
    carry = block_update(i, scores(i, 1), [None], key_id <= query_id, init)

    def pair_update(j, s, carry):
        bits = picks >> j
        return block_update(j, s, [(bits & 1) == 1, ((bits >> 1) & 1) == 1], None, carry)

    def two_pairs(jq, carry):
        j = 4 * jq
        s_a, s_b = scores(j, 2), scores(j + 2, 2)
        return pair_update(j + 2, s_b, pair_update(j, s_a, carry))

    carry = lax.fori_loop(0, i // 4, two_pairs, carry)
    done = (i // 4) * 4
    _, accs = lax.fori_loop(0, (i - done + 1) // 2,
                            lambda jj, c: pair_update(done + 2 * jj, scores(done + 2 * jj, 2), c), carry)
    out_t = jnp.concatenate([acc[:HEAD_DIM] / acc[HEAD_DIM:HEAD_DIM + 1] for acc in accs], axis=0)
    o_ref[0] = out_t.T.astype(o_ref.dtype)


def _moba(q3, k3, vt4, kmean, heads=4):
    b, t, _ = q3.shape
    nb = t // MOBA_BLOCK
    width = heads * HEAD_DIM
    qblk = pl.BlockSpec((1, MOBA_BLOCK, width), lambda bi, hi, i: (bi, i, hi))
    return pl.pallas_call(
        functools.partial(_moba_kernel, nb=nb, heads=heads),
        grid=(b, N_HEADS // heads, nb),
        in_specs=[qblk, pl.BlockSpec((1, t, width), lambda bi, hi, i: (bi, 0, hi)),
                  pl.BlockSpec((1, nb, width, MOBA_BLOCK), lambda bi, hi, i: (bi, 0, hi, 0)),
                  pl.BlockSpec((1, nb, width), lambda bi, hi, i: (bi, 0, hi))],
        out_specs=qblk,
        out_shape=jax.ShapeDtypeStruct((b, t, BRANCH_DIM), BF16),
        compiler_params=_cparams("parallel", "parallel", "arbitrary"),
        name="moba",
    )(q3, k3, vt4, kmean)


def _rwkv_kernel(r_ref, k_ref, v_ref, lo_ref, hr_ref, hk_ref, hv_ref, hl_ref,
                 mu_r_ref, mu_k_ref, mu_v_ref, mu_l_ref, w0_ref, w2_ref, a0_ref, a2_ref, g2_ref,
                 kk_ref, ka_ref, rk_ref, gng_ref, gnb_ref, o_ref, state_ref):
    i = pl.program_id(1)
    first = i == 0
    tq, ch = RWKV_TILE, RWKV_CHUNK
    hd, bd = HEAD_DIM, BRANCH_DIM

    @pl.when(first)
    def _():
        state_ref[...] = jnp.zeros_like(state_ref)

    def lerp(x_ref, h_ref, mu_ref):
        x = x_ref[0]
        halo = jnp.where(first, 0.0, h_ref[0])
        return x + (_shift_rows(x, halo, 1) - x) * mu_ref[...]

    r = lerp(r_ref, hr_ref, mu_r_ref)
    k = lerp(k_ref, hk_ref, mu_k_ref)
    v = lerp(v_ref, hv_ref, mu_v_ref)
    lo = lerp(lo_ref, hl_ref, mu_l_ref)

    z = w0_ref[...] + _dot(jnp.tanh(lo), w2_ref[...])
    ew = jnp.exp(-0.5) * _sigmoid(z)
    a = _sigmoid(a0_ref[...] + _dot(lo, a2_ref[...]))
    g = _dot(_sigmoid(lo), g2_ref[...])

    half = bd // 2
    lane_r = lax.broadcasted_iota(jnp.int32, (half, half), 0)
    lane_c = lax.broadcasted_iota(jnp.int32, (half, half), 1)
    ones_blocks = jnp.where((lane_r >> HEAD_SHIFT) == (lane_c >> HEAD_SHIFT), 1.0, 0.0).astype(BF16)

    def head_sum(x):
        return jnp.concatenate([_dot_split(x[:, :half], ones_blocks, data_on_left=True),
                                _dot_split(x[:, half:], ones_blocks, data_on_left=True)], axis=1)

    kk = k * kk_ref[...]
    kk = kk / jnp.maximum(jnp.sqrt(head_sum(kk * kk)), 1e-12)
    k2 = k * (1.0 + (a - 1.0) * ka_ref[...])
    bonus = head_sum(r * k2 * rk_ref[...]) * v

    t_r = lax.broadcasted_iota(jnp.int32, (tq, tq), 0)
    t_c = lax.broadcasted_iota(jnp.int32, (tq, tq), 1)
    same_chunk = (t_r >> CHUNK_SHIFT) == (t_c >> CHUNK_SHIFT)
    cs = _dot_split(jnp.where(same_chunk & (t_c <= t_r), 1.0, 0.0).astype(BF16), ew, data_on_left=False)
    cs_end = jnp.concatenate(
        [jnp.broadcast_to(cs[c * ch + ch - 1:c * ch + ch, :], (ch, bd)) for c in range(tq // ch)], axis=0)

    kka = kk * a
    a_t = -kk * jnp.exp(ew - cs)
    b_hat = (kka * jnp.exp(cs)).astype(BF16)
    k_hat = (k2 * jnp.exp(cs)).astype(BF16)
    r_t = r * jnp.exp(-cs)
    b_e = (kka * jnp.exp(cs - cs_end)).astype(BF16)
    k_e = (k2 * jnp.exp(cs - cs_end)).astype(BF16)
    p_c = jnp.exp(-cs_end)
    v_b = v.astype(BF16)

    strict = jnp.where(same_chunk & (t_c < t_r), 1.0, 0.0)
    incl = jnp.where(same_chunk & (t_c <= t_r), 1.0, 0.0)
    eye = jnp.where(t_c == t_r, 1.0, 0.0)
    pw = PAIR_WIDTH
    second = (lax.broadcasted_iota(jnp.int32, (1, pw), 1) >> HEAD_SHIFT) == 1
    same_head = ((lax.broadcasted_iota(jnp.int32, (pw, pw), 0) >> HEAD_SHIFT)
                 == (lax.broadcasted_iota(jnp.int32, (pw, pw), 1) >> HEAD_SHIFT))

    n_pairs = N_HEADS // 2
    pair = lambda x, p: x[:, p * pw:(p + 1) * pw]
    head_list = [(p, hm) for p in range(n_pairs) for hm in (~second, second)]
    a_hs = [jnp.where(hm, pair(a_t, p), 0.0).astype(BF16) for p, hm in head_list]
    l_abs = [strict * _dot_nt(a_h, pair(b_hat, p)) for a_h, (p, _) in zip(a_hs, head_list)]
    t_invs = [eye + x for x in l_abs]
    powers = l_abs
    n = 2
    while n < ch:
        powers = [_dot(x, x) for x in powers]
        t_invs = [t + _dot(t, x) for t, x in zip(t_invs, powers)]
        n *= 2
    l_aks = [strict * _dot_nt(a_h, pair(k_hat, p)) for a_h, (p, _) in zip(a_hs, head_list)]
    lakvs = [_dot(l_ak, pair(v_b, p)) for l_ak, (p, _) in zip(l_aks, head_list)]
    sols = [_dot(t, jnp.concatenate([pair(a_t, p).astype(BF16), lakv.astype(BF16)], axis=1))
            for t, lakv, (p, _) in zip(t_invs, lakvs, head_list)]
    r_hs = [jnp.where(hm, pair(r_t, p), 0.0).astype(BF16) for p, hm in head_list]
    m_rbs = [incl * _dot_nt(r_h, pair(b_hat, p)) for r_h, (p, _) in zip(r_hs, head_list)]
    m_rks = [incl * _dot_nt(r_h, pair(k_hat, p)) for r_h, (p, _) in zip(r_hs, head_list)]
    apps = [_dot(m_rb, sol) for m_rb, sol in zip(m_rbs, sols)]
    mrkvs = [_dot(m_rk, pair(v_b, p)) for m_rk, (p, _) in zip(m_rks, head_list)]

    def both(xs, p, cols):
        return jnp.where(second, xs[2 * p + 1][:, cols], xs[2 * p][:, cols])

    left, right = slice(0, pw), slice(pw, 2 * pw)
    a_til = [both(sols, p, left).astype(BF16) for p in range(n_pairs)]
    u_0 = [both(sols, p, right).astype(BF16) for p in range(n_pairs)]
    r_til = [(both(apps, p, left) + pair(r_t, p)).astype(BF16) for p in range(n_pairs)]
    y_hat = [both(apps, p, right) + both(mrkvs, p, left) for p in range(n_pairs)]

    states = [state_ref[p] for p in range(n_pairs)]
    y_rows = []
    for c in range(tq // ch):
        rows = slice(c * ch, (c + 1) * ch)
        g_cs = [jnp.where(same_head, _dot_tn(a_til[p][rows], pair(b_e, p)[rows]), 0.0) for p in range(n_pairs)]
        h_cs = [jnp.where(same_head, _dot_tn(u_0[p][rows], pair(b_e, p)[rows])
                          + _dot_tn(pair(v_b, p)[rows], pair(k_e, p)[rows]), 0.0) for p in range(n_pairs)]
        y_rows.append(jnp.concatenate(
            [_dot_nt(r_til[p][rows], states[p]) + y_hat[p][rows] for p in range(n_pairs)], axis=1))
        states = [states[p] * pair(p_c, p)[c * ch:c * ch + 1] + _dot(states[p], g_cs[p]) + h_cs[p]
                  for p in range(n_pairs)]
    for p in range(n_pairs):
        state_ref[p] = states[p]
    y = jnp.concatenate(y_rows, axis=0)

    mu = head_sum(y) * (1.0 / hd)
    d = y - mu
    var = head_sum(d * d) * (1.0 / hd)
    y = d * lax.rsqrt(var + GN_EPS) * gng_ref[...] + gnb_ref[...]
    o_ref[0] = ((y + bonus) * g).astype(o_ref.dtype)


def _rwkv(rkv3, lora3, p):
    b, t, _ = rkv3.shape
    tq = RWKV_TILE
    halo = _halo_index(tq)
    blk = lambda col: pl.BlockSpec((1, tq, BRANCH_DIM), lambda bi, i: (bi, i, col))
    hblk = lambda col: pl.BlockSpec((1, SUBLANES, BRANCH_DIM), halo(col))
    vec = lambda width: pl.BlockSpec((1, width), lambda bi, i: (0, 0))
    mat = lambda rows: pl.BlockSpec((rows, BRANCH_DIM), lambda bi, i: (0, 0))
    return pl.pallas_call(
        _rwkv_kernel,
        grid=(b, t // tq),
        in_specs=[blk(0), blk(1), blk(2), pl.BlockSpec((1, tq, LORA_DIM), lambda bi, i: (bi, i, 0)),
                  hblk(0), hblk(1), hblk(2), pl.BlockSpec((1, SUBLANES, LORA_DIM), halo(0)),
                  vec(BRANCH_DIM), vec(BRANCH_DIM), vec(BRANCH_DIM), vec(LORA_DIM),
                  vec(BRANCH_DIM), mat(LORA_DIM), vec(BRANCH_DIM), mat(LORA_DIM), mat(LORA_DIM),
                  vec(BRANCH_DIM), vec(BRANCH_DIM), vec(BRANCH_DIM), vec(BRANCH_DIM), vec(BRANCH_DIM)],
        out_specs=pl.BlockSpec((1, tq, BRANCH_DIM), lambda bi, i: (bi, i, 0)),
        out_shape=jax.ShapeDtypeStruct((b, t, BRANCH_DIM), BF16),
        scratch_shapes=[pltpu.VMEM((N_HEADS // 2, PAIR_WIDTH, PAIR_WIDTH), F32)],
        compiler_params=_cparams("parallel", "arbitrary"),
        name="rwkv7",
    )(rkv3, rkv3, rkv3, lora3, rkv3, rkv3, rkv3, lora3,
      p["mu_r"], p["mu_k"], p["mu_v"], p["mu_l"], p["w0"], p["w2"], p["a0"], p["a2"], p["g2"],
      p["k_k"], p["k_a"], p["r_k"], p["gn_g"], p["gn_b"])


def _merge_kernel(oc_ref, oa_ref, or_ref, gate_ref, x_ref, wb_ref, wo_ref, g_ref, b_ref, o_ref):
    d = D_MODEL
    bd = BRANCH_DIM
    merged = (gate_ref[:, 0:d] * jnp.dot(oc_ref[...], wb_ref[0:bd, :], preferred_element_type=F32)
              + gate_ref[:, d:2 * d] * jnp.dot(oa_ref[...], wb_ref[bd:2 * bd, :], preferred_element_type=F32)
              + gate_ref[:, 2 * d:3 * d] * jnp.dot(or_ref[...], wb_ref[2 * bd:3 * bd, :],
                                                   preferred_element_type=F32))
    h = jnp.dot(merged.astype(BF16), wo_ref[...], preferred_element_type=F32)
    o_ref[...] = _layer_norm(DN_ALPHA * x_ref[...] + h, g_ref[...], b_ref[...])


def _merge(o_conv, o_att, o_rwkv, gates, x2, w_branch, w_out, ln_g, ln_b, tm=512):
    n = x2.shape[0]
    row = lambda i: (i, 0)
    vec = pl.BlockSpec((1, D_MODEL), lambda i: (0, 0))
    return pl.pallas_call(
        _merge_kernel,
        grid=(n // tm,),
        in_specs=[pl.BlockSpec((tm, BRANCH_DIM), row)] * 3
                 + [pl.BlockSpec((tm, GATE_COLS), row), pl.BlockSpec((tm, D_MODEL), row),
                    _vmem_whole(), _vmem_whole(), vec, vec],
        out_specs=pl.BlockSpec((tm, D_MODEL), row),
        out_shape=jax.ShapeDtypeStruct((n, D_MODEL), F32),
        compiler_params=_cparams("parallel"),
        name="merge_ln",
    )(o_conv, o_att, o_rwkv, gates, x2, w_branch, w_out, ln_g, ln_b)


def _swiglu_rows(xb, wg_ref, wu_ref, wd_ref, lead):
    hidden = []
    for c in range(0, D_FF, FF_CHUNK):
        cols = slice(c, c + FF_CHUNK)
        hg = jnp.dot(xb, wg_ref[lead + (slice(None), cols)], preferred_element_type=F32)
        hu = jnp.dot(xb, wu_ref[lead + (slice(None), cols)], preferred_element_type=F32)
        hidden.append((hg * _sigmoid(hg) * hu).astype(BF16))
    return jnp.dot(jnp.concatenate(hidden, axis=1), wd_ref[lead + (slice(None), slice(None))],
                   preferred_element_type=F32)


def _ffn_kernel(x_ref, wg_ref, wu_ref, wd_ref, g_ref, b_ref, o_ref):
    x = x_ref[...]
    f = _swiglu_rows(x.astype(BF16), wg_ref, wu_ref, wd_ref, ())
    o_ref[...] = _layer_norm(DN_ALPHA * x + f, g_ref[...], b_ref[...])


def _ffn(x2, wg, wu, wd, ln_g, ln_b, tm=256):
    n = x2.shape[0]
    row = lambda i: (i, 0)
    vec = pl.BlockSpec((1, D_MODEL), lambda i: (0, 0))
    return pl.pallas_call(
        _ffn_kernel,
        grid=(n // tm,),
        in_specs=[pl.BlockSpec((tm, D_MODEL), row), _vmem_whole(), _vmem_whole(), _vmem_whole(), vec, vec],
        out_specs=pl.BlockSpec((tm, D_MODEL), row),
        out_shape=jax.ShapeDtypeStruct((n, D_MODEL), F32),
        compiler_params=_cparams("parallel"),
        name="ffn_ln",
    )(x2, wg, wu, wd, ln_g, ln_b)


def _router_kernel(x_ref, w_ref, e_ref, g_ref):
    x_hi, x_lo = _split_bf16(x_ref[...])
    w_hi, w_lo = _split_bf16(w_ref[...])
    tm = x_hi.shape[0]
    hi_part = _dot(jnp.concatenate([x_hi, x_lo], axis=0), w_hi)
    logits = hi_part[:tm] + hi_part[tm:] + _dot(x_hi, w_lo)
    lane = lax.broadcasted_iota(jnp.int32, logits.shape, 1).astype(F32)
    m1 = jnp.max(logits, axis=1, keepdims=True)
    e1 = jnp.min(jnp.where(logits == m1, lane, float(N_EXPERTS)), axis=1, keepdims=True)
    rest = jnp.where(lane == e1, -jnp.inf, logits)
    m2 = jnp.max(rest, axis=1, keepdims=True)
    e2 = jnp.min(jnp.where(rest == m2, lane, float(N_EXPERTS)), axis=1, keepdims=True)
    w2 = jnp.exp(m2 - m1)
    denom = 1.0 + w2
    e_ref[...] = jnp.where(lane == 0.0, e1, e2).astype(jnp.int32)
    g_ref[...] = jnp.where(lane == 0.0, 1.0 / denom, w2 / denom)


def _router(x2, router_w, tm=512):
    n = x2.shape[0]
    row = lambda i: (i, 0)
    return pl.pallas_call(
        _router_kernel,
        grid=(n // tm,),
        in_specs=[pl.BlockSpec((tm, D_MODEL), row), pl.BlockSpec((D_MODEL, N_EXPERTS), lambda i: (0, 0))],
        out_specs=[pl.BlockSpec((tm, N_EXPERTS), row), pl.BlockSpec((tm, N_EXPERTS), row)],
        out_shape=[jax.ShapeDtypeStruct((n, N_EXPERTS), jnp.int32),
                   jax.ShapeDtypeStruct((n, N_EXPERTS), F32)],
        compiler_params=_cparams("parallel"),
        name="router",
    )(x2, router_w)


ROW_TILES = D_MODEL // 128


def _to_row_tiles(dst_ref, lead, x):
    for c in range(ROW_TILES):
        dst_ref[lead + (slice(None), c, slice(None))] = x[:, c * 128:(c + 1) * 128]


def _from_row_tiles(src_ref, dtype):
    return jnp.concatenate([src_ref[:, c, :].astype(dtype) for c in range(ROW_TILES)], axis=1)


def _scatter_step(step, n_steps, wait_step, start_this_step):
    @pl.when(step >= 2)
    def _():
        wait_step(step - 2)

    start_this_step()

    @pl.when(step == n_steps - 1)
    def _():
        @pl.when(step >= 1)
        def _():
            wait_step(step - 1)

        wait_step(step)


def _wait_rows(buf_rows, hbm_ref, n_rows, sem):
    pltpu.make_async_copy(buf_rows.at[pl.ds(0, n_rows)], hbm_ref.at[pl.ds(0, n_rows)], sem).wait()


def _dispatch_kernel(dest_ref, x_ref, init_ref, xb_ref, back_ref, buf, sems, *, tm, n_tokens, n_rows):
    del init_ref
    i = pl.program_id(0)

    @pl.when(i == 0)
    def _():
        def clear(r, carry):
            back_ref[r] = 0
            return carry
        lax.fori_loop(0, n_rows, clear, 0, unroll=8)

    def wait_step(step):
        for _ in range(EXPERT_TOPK):
            _wait_rows(buf.at[step % 2], xb_ref, tm, sems.at[step % 2])

    def start():
        slot = i % 2
        _to_row_tiles(buf, (slot,), x_ref[...])

        def one(r, carry):
            for k in range(EXPERT_TOPK):
                dst = dest_ref[(i * tm + r) * EXPERT_TOPK + k]
                back_ref[dst] = k * n_tokens + i * tm + r
                pltpu.make_async_copy(buf.at[slot, r], xb_ref.at[dst], sems.at[slot]).start()
            return carry
        lax.fori_loop(0, tm, one, 0, unroll=4)

    _scatter_step(i, pl.num_programs(0), wait_step, start)


def _dispatch(x2, dest, n_rows, tm=256):
    n = x2.shape[0]
    init = jnp.zeros((n_rows, ROW_TILES, 128), F32)
    return pl.pallas_call(
        functools.partial(_dispatch_kernel, tm=tm, n_tokens=n, n_rows=n_rows),
        grid_spec=pltpu.PrefetchScalarGridSpec(
            num_scalar_prefetch=1,
            grid=(n // tm,),
            in_specs=[pl.BlockSpec((tm, D_MODEL), lambda i, d: (i, 0)), pl.BlockSpec(memory_space=pl.ANY)],
            out_specs=[pl.BlockSpec(memory_space=pl.ANY), pl.BlockSpec(memory_space=pltpu.SMEM)],
            scratch_shapes=[pltpu.VMEM((2, tm, ROW_TILES, 128), F32), pltpu.SemaphoreType.DMA((2,))],
        ),
        out_shape=[jax.ShapeDtypeStruct((n_rows, ROW_TILES, 128), F32),
                   jax.ShapeDtypeStruct((n_rows,), jnp.int32)],
        input_output_aliases={2: 0},
        compiler_params=_cparams("arbitrary"),
        name="moe_dispatch",
    )(dest, x2, init)


def _moe_kernel(blk_e_ref, n_valid_ref, row_dst_ref, x_ref, wg_ref, wu_ref, wd_ref, y_ref, xs, ybuf, sems):
    del blk_e_ref
    i = pl.program_id(0)

    def wait_step(step):
        @pl.when(n_valid_ref[step] > 0)
        def _():
            _wait_rows(ybuf.at[step % 2], y_ref, n_valid_ref[step], sems.at[step % 2])

    def start():
        @pl.when(n_valid_ref[i] > 0)
        def _():
            slot = i % 2
            xs[...] = _from_row_tiles(x_ref, BF16)
            _to_row_tiles(ybuf, (slot,), _swiglu_rows(xs[...], wg_ref, wu_ref, wd_ref, (0,)))

            def one(r, carry):
                dst = row_dst_ref[i * MOE_ROWS + r]
                pltpu.make_async_copy(ybuf.at[slot, r], y_ref.at[dst], sems.at[slot]).start()
                return carry

            def four(r4, carry):
                for k in range(4):
                    one(4 * r4 + k, carry)
                return carry

            n_rows = n_valid_ref[i]
            lax.fori_loop(0, n_rows // 4, four, 0)
            lax.fori_loop(n_rows // 4 * 4, n_rows, one, 0)

    _scatter_step(i, pl.num_programs(0), wait_step, start)


def _moe_experts(xb, blk_e, n_valid, row_dst, wg, wu, wd, n_out_rows):
    n_blocks = xb.shape[0] // MOE_ROWS
    weight = lambda shape: pl.BlockSpec((1,) + shape, lambda i, be, nv, rd: (be[i], 0, 0))
    return pl.pallas_call(
        _moe_kernel,
        grid_spec=pltpu.PrefetchScalarGridSpec(
            num_scalar_prefetch=3,
            grid=(n_blocks,),
            in_specs=[pl.BlockSpec((MOE_ROWS, ROW_TILES, 128), lambda i, be, nv, rd: (i, 0, 0)),
                      weight((D_MODEL, D_FF)), weight((D_MODEL, D_FF)), weight((D_FF, D_MODEL))],
            out_specs=pl.BlockSpec(memory_space=pl.ANY),
            scratch_shapes=[pltpu.VMEM((MOE_ROWS, D_MODEL), BF16),
                            pltpu.VMEM((2, MOE_ROWS, ROW_TILES, 128), F32),
                            pltpu.SemaphoreType.DMA((2,))],
        ),
        out_shape=jax.ShapeDtypeStruct((n_out_rows, ROW_TILES, 128), F32),
        compiler_params=_cparams("arbitrary"),
        name="moe_experts",
    )(blk_e, n_valid, row_dst, xb, wg, wu, wd)


def _combine_kernel(x_ref, y0_ref, y1_ref, gate_ref, g_ref, b_ref, o_ref):
    gate = gate_ref[...]
    f = gate[:, 0:1] * _from_row_tiles(y0_ref, F32) + gate[:, 1:2] * _from_row_tiles(y1_ref, F32)
    o_ref[...] = _layer_norm(DN_ALPHA * x_ref[...] + f, g_ref[...], b_ref[...])


def _combine(x2, y_slots, gate, ln_g, ln_b, tm=512):
    n = x2.shape[0]
    row = lambda i: (i, 0)
    vec = pl.BlockSpec((1, D_MODEL), lambda i: (0, 0))
    return pl.pallas_call(
        _combine_kernel,
        grid=(n // tm,),
        in_specs=[pl.BlockSpec((tm, D_MODEL), row),
                  pl.BlockSpec((tm, ROW_TILES, 128), lambda i: (i, 0, 0)),
                  pl.BlockSpec((tm, ROW_TILES, 128), lambda i: (n // tm + i, 0, 0)),
                  pl.BlockSpec((tm, EXPERT_TOPK), row), vec, vec],
        out_specs=pl.BlockSpec((tm, D_MODEL), row),
        out_shape=jax.ShapeDtypeStruct((n, D_MODEL), F32),
        compiler_params=_cparams("parallel"),
        name="combine_ln",
    )(x2, y_slots, y_slots, gate, ln_g, ln_b)


def _moe_layer(x2, router_w, wg, wu, wd, ln_g, ln_b):
    n = x2.shape[0]
    nk = n * EXPERT_TOPK
    top_e, gate = _router(x2, router_w)
    top_e = top_e[:, :EXPERT_TOPK]
    gate = gate[:, :EXPERT_TOPK]
    flat_e = top_e.reshape(-1)
    onehot = (flat_e[:, None] == jnp.arange(N_EXPERTS, dtype=jnp.int32)[None, :]).astype(jnp.int32)
    counts = jnp.sum(onehot, axis=0)
    rank = jnp.sum((jnp.cumsum(onehot, axis=0) - onehot) * onehot, axis=1)
    padded = (counts + MOE_ROWS - 1) // MOE_ROWS * MOE_ROWS
    pad_end = jnp.cumsum(padded)
    pad_start = pad_end - padded
    dest = (pad_start[flat_e] + rank).astype(jnp.int32)
    n_blocks = -(-nk // MOE_ROWS) + N_EXPERTS
    n_rows = n_blocks * MOE_ROWS
    blk_start = jnp.arange(n_blocks, dtype=jnp.int32) * MOE_ROWS
    blk_e = jnp.sum((blk_start[:, None] >= pad_end[None, :]).astype(jnp.int32), axis=1)
    valid_end = jnp.concatenate([pad_start + counts, jnp.zeros((1,), jnp.int32)])
    n_valid = jnp.clip(valid_end[blk_e] - blk_start, 0, MOE_ROWS).astype(jnp.int32)
    blk_e = jnp.minimum(blk_e, N_EXPERTS - 1).astype(jnp.int32)
    xb, row_dst = _dispatch(x2, dest, n_rows)
    y_slots = _moe_experts(xb, blk_e, n_valid, row_dst, wg, wu, wd, nk)
    return _combine(x2, y_slots, gate, ln_g, ln_b)


def _rope_tables(t):
    half = HEAD_DIM // 2
    inv = ROPE_THETA ** (-jnp.arange(half, dtype=F32) / half)
    ang = jnp.arange(t).astype(F32)[:, None] * inv[None, :]
    cos = jnp.cos(ang)
    sin = jnp.sin(ang)
    cos_t = jnp.tile(jnp.concatenate([cos, cos], -1), (1, N_HEADS))
    sin_t = jnp.tile(jnp.concatenate([-sin, sin], -1), (1, N_HEADS))
    return cos_t, sin_t


def _pad_rows(w, start):
    out = jnp.zeros((LORA_DIM, BRANCH_DIM), F32)
    return lax.dynamic_update_slice(out, w, (start, 0)).astype(BF16)


def _mixer(x2, b, t, cos_t, sin_t, w_in, conv_w, shift_mu, decay_w0, decay_w2, aaa_a0, aaa_w2, gate_w2,
           k_k, k_a, r_k, gn_g, gn_b, w_branch, w_out, ln_g, ln_b):
    w_in = w_in.astype(BF16)
    o_conv, q3, k3, vt4, kmean, rkv3, lora3, gates = _in_proj(
        x2.reshape(b, t, D_MODEL), w_in[:, :MIX_COLS], w_in[:, MIX_COLS:MIX_COLS + LORA_DIM],
        w_in[:, MIX_COLS + LORA_DIM:], conv_w, cos_t, sin_t)
    o_att = _moba(q3, k3, vt4, kmean.reshape(b, t // MOBA_BLOCK, BRANCH_DIM)).reshape(b * t, BRANCH_DIM)

    row = lambda a: a.reshape(1, -1)
    p = {
        "mu_r": row(shift_mu[0:BRANCH_DIM]), "mu_k": row(shift_mu[BRANCH_DIM:2 * BRANCH_DIM]),
        "mu_v": row(shift_mu[2 * BRANCH_DIM:3 * BRANCH_DIM]), "mu_l": row(shift_mu[3 * BRANCH_DIM:]),
        "w0": row(decay_w0), "w2": _pad_rows(decay_w2, 0),
        "a0": row(aaa_a0), "a2": _pad_rows(aaa_w2, DECAY_LORA),
        "g2": _pad_rows(gate_w2, DECAY_LORA + AAA_LORA),
        "k_k": row(k_k), "k_a": row(k_a), "r_k": row(r_k), "gn_g": row(gn_g), "gn_b": row(gn_b),
    }
    o_rwkv = _rwkv(rkv3, lora3, p)

    return _merge(o_conv.reshape(b * t, BRANCH_DIM), o_att, o_rwkv.reshape(b * t, BRANCH_DIM),
                  gates.reshape(b * t, GATE_COLS), x2,
                  w_branch.astype(BF16), w_out.astype(BF16), row(ln_g), row(ln_b))


def kernel(x, w_in, conv_w, shift_mu, decay_w0, decay_w2, aaa_a0, aaa_w2, gate_w2, k_k, k_a, r_k, gn_g, gn_b,
           w_branch, w_out, ln1_g, ln1_b, ln2_g, ln2_b, ffn_w_gate, ffn_w_up, ffn_w_down, router_w,
           moe_w_gate, moe_w_up, moe_w_down):
    b, t, d = x.shape
    x2 = x.reshape(b * t, d)
    cos_t, sin_t = _rope_tables(t)
    row = lambda a: a.reshape(1, -1)
    for l in range(DEPTH):
        x2 = _mixer(x2, b, t, cos_t, sin_t, w_in[l], conv_w[l], shift_mu[l], decay_w0[l], decay_w2[l],
                    aaa_a0[l], aaa_w2[l], gate_w2[l], k_k[l], k_a[l], r_k[l], gn_g[l], gn_b[l],
                    w_branch[l], w_out[l], ln1_g[l], ln1_b[l])
        j = l // 2
        if l % 2 == 0:
            x2 = _ffn(x2, ffn_w_gate[j].astype(BF16), ffn_w_up[j].astype(BF16), ffn_w_down[j].astype(BF16),
                      row(ln2_g[l]), row(ln2_b[l]))
        else:
            x2 = _moe_layer(x2, router_w[j], moe_w_gate[j].astype(BF16), moe_w_up[j].astype(BF16),
                            moe_w_down[j].astype(BF16), row(ln2_g[l]), row(ln2_b[l]))
    return x2.reshape(b, t, d)
```

```python
import functools

import jax
import jax.numpy as jnp
from jax import lax
from jax.experimental import pallas as pl
from jax.experimental.pallas import tpu as pltpu

F32 = jnp.float32
BF16 = jnp.bfloat16

D_MODEL = 1024
HEAD_DIM = 64
BRANCH_DIM = 512
N_HEADS = BRANCH_DIM // HEAD_DIM
DECAY_LORA = 64
AAA_LORA = 64
GATE_LORA = 128
LORA_DIM = DECAY_LORA + AAA_LORA + GATE_LORA
MIX_COLS = 9 * BRANCH_DIM
GATE_COLS = 3 * D_MODEL
MOBA_BLOCK = 256
MOBA_TOPK = 3
ROPE_THETA = 10000.0
D_FF = 2816
N_EXPERTS = 8
EXPERT_TOPK = 2
MOE_ROWS = 256
LN_EPS = 1e-5
GN_EPS = 64e-5
DEPTH = 2
DN_ALPHA = (2 * DEPTH) ** 0.25
LOG2_E = 1.4426950408889634

VMEM_LIMIT_BYTES = 56 * 1024 * 1024
SUBLANES = 8
BF16_ROWS = 16
RWKV_TILE = 256
RWKV_CHUNK = 64
FF_CHUNK = 256
SCATTER_UNROLL = 8
PAIR_WIDTH = 2 * HEAD_DIM
HEAD_SHIFT = HEAD_DIM.bit_length() - 1
CHUNK_SHIFT = RWKV_CHUNK.bit_length() - 1


def _cparams(*sem):
    return pltpu.CompilerParams(dimension_semantics=sem, vmem_limit_bytes=VMEM_LIMIT_BYTES)


def _vmem_whole():
    return pl.BlockSpec(memory_space=pltpu.VMEM)


def _dot(a, b):
    return jnp.dot(a.astype(BF16), b.astype(BF16), preferred_element_type=F32)


def _dot_nt(a, b):
    return lax.dot_general(a.astype(BF16), b.astype(BF16), (((1,), (1,)), ((), ())),
                           preferred_element_type=F32)


def _dot_tn(a, b):
    return lax.dot_general(a.astype(BF16), b.astype(BF16), (((0,), (0,)), ((), ())),
                           preferred_element_type=F32)


def _split_bf16(x):
    hi = x.astype(BF16)
    return hi, (x - hi.astype(F32)).astype(BF16)


def _dot_split(a, b, *, data_on_left):
    hi, lo = _split_bf16(a if data_on_left else b)
    if data_on_left:
        return jnp.dot(hi, b, preferred_element_type=F32) + jnp.dot(lo, b, preferred_element_type=F32)
    return jnp.dot(a, hi, preferred_element_type=F32) + jnp.dot(a, lo, preferred_element_type=F32)


def _sigmoid(x):
    return 1.0 / (1.0 + jnp.exp(-x))


def _layer_norm(y, g, b):
    mu = jnp.mean(y, axis=-1, keepdims=True)
    d = y - mu
    var = jnp.mean(d * d, axis=-1, keepdims=True)
    return d * lax.rsqrt(var + LN_EPS) * g + b


def _shift_rows(u, halo, n):
    out = pltpu.roll(u, n, axis=0)
    row = lax.broadcasted_iota(jnp.int32, u.shape, 0)
    last = halo.shape[0]
    for r in range(n):
        src = halo[last - n + r:last - n + r + 1, :]
        out = jnp.where(row == r, src, out)
    return out


def _halo_index(tile_rows, halo_rows):
    step = tile_rows // halo_rows

    def index(col):
        return lambda b, i: (b, jnp.maximum(i * step - 1, 0), col)
    return index


def _in_proj_kernel(x_ref, wm_ref, convw_ref, cos_ref, sin_ref,
                    conv_ref, q_ref, k_ref, vt_ref, km_ref, rkv_ref, lora_ref, gate_ref, halo_ref):
    first = pl.program_id(1) == 0
    bd = BRANCH_DIM
    xb = x_ref[0].astype(BF16)
    proj = lambda g: jnp.dot(xb, wm_ref[:, g * bd:(g + 1) * bd], preferred_element_type=F32)

    u = proj(2) * proj(0)
    halo = jnp.where(first, 0.0, halo_ref[...])
    w = convw_ref[...]
    y = w[2:3, :] * u + w[1:2, :] * _shift_rows(u, halo, 1) + w[0:1, :] * _shift_rows(u, halo, 2)
    conv_ref[0] = (proj(1) * y).astype(conv_ref.dtype)
    halo_ref[...] = u[u.shape[0] - SUBLANES:, :]

    cos = cos_ref[...]
    sin = sin_ref[...]
    lane = lax.broadcasted_iota(jnp.int32, cos.shape, 1)
    first_half = (lane & (HEAD_DIM - 1)) < (HEAD_DIM // 2)

    def rope(t):
        swapped = jnp.where(first_half, pltpu.roll(t, bd - HEAD_DIM // 2, axis=1),
                            pltpu.roll(t, HEAD_DIM // 2, axis=1))
        return t * cos + swapped * sin

    kr = rope(proj(4))
    q_ref[0] = rope(proj(3))
    k_ref[0] = kr.astype(BF16)
    km_ref[0, 0] = jnp.mean(kr, axis=0, keepdims=True)
    vt_ref[0, 0] = proj(5).astype(BF16).T

    for g in range(3):
        rkv_ref[0, :, g * bd:(g + 1) * bd] = proj(6 + g).astype(rkv_ref.dtype)
    lora_ref[0] = jnp.dot(xb, wm_ref[:, MIX_COLS:MIX_COLS + LORA_DIM], preferred_element_type=F32)
    for j in range(0, GATE_COLS, bd):
        col = MIX_COLS + LORA_DIM + j
        g = jnp.dot(xb, wm_ref[:, col:col + bd], preferred_element_type=F32)
        gate_ref[0, :, j:j + bd] = _sigmoid(g).astype(gate_ref.dtype)


def _in_proj(x3, w_in, conv_w, cos_t, sin_t):
    b, t, _ = x3.shape
    tm = MOBA_BLOCK
    nb = t // tm
    blk = lambda width: pl.BlockSpec((1, tm, width), lambda bi, i: (bi, i, 0))
    tab = pl.BlockSpec((tm, BRANCH_DIM), lambda bi, i: (i, 0))
    shape3 = lambda width, dtype: jax.ShapeDtypeStruct((b, t, width), dtype)
    return pl.pallas_call(
        _in_proj_kernel,
        grid=(b, nb),
        in_specs=[blk(D_MODEL), _vmem_whole(), pl.BlockSpec((3, BRANCH_DIM), lambda bi, i: (0, 0)), tab, tab],
        out_specs=[blk(BRANCH_DIM), blk(BRANCH_DIM), blk(BRANCH_DIM),
                   pl.BlockSpec((1, 1, BRANCH_DIM, tm), lambda bi, i: (bi, i, 0, 0)),
                   pl.BlockSpec((1, 1, 1, BRANCH_DIM), lambda bi, i: (bi, i, 0, 0)),
                   blk(3 * BRANCH_DIM), blk(LORA_DIM), blk(GATE_COLS)],
        out_shape=[shape3(BRANCH_DIM, BF16), shape3(BRANCH_DIM, F32), shape3(BRANCH_DIM, BF16),
                   jax.ShapeDtypeStruct((b, nb, BRANCH_DIM, tm), BF16),
                   jax.ShapeDtypeStruct((b, nb, 1, BRANCH_DIM), F32),
                   shape3(3 * BRANCH_DIM, BF16), shape3(LORA_DIM, F32), shape3(GATE_COLS, BF16)],
        scratch_shapes=[pltpu.VMEM((SUBLANES, BRANCH_DIM), F32)],
        compiler_params=_cparams("parallel", "arbitrary"),
        name="in_proj",
    )(x3, w_in, conv_w, cos_t, sin_t)


def _moba_kernel(q_ref, k_ref, vt_ref, km_ref, o_ref, *, nb, heads):
    i = pl.program_id(2)
    bs = MOBA_BLOCK
    width = heads * HEAD_DIM
    q = q_ref[0]
    km = km_ref[0]
    head_of_lane = lax.broadcasted_iota(jnp.int32, (1, width), 1) >> HEAD_SHIFT
    q_heads = jnp.concatenate([jnp.where(head_of_lane == g, q, 0.0) for g in range(heads)], axis=0)
    q_all = (q_heads * (HEAD_DIM ** -0.5 * LOG2_E)).astype(BF16)

    blk = lax.broadcasted_iota(jnp.int32, (nb, heads * bs), 0)
    valid = blk < i
    km_heads = jnp.concatenate([jnp.where(head_of_lane == g, km, 0.0) for g in range(heads)], axis=0)
    km_hi, km_lo = _split_bf16(km_heads)
    q_hi, q_lo = _split_bf16(q)
    hi_part = _dot_nt(jnp.concatenate([km_hi, km_lo], axis=0), q_hi)
    gate = hi_part[:heads * nb] + hi_part[heads * nb:] + _dot_nt(km_hi, q_lo)
    gate = jnp.concatenate([gate[g * nb:(g + 1) * nb] for g in range(heads)], axis=1)
    gate = jnp.where(valid, gate, -jnp.inf)
    rank = jnp.zeros((nb, heads * bs), jnp.int32)
    for m in range(nb):
        gm = gate[m:m + 1, :]
        beats = (gm > gate) | ((gm == gate) & (blk > m))
        rank = rank + beats.astype(jnp.int32)
    chosen = jnp.where(valid & (rank < MOBA_TOPK), (1 << blk).astype(F32), 0.0)
    picks = jnp.sum(chosen, axis=0, keepdims=True).astype(jnp.int32)

    ones_rows = jnp.ones((SUBLANES, bs), BF16)

    def scores(j, n_blocks):
        kb = k_ref[0, pl.ds(pl.multiple_of(j * bs, bs), n_blocks * bs), :]
        return lax.dot_general(kb, q_all, (((1,), (1,)), ((), ())), preferred_element_type=F32)

    def block_update(j, s, chosen, causal, carry):
        n_blocks = len(chosen)
        m, accs = carry
        if causal is not None:
            s = jnp.where(causal, s, -jnp.inf)
        parts = [s[n * bs:(n + 1) * bs] for n in range(n_blocks)]
        tops = [jnp.max(part, axis=0, keepdims=True) for part in parts]
        tops = [t if c is None else jnp.where(c, t, -jnp.inf) for t, c in zip(tops, chosen)]
        m_new = functools.reduce(jnp.maximum, tops, m)
        alpha = jnp.exp2(m - m_new)
        shifts = [m_new if c is None else jnp.where(c, m_new, jnp.inf) for c in chosen]
        ps = [jnp.exp2(part - shift).astype(BF16) for part, shift in zip(parts, shifts)]
        new_acc = []
        for g in range(heads):
            cols = slice(g * bs, (g + 1) * bs)
            acc = alpha[:, cols] * accs[g]
            for n in range(n_blocks):
                vt = jnp.concatenate([vt_ref[0, j + n, g * HEAD_DIM:(g + 1) * HEAD_DIM, :], ones_rows], axis=0)
                acc = acc + jnp.dot(vt, ps[n][:, cols], preferred_element_type=F32)
            new_acc.append(acc)
        return m_new, tuple(new_acc)

    key_id = lax.broadcasted_iota(jnp.int32, (bs, heads * bs), 0)
    query_id = lax.broadcasted_iota(jnp.int32, (bs, heads * bs), 1) & (bs - 1)
    init = (jnp.full((1, heads * bs), -jnp.inf, F32),
            tuple(jnp.zeros((HEAD_DIM + SUBLANES, bs), F32) for _ in range(heads)))
    carry = block_update(i, scores(i, 1), [None], key_id <= query_id, init)

    def pair_update(j, s, carry):
        bits = picks >> j
        return block_update(j, s, [(bits & 1) == 1, ((bits >> 1) & 1) == 1], None, carry)

    def two_pairs(jq, carry):
        j = 4 * jq
        s_a, s_b = scores(j, 2), scores(j + 2, 2)
        return pair_update(j + 2, s_b, pair_update(j, s_a, carry))

    carry = lax.fori_loop(0, i // 4, two_pairs, carry)
    done = (i // 4) * 4
    _, accs = lax.fori_loop(0, (i - done + 1) // 2,
                            lambda jj, c: pair_update(done + 2 * jj, scores(done + 2 * jj, 2), c), carry)
    out_t = jnp.concatenate([acc[:HEAD_DIM] / acc[HEAD_DIM:HEAD_DIM + 1] for acc in accs], axis=0)
    o_ref[0] = out_t.T.astype(o_ref.dtype)


def _moba(q3, k3, vt4, kmean, heads=4):
    b, t, _ = q3.shape
    nb = t // MOBA_BLOCK
    width = heads * HEAD_DIM
    qblk = pl.BlockSpec((1, MOBA_BLOCK, width), lambda bi, hi, i: (bi, i, hi))
    return pl.pallas_call(
        functools.partial(_moba_kernel, nb=nb, heads=heads),
        grid=(b, N_HEADS // heads, nb),
        in_specs=[qblk, pl.BlockSpec((1, t, width), lambda bi, hi, i: (bi, 0, hi)),
                  pl.BlockSpec((1, nb, width, MOBA_BLOCK), lambda bi, hi, i: (bi, 0, hi, 0)),
                  pl.BlockSpec((1, nb, width), lambda bi, hi, i: (bi, 0, hi))],
        out_specs=qblk,
        out_shape=jax.ShapeDtypeStruct((b, t, BRANCH_DIM), BF16),
        compiler_params=_cparams("parallel", "parallel", "arbitrary"),
        name="moba",
    )(q3, k3, vt4, kmean)


def _rwkv_kernel(r_ref, k_ref, v_ref, lo_ref, hr_ref, hk_ref, hv_ref, hl_ref,
                 mu_r_ref, mu_k_ref, mu_v_ref, mu_l_ref, w0_ref, w2_ref, a0_ref, a2_ref, g2_ref,
                 kk_ref, ka_ref, rk_ref, gng_ref, gnb_ref, o_ref, state_ref):
    i = pl.program_id(1)
    first = i == 0
    tq, ch = RWKV_TILE, RWKV_CHUNK
    hd, bd = HEAD_DIM, BRANCH_DIM

    @pl.when(first)
    def _():
        state_ref[...] = jnp.zeros_like(state_ref)

    def lerp(x_ref, h_ref, mu_ref):
        x = x_ref[0].astype(F32)
        halo = jnp.where(first, 0.0, h_ref[0].astype(F32))
        return x + (_shift_rows(x, halo, 1) - x) * mu_ref[...]

    r = lerp(r_ref, hr_ref, mu_r_ref)
    k = lerp(k_ref, hk_ref, mu_k_ref)
    v = lerp(v_ref, hv_ref, mu_v_ref)
    lo = lerp(lo_ref, hl_ref, mu_l_ref)

    z = w0_ref[...] + _dot(jnp.tanh(lo), w2_ref[...])
    ew = jnp.exp(-0.5) * _sigmoid(z)
    a = _sigmoid(a0_ref[...] + _dot(lo, a2_ref[...]))
    g = _dot(_sigmoid(lo), g2_ref[...])

    half = bd // 2
    lane_r = lax.broadcasted_iota(jnp.int32, (half, half), 0)
    lane_c = lax.broadcasted_iota(jnp.int32, (half, half), 1)
    ones_blocks = jnp.where((lane_r >> HEAD_SHIFT) == (lane_c >> HEAD_SHIFT), 1.0, 0.0).astype(BF16)

    def head_sum(x):
        return jnp.concatenate([_dot_split(x[:, :half], ones_blocks, data_on_left=True),
                                _dot_split(x[:, half:], ones_blocks, data_on_left=True)], axis=1)

    kk = k * kk_ref[...]
    kk = kk / jnp.maximum(jnp.sqrt(head_sum(kk * kk)), 1e-12)
    k2 = k * (1.0 + (a - 1.0) * ka_ref[...])
    bonus = head_sum(r * k2 * rk_ref[...]) * v

    t_r = lax.broadcasted_iota(jnp.int32, (tq, tq), 0)
    t_c = lax.broadcasted_iota(jnp.int32, (tq, tq), 1)
    same_chunk = (t_r >> CHUNK_SHIFT) == (t_c >> CHUNK_SHIFT)
    cs = _dot_split(jnp.where(same_chunk & (t_c <= t_r), 1.0, 0.0).astype(BF16), ew, data_on_left=False)
    cs_end = jnp.concatenate(
        [jnp.broadcast_to(cs[c * ch + ch - 1:c * ch + ch, :], (ch, bd)) for c in range(tq // ch)], axis=0)

    kka = kk * a
    a_t = -kk * jnp.exp(ew - cs)
    b_hat = (kka * jnp.exp(cs)).astype(BF16)
    k_hat = (k2 * jnp.exp(cs)).astype(BF16)
    r_t = r * jnp.exp(-cs)
    b_e = (kka * jnp.exp(cs - cs_end)).astype(BF16)
    k_e = (k2 * jnp.exp(cs - cs_end)).astype(BF16)
    p_c = jnp.exp(-cs_end)
    v_b = v.astype(BF16)

    strict = jnp.where(same_chunk & (t_c < t_r), 1.0, 0.0)
    incl = jnp.where(same_chunk & (t_c <= t_r), 1.0, 0.0)
    eye = jnp.where(t_c == t_r, 1.0, 0.0)
    pw = PAIR_WIDTH
    second = (lax.broadcasted_iota(jnp.int32, (1, pw), 1) >> HEAD_SHIFT) == 1
    same_head = ((lax.broadcasted_iota(jnp.int32, (pw, pw), 0) >> HEAD_SHIFT)
                 == (lax.broadcasted_iota(jnp.int32, (pw, pw), 1) >> HEAD_SHIFT))

    n_pairs = N_HEADS // 2
    pair = lambda x, p: x[:, p * pw:(p + 1) * pw]
    head_list = [(p, hm) for p in range(n_pairs) for hm in (~second, second)]
    a_hs = [jnp.where(hm, pair(a_t, p), 0.0).astype(BF16) for p, hm in head_list]
    l_abs = [strict * _dot_nt(a_h, pair(b_hat, p)) for a_h, (p, _) in zip(a_hs, head_list)]
    t_invs = [eye + x for x in l_abs]
    powers = l_abs
    n = 2
    while n < ch:
        powers = [_dot(x, x) for x in powers]
        t_invs = [t + _dot(t, x) for t, x in zip(t_invs, powers)]
        n *= 2
    l_aks = [strict * _dot_nt(a_h, pair(k_hat, p)) for a_h, (p, _) in zip(a_hs, head_list)]
    lakvs = [_dot(l_ak, pair(v_b, p)) for l_ak, (p, _) in zip(l_aks, head_list)]
    sols = [_dot(t, jnp.concatenate([pair(a_t, p).astype(BF16), lakv.astype(BF16)], axis=1))
            for t, lakv, (p, _) in zip(t_invs, lakvs, head_list)]
    r_hs = [jnp.where(hm, pair(r_t, p), 0.0).astype(BF16) for p, hm in head_list]
    m_rbs = [incl * _dot_nt(r_h, pair(b_hat, p)) for r_h, (p, _) in zip(r_hs, head_list)]
    m_rks = [incl * _dot_nt(r_h, pair(k_hat, p)) for r_h, (p, _) in zip(r_hs, head_list)]
    apps = [_dot(m_rb, sol) for m_rb, sol in zip(m_rbs, sols)]
    mrkvs = [_dot(m_rk, pair(v_b, p)) for m_rk, (p, _) in zip(m_rks, head_list)]

    def both(xs, p, cols):
        return jnp.where(second, xs[2 * p + 1][:, cols], xs[2 * p][:, cols])

    left, right = slice(0, pw), slice(pw, 2 * pw)
    a_til = [both(sols, p, left).astype(BF16) for p in range(n_pairs)]
    u_0 = [both(sols, p, right).astype(BF16) for p in range(n_pairs)]
    r_til = [(both(apps, p, left) + pair(r_t, p)).astype(BF16) for p in range(n_pairs)]
    y_hat = [both(apps, p, right) + both(mrkvs, p, left) for p in range(n_pairs)]

    states = [state_ref[p] for p in range(n_pairs)]
    y_rows = []
    for c in range(tq // ch):
        rows = slice(c * ch, (c + 1) * ch)
        g_cs = [jnp.where(same_head, _dot_tn(a_til[p][rows], pair(b_e, p)[rows]), 0.0) for p in range(n_pairs)]
        h_cs = [jnp.where(same_head, _dot_tn(u_0[p][rows], pair(b_e, p)[rows])
                          + _dot_tn(pair(v_b, p)[rows], pair(k_e, p)[rows]), 0.0) for p in range(n_pairs)]
        y_rows.append(jnp.concatenate(
            [_dot_nt(r_til[p][rows], states[p]) + y_hat[p][rows] for p in range(n_pairs)], axis=1))
        states = [states[p] * pair(p_c, p)[c * ch:c * ch + 1] + _dot(states[p], g_cs[p]) + h_cs[p]
                  for p in range(n_pairs)]
    for p in range(n_pairs):
        state_ref[p] = states[p]
    y = jnp.concatenate(y_rows, axis=0)

    mu = head_sum(y) * (1.0 / hd)
    d = y - mu
    var = head_sum(d * d) * (1.0 / hd)
    y = d * lax.rsqrt(var + GN_EPS) * gng_ref[...] + gnb_ref[...]
    o_ref[0] = ((y + bonus) * g).astype(o_ref.dtype)


def _rwkv(rkv3, lora3, p):
    b, t, _ = rkv3.shape
    tq = RWKV_TILE
    halo = _halo_index(tq, SUBLANES)
    halo_bf16 = _halo_index(tq, BF16_ROWS)
    blk = lambda col: pl.BlockSpec((1, tq, BRANCH_DIM), lambda bi, i: (bi, i, col))
    hblk = lambda col: pl.BlockSpec((1, BF16_ROWS, BRANCH_DIM), halo_bf16(col))
    vec = lambda width: pl.BlockSpec((1, width), lambda bi, i: (0, 0))
    mat = lambda rows: pl.BlockSpec((rows, BRANCH_DIM), lambda bi, i: (0, 0))
    return pl.pallas_call(
        _rwkv_kernel,
        grid=(b, t // tq),
        in_specs=[blk(0), blk(1), blk(2), pl.BlockSpec((1, tq, LORA_DIM), lambda bi, i: (bi, i, 0)),
                  hblk(0), hblk(1), hblk(2), pl.BlockSpec((1, SUBLANES, LORA_DIM), halo(0)),
                  vec(BRANCH_DIM), vec(BRANCH_DIM), vec(BRANCH_DIM), vec(LORA_DIM),
                  vec(BRANCH_DIM), mat(LORA_DIM), vec(BRANCH_DIM), mat(LORA_DIM), mat(LORA_DIM),
                  vec(BRANCH_DIM), vec(BRANCH_DIM), vec(BRANCH_DIM), vec(BRANCH_DIM), vec(BRANCH_DIM)],
        out_specs=pl.BlockSpec((1, tq, BRANCH_DIM), lambda bi, i: (bi, i, 0)),
        out_shape=jax.ShapeDtypeStruct((b, t, BRANCH_DIM), BF16),
        scratch_shapes=[pltpu.VMEM((N_HEADS // 2, PAIR_WIDTH, PAIR_WIDTH), F32)],
        compiler_params=_cparams("parallel", "arbitrary"),
        name="rwkv7",
    )(rkv3, rkv3, rkv3, lora3, rkv3, rkv3, rkv3, lora3,
      p["mu_r"], p["mu_k"], p["mu_v"], p["mu_l"], p["w0"], p["w2"], p["a0"], p["a2"], p["g2"],
      p["k_k"], p["k_a"], p["r_k"], p["gn_g"], p["gn_b"])


def _merge_kernel(oc_ref, oa_ref, or_ref, gate_ref, x_ref, wb_ref, wo_ref, g_ref, b_ref, o_ref):
    d = D_MODEL
    bd = BRANCH_DIM
    merged = (gate_ref[:, 0:d] * jnp.dot(oc_ref[...], wb_ref[0:bd, :], preferred_element_type=F32)
              + gate_ref[:, d:2 * d] * jnp.dot(oa_ref[...], wb_ref[bd:2 * bd, :], preferred_element_type=F32)
              + gate_ref[:, 2 * d:3 * d] * jnp.dot(or_ref[...], wb_ref[2 * bd:3 * bd, :],
                                                   preferred_element_type=F32))
    h = jnp.dot(merged.astype(BF16), wo_ref[...], preferred_element_type=F32)
    o_ref[...] = _layer_norm(DN_ALPHA * x_ref[...] + h, g_ref[...], b_ref[...])


def _merge(o_conv, o_att, o_rwkv, gates, x2, w_branch, w_out, ln_g, ln_b, tm=512):
    n = x2.shape[0]
    row = lambda i: (i, 0)
    vec = pl.BlockSpec((1, D_MODEL), lambda i: (0, 0))
    return pl.pallas_call(
        _merge_kernel,
        grid=(n // tm,),
        in_specs=[pl.BlockSpec((tm, BRANCH_DIM), row)] * 3
                 + [pl.BlockSpec((tm, GATE_COLS), row), pl.BlockSpec((tm, D_MODEL), row),
                    _vmem_whole(), _vmem_whole(), vec, vec],
        out_specs=pl.BlockSpec((tm, D_MODEL), row),
        out_shape=jax.ShapeDtypeStruct((n, D_MODEL), F32),
        compiler_params=_cparams("parallel"),
        name="merge_ln",
    )(o_conv, o_att, o_rwkv, gates, x2, w_branch, w_out, ln_g, ln_b)


def _swiglu_rows(xb, wg_ref, wu_ref, wd_ref, lead):
    hidden = []
    for c in range(0, D_FF, FF_CHUNK):
        cols = slice(c, c + FF_CHUNK)
        hg = jnp.dot(xb, wg_ref[lead + (slice(None), cols)], preferred_element_type=F32)
        hu = jnp.dot(xb, wu_ref[lead + (slice(None), cols)], preferred_element_type=F32)
        hidden.append((hg * _sigmoid(hg) * hu).astype(BF16))
    return jnp.dot(jnp.concatenate(hidden, axis=1), wd_ref[lead + (slice(None), slice(None))],
                   preferred_element_type=F32)


def _ffn_kernel(x_ref, wg_ref, wu_ref, wd_ref, g_ref, b_ref, o_ref):
    x = x_ref[...]
    f = _swiglu_rows(x.astype(BF16), wg_ref, wu_ref, wd_ref, ())
    o_ref[...] = _layer_norm(DN_ALPHA * x + f, g_ref[...], b_ref[...])


def _ffn(x2, wg, wu, wd, ln_g, ln_b, tm=256):
    n = x2.shape[0]
    row = lambda i: (i, 0)
    vec = pl.BlockSpec((1, D_MODEL), lambda i: (0, 0))
    return pl.pallas_call(
        _ffn_kernel,
        grid=(n // tm,),
        in_specs=[pl.BlockSpec((tm, D_MODEL), row), _vmem_whole(), _vmem_whole(), _vmem_whole(), vec, vec],
        out_specs=pl.BlockSpec((tm, D_MODEL), row),
        out_shape=jax.ShapeDtypeStruct((n, D_MODEL), F32),
        compiler_params=_cparams("parallel"),
        name="ffn_ln",
    )(x2, wg, wu, wd, ln_g, ln_b)


def _router_kernel(x_ref, w_ref, e_ref, g_ref):
    x_hi, x_lo = _split_bf16(x_ref[...])
    w_hi, w_lo = _split_bf16(w_ref[...])
    tm = x_hi.shape[0]
    hi_part = _dot(jnp.concatenate([x_hi, x_lo], axis=0), w_hi)
    logits = hi_part[:tm] + hi_part[tm:] + _dot(x_hi, w_lo)
    lane = lax.broadcasted_iota(jnp.int32, logits.shape, 1).astype(F32)
    m1 = jnp.max(logits, axis=1, keepdims=True)
    e1 = jnp.min(jnp.where(logits == m1, lane, float(N_EXPERTS)), axis=1, keepdims=True)
    rest = jnp.where(lane == e1, -jnp.inf, logits)
    m2 = jnp.max(rest, axis=1, keepdims=True)
    e2 = jnp.min(jnp.where(rest == m2, lane, float(N_EXPERTS)), axis=1, keepdims=True)
    w2 = jnp.exp(m2 - m1)
    denom = 1.0 + w2
    e_ref[...] = jnp.where(lane == 0.0, e1, e2).astype(jnp.int32)
    g_ref[...] = jnp.where(lane == 0.0, 1.0 / denom, w2 / denom)


def _router(x2, router_w, tm=512):
    n = x2.shape[0]
    row = lambda i: (i, 0)
    return pl.pallas_call(
        _router_kernel,
        grid=(n // tm,),
        in_specs=[pl.BlockSpec((tm, D_MODEL), row), pl.BlockSpec((D_MODEL, N_EXPERTS), lambda i: (0, 0))],
        out_specs=[pl.BlockSpec((tm, N_EXPERTS), row), pl.BlockSpec((tm, N_EXPERTS), row)],
        out_shape=[jax.ShapeDtypeStruct((n, N_EXPERTS), jnp.int32),
                   jax.ShapeDtypeStruct((n, N_EXPERTS), F32)],
        compiler_params=_cparams("parallel"),
        name="router",
    )(x2, router_w)


ROW_TILES = D_MODEL // 128


def _to_row_tiles(dst_ref, lead, x):
    for c in range(ROW_TILES):
        dst_ref[lead + (slice(None), c, slice(None))] = x[:, c * 128:(c + 1) * 128]


def _from_row_tiles(src_ref, dtype):
    return jnp.concatenate([src_ref[:, c, :].astype(dtype) for c in range(ROW_TILES)], axis=1)


def _scatter_step(step, n_steps, wait_step, start_this_step):
    @pl.when(step >= 2)
    def _():
        wait_step(step - 2)

    start_this_step()

    @pl.when(step == n_steps - 1)
    def _():
        @pl.when(step >= 1)
        def _():
            wait_step(step - 1)

        wait_step(step)


def _wait_rows(buf_rows, hbm_ref, n_rows, sem):
    pltpu.make_async_copy(buf_rows.at[pl.ds(0, n_rows)], hbm_ref.at[pl.ds(0, n_rows)], sem).wait()


def _dispatch_kernel(dest_ref, x_ref, init_ref, xb_ref, back_ref, buf, sems, *, tm, n_tokens, n_rows):
    del init_ref
    i = pl.program_id(0)

    @pl.when(i == 0)
    def _():
        def clear(r, carry):
            back_ref[r] = 0
            return carry
        lax.fori_loop(0, n_rows, clear, 0, unroll=8)

    def wait_step(step):
        for _ in range(EXPERT_TOPK):
            _wait_rows(buf.at[step % 2], xb_ref, tm, sems.at[step % 2])

    def start():
        slot = i % 2
        _to_row_tiles(buf, (slot,), x_ref[...])

        def one(r, carry):
            for k in range(EXPERT_TOPK):
                dst = dest_ref[(i * tm + r) * EXPERT_TOPK + k]
                back_ref[dst] = k * n_tokens + i * tm + r
                pltpu.make_async_copy(buf.at[slot, r], xb_ref.at[dst], sems.at[slot]).start()
            return carry
        lax.fori_loop(0, tm, one, 0, unroll=SCATTER_UNROLL)

    _scatter_step(i, pl.num_programs(0), wait_step, start)


def _dispatch(x2, dest, n_rows, tm=256):
    n = x2.shape[0]
    init = jnp.zeros((n_rows, ROW_TILES, 128), F32)
    return pl.pallas_call(
        functools.partial(_dispatch_kernel, tm=tm, n_tokens=n, n_rows=n_rows),
        grid_spec=pltpu.PrefetchScalarGridSpec(
            num_scalar_prefetch=1,
            grid=(n // tm,),
            in_specs=[pl.BlockSpec((tm, D_MODEL), lambda i, d: (i, 0)), pl.BlockSpec(memory_space=pl.ANY)],
            out_specs=[pl.BlockSpec(memory_space=pl.ANY), pl.BlockSpec(memory_space=pltpu.SMEM)],
            scratch_shapes=[pltpu.VMEM((2, tm, ROW_TILES, 128), F32), pltpu.SemaphoreType.DMA((2,))],
        ),
        out_shape=[jax.ShapeDtypeStruct((n_rows, ROW_TILES, 128), F32),
                   jax.ShapeDtypeStruct((n_rows,), jnp.int32)],
        input_output_aliases={2: 0},
        compiler_params=_cparams("arbitrary"),
        name="moe_dispatch",
    )(dest, x2, init)


def _moe_kernel(blk_e_ref, n_valid_ref, row_dst_ref, x_ref, wg_ref, wu_ref, wd_ref, y_ref, xs, ybuf, sems):
    del blk_e_ref
    i = pl.program_id(0)

    def wait_step(step):
        @pl.when(n_valid_ref[step] > 0)
        def _():
            _wait_rows(ybuf.at[step % 2], y_ref, n_valid_ref[step], sems.at[step % 2])

    def start():
        @pl.when(n_valid_ref[i] > 0)
        def _():
            slot = i % 2
            xs[...] = _from_row_tiles(x_ref, BF16)
            _to_row_tiles(ybuf, (slot,), _swiglu_rows(xs[...], wg_ref, wu_ref, wd_ref, (0,)))

            def one(r, carry):
                dst = row_dst_ref[i * MOE_ROWS + r]
                pltpu.make_async_copy(ybuf.at[slot, r], y_ref.at[dst], sems.at[slot]).start()
                return carry

            def group(rg, carry):
                for k in range(SCATTER_UNROLL):
                    one(SCATTER_UNROLL * rg + k, carry)
                return carry

            n_rows = n_valid_ref[i]
            n_groups = n_rows // SCATTER_UNROLL
            lax.fori_loop(0, n_groups, group, 0)
            lax.fori_loop(n_groups * SCATTER_UNROLL, n_rows, one, 0)

    _scatter_step(i, pl.num_programs(0), wait_step, start)


def _moe_experts(xb, blk_e, n_valid, row_dst, wg, wu, wd, n_out_rows):
    n_blocks = xb.shape[0] // MOE_ROWS
    weight = lambda shape: pl.BlockSpec((1,) + shape, lambda i, be, nv, rd: (be[i], 0, 0))
    return pl.pallas_call(
        _moe_kernel,
        grid_spec=pltpu.PrefetchScalarGridSpec(
            num_scalar_prefetch=3,
            grid=(n_blocks,),
            in_specs=[pl.BlockSpec((MOE_ROWS, ROW_TILES, 128), lambda i, be, nv, rd: (i, 0, 0)),
                      weight((D_MODEL, D_FF)), weight((D_MODEL, D_FF)), weight((D_FF, D_MODEL))],
            out_specs=pl.BlockSpec(memory_space=pl.ANY),
            scratch_shapes=[pltpu.VMEM((MOE_ROWS, D_MODEL), BF16),
                            pltpu.VMEM((2, MOE_ROWS, ROW_TILES, 128), F32),
                            pltpu.SemaphoreType.DMA((2,))],
        ),
        out_shape=jax.ShapeDtypeStruct((n_out_rows, ROW_TILES, 128), F32),
        compiler_params=_cparams("arbitrary"),
        name="moe_experts",
    )(blk_e, n_valid, row_dst, xb, wg, wu, wd)


def _combine_kernel(x_ref, y0_ref, y1_ref, gate_ref, g_ref, b_ref, o_ref):
    gate = gate_ref[...]
    f = gate[:, 0:1] * _from_row_tiles(y0_ref, F32) + gate[:, 1:2] * _from_row_tiles(y1_ref, F32)
    o_ref[...] = _layer_norm(DN_ALPHA * x_ref[...] + f, g_ref[...], b_ref[...])


def _combine(x2, y_slots, gate, ln_g, ln_b, tm=512):
    n = x2.shape[0]
    row = lambda i: (i, 0)
    vec = pl.BlockSpec((1, D_MODEL), lambda i: (0, 0))
    return pl.pallas_call(
        _combine_kernel,
        grid=(n // tm,),
        in_specs=[pl.BlockSpec((tm, D_MODEL), row),
                  pl.BlockSpec((tm, ROW_TILES, 128), lambda i: (i, 0, 0)),
                  pl.BlockSpec((tm, ROW_TILES, 128), lambda i: (n // tm + i, 0, 0)),
                  pl.BlockSpec((tm, EXPERT_TOPK), row), vec, vec],
        out_specs=pl.BlockSpec((tm, D_MODEL), row),
        out_shape=jax.ShapeDtypeStruct((n, D_MODEL), F32),
        compiler_params=_cparams("parallel"),
        name="combine_ln",
    )(x2, y_slots, y_slots, gate, ln_g, ln_b)


def _moe_layer(x2, router_w, wg, wu, wd, ln_g, ln_b):
    n = x2.shape[0]
    nk = n * EXPERT_TOPK
    top_e, gate = _router(x2, router_w)
    top_e = top_e[:, :EXPERT_TOPK]
    gate = gate[:, :EXPERT_TOPK]
    flat_e = top_e.reshape(-1)
    onehot = (flat_e[:, None] == jnp.arange(N_EXPERTS, dtype=jnp.int32)[None, :]).astype(jnp.int32)
    counts = jnp.sum(onehot, axis=0)
    rank = jnp.sum((jnp.cumsum(onehot, axis=0) - onehot) * onehot, axis=1)
    padded = (counts + MOE_ROWS - 1) // MOE_ROWS * MOE_ROWS
    pad_end = jnp.cumsum(padded)
    pad_start = pad_end - padded
    dest = (pad_start[flat_e] + rank).astype(jnp.int32)
    n_blocks = -(-nk // MOE_ROWS) + N_EXPERTS
    n_rows = n_blocks * MOE_ROWS
    blk_start = jnp.arange(n_blocks, dtype=jnp.int32) * MOE_ROWS
    blk_e = jnp.sum((blk_start[:, None] >= pad_end[None, :]).astype(jnp.int32), axis=1)
    valid_end = jnp.concatenate([pad_start + counts, jnp.zeros((1,), jnp.int32)])
    n_valid = jnp.clip(valid_end[blk_e] - blk_start, 0, MOE_ROWS).astype(jnp.int32)
    blk_e = jnp.minimum(blk_e, N_EXPERTS - 1).astype(jnp.int32)
    xb, row_dst = _dispatch(x2, dest, n_rows)
    y_slots = _moe_experts(xb, blk_e, n_valid, row_dst, wg, wu, wd, nk)
    return _combine(x2, y_slots, gate, ln_g, ln_b)


def _rope_tables(t):
    half = HEAD_DIM // 2
    inv = ROPE_THETA ** (-jnp.arange(half, dtype=F32) / half)
    ang = jnp.arange(t).astype(F32)[:, None] * inv[None, :]
    cos = jnp.cos(ang)
    sin = jnp.sin(ang)
    cos_t = jnp.tile(jnp.concatenate([cos, cos], -1), (1, N_HEADS))
    sin_t = jnp.tile(jnp.concatenate([-sin, sin], -1), (1, N_HEADS))
    return cos_t, sin_t


def _pad_rows(w, start):
    out = jnp.zeros((LORA_DIM, BRANCH_DIM), F32)
    return lax.dynamic_update_slice(out, w, (start, 0)).astype(BF16)


def _mixer(x2, b, t, cos_t, sin_t, w_in, conv_w, shift_mu, decay_w0, decay_w2, aaa_a0, aaa_w2, gate_w2,
           k_k, k_a, r_k, gn_g, gn_b, w_branch, w_out, ln_g, ln_b):
    w_in = w_in.astype(BF16)
    o_conv, q3, k3, vt4, kmean, rkv3, lora3, gates = _in_proj(
        x2.reshape(b, t, D_MODEL), w_in, conv_w, cos_t, sin_t)
    o_att = _moba(q3, k3, vt4, kmean.reshape(b, t // MOBA_BLOCK, BRANCH_DIM)).reshape(b * t, BRANCH_DIM)

    row = lambda a: a.reshape(1, -1)
    p = {
        "mu_r": row(shift_mu[0:BRANCH_DIM]), "mu_k": row(shift_mu[BRANCH_DIM:2 * BRANCH_DIM]),
        "mu_v": row(shift_mu[2 * BRANCH_DIM:3 * BRANCH_DIM]), "mu_l": row(shift_mu[3 * BRANCH_DIM:]),
        "w0": row(decay_w0), "w2": _pad_rows(decay_w2, 0),
        "a0": row(aaa_a0), "a2": _pad_rows(aaa_w2, DECAY_LORA),
        "g2": _pad_rows(gate_w2, DECAY_LORA + AAA_LORA),
        "k_k": row(k_k), "k_a": row(k_a), "r_k": row(r_k), "gn_g": row(gn_g), "gn_b": row(gn_b),
    }
    o_rwkv = _rwkv(rkv3, lora3, p)

    return _merge(o_conv.reshape(b * t, BRANCH_DIM), o_att, o_rwkv.reshape(b * t, BRANCH_DIM),
                  gates.reshape(b * t, GATE_COLS), x2,
                  w_branch.astype(BF16), w_out.astype(BF16), row(ln_g), row(ln_b))


def kernel(x, w_in, conv_w, shift_mu, decay_w0, decay_w2, aaa_a0, aaa_w2, gate_w2, k_k, k_a, r_k, gn_g, gn_b,
           w_branch, w_out, ln1_g, ln1_b, ln2_g, ln2_b, ffn_w_gate, ffn_w_up, ffn_w_down, router_w,
           moe_w_gate, moe_w_up, moe_w_down):
    b, t, d = x.shape
    x2 = x.reshape(b * t, d)
    cos_t, sin_t = _rope_tables(t)
    row = lambda a: a.reshape(1, -1)
    for l in range(DEPTH):
        x2 = _mixer(x2, b, t, cos_t, sin_t, w_in[l], conv_w[l], shift_mu[l], decay_w0[l], decay_w2[l],
                    aaa_a0[l], aaa_w2[l], gate_w2[l], k_k[l], k_a[l], r_k[l], gn_g[l], gn_b[l],
                    w_branch[l], w_out[l], ln1_g[l], ln1_b[l])
        j = l // 2
        if l % 2 == 0:
            x2 = _ffn(x2, ffn_w_gate[j].astype(BF16), ffn_w_up[j].astype(BF16), ffn_w_down[j].astype(BF16),
                      row(ln2_g[l]), row(ln2_b[l]))
        else:
            x2 = _moe_layer(x2, router_w[j], moe_w_gate[j].astype(BF16), moe_w_up[j].astype(BF16),
                            moe_w_down[j].astype(BF16), row(ln2_g[l]), row(ln2_b[l]))
    return x2.reshape(b, t, d)
```

```python
import functools

import jax
import jax.numpy as jnp
from jax import lax
from jax.experimental import pallas as pl
from jax.experimental.pallas import tpu as pltpu

F32 = jnp.float32
BF16 = jnp.bfloat16

D_MODEL = 1024
HEAD_DIM = 64
BRANCH_DIM = 512
N_HEADS = BRANCH_DIM // HEAD_DIM
DECAY_LORA = 64
AAA_LORA = 64
GATE_LORA = 128
LORA_DIM = DECAY_LORA + AAA_LORA + GATE_LORA
MIX_COLS = 9 * BRANCH_DIM
GATE_COLS = 3 * D_MODEL
MOBA_BLOCK = 256
MOBA_TOPK = 3
ROPE_THETA = 10000.0
D_FF = 2816
N_EXPERTS = 8
EXPERT_TOPK = 2
MOE_ROWS = 256
LN_EPS = 1e-5
GN_EPS = 64e-5
DEPTH = 2
DN_ALPHA = (2 * DEPTH) ** 0.25
LOG2_E = 1.4426950408889634

VMEM_LIMIT_BYTES = 56 * 1024 * 1024
SUBLANES = 8
BF16_ROWS = 16
RWKV_TILE = 256
RWKV_CHUNK = 64
FF_CHUNK = 256
SCATTER_UNROLL = 8
PAIR_WIDTH = 2 * HEAD_DIM
HEAD_SHIFT = HEAD_DIM.bit_length() - 1
CHUNK_SHIFT = RWKV_CHUNK.bit_length() - 1


def _cparams(*sem):
    return pltpu.CompilerParams(dimension_semantics=sem, vmem_limit_bytes=VMEM_LIMIT_BYTES)


def _vmem_whole():
    return pl.BlockSpec(memory_space=pltpu.VMEM)


def _dot(a, b):
    return jnp.dot(a.astype(BF16), b.astype(BF16), preferred_element_type=F32)


def _dot_nt(a, b):
    return lax.dot_general(a.astype(BF16), b.astype(BF16), (((1,), (1,)), ((), ())),
                           preferred_element_type=F32)


def _dot_tn(a, b):
    return lax.dot_general(a.astype(BF16), b.astype(BF16), (((0,), (0,)), ((), ())),
                           preferred_element_type=F32)


def _split_bf16(x):
    hi = x.astype(BF16)
    return hi, (x - hi.astype(F32)).astype(BF16)


def _dot_split(a, b, *, data_on_left):
    hi, lo = _split_bf16(a if data_on_left else b)
    if data_on_left:
        return jnp.dot(hi, b, preferred_element_type=F32) + jnp.dot(lo, b, preferred_element_type=F32)
    return jnp.dot(a, hi, preferred_element_type=F32) + jnp.dot(a, lo, preferred_element_type=F32)


def _sigmoid(x):
    return 1.0 / (1.0 + jnp.exp(-x))


def _layer_norm(y, g, b):
    mu = jnp.mean(y, axis=-1, keepdims=True)
    d = y - mu
    var = jnp.mean(d * d, axis=-1, keepdims=True)
    return d * lax.rsqrt(var + LN_EPS) * g + b


def _shift_rows(u, halo, n):
    out = pltpu.roll(u, n, axis=0)
    row = lax.broadcasted_iota(jnp.int32, u.shape, 0)
    last = halo.shape[0]
    for r in range(n):
        src = halo[last - n + r:last - n + r + 1, :]
        out = jnp.where(row == r, src, out)
    return out


def _halo_index(tile_rows, halo_rows):
    step = tile_rows // halo_rows

    def index(col):
        return lambda b, i: (b, jnp.maximum(i * step - 1, 0), col)
    return index


def _in_proj_kernel(x_ref, wm_ref, convw_ref, cos_ref, sin_ref,
                    conv_ref, q_ref, k_ref, vt_ref, km_ref, rkv_ref, lora_ref, gate_ref, halo_ref):
    first = pl.program_id(1) == 0
    bd = BRANCH_DIM
    xb = x_ref[0].astype(BF16)
    proj = lambda g: jnp.dot(xb, wm_ref[:, g * bd:(g + 1) * bd], preferred_element_type=F32)

    u = proj(2) * proj(0)
    halo = jnp.where(first, 0.0, halo_ref[...])
    w = convw_ref[...]
    y = w[2:3, :] * u + w[1:2, :] * _shift_rows(u, halo, 1) + w[0:1, :] * _shift_rows(u, halo, 2)
    conv_ref[0] = (proj(1) * y).astype(conv_ref.dtype)
    halo_ref[...] = u[u.shape[0] - SUBLANES:, :]

    cos = cos_ref[...]
    sin = sin_ref[...]
    lane = lax.broadcasted_iota(jnp.int32, cos.shape, 1)
    first_half = (lane & (HEAD_DIM - 1)) < (HEAD_DIM // 2)

    def rope(t):
        swapped = jnp.where(first_half, pltpu.roll(t, bd - HEAD_DIM // 2, axis=1),
                            pltpu.roll(t, HEAD_DIM // 2, axis=1))
        return t * cos + swapped * sin

    kr = rope(proj(4))
    q_ref[0] = rope(proj(3))
    k_ref[0] = kr.astype(BF16)
    km_ref[0, 0] = jnp.mean(kr, axis=0, keepdims=True)
    vt_ref[0, 0] = proj(5).astype(BF16).T

    for g in range(3):
        rkv_ref[0, :, g * bd:(g + 1) * bd] = proj(6 + g).astype(rkv_ref.dtype)
    lora_ref[0] = jnp.dot(xb, wm_ref[:, MIX_COLS:MIX_COLS + LORA_DIM], preferred_element_type=F32)
    for j in range(0, GATE_COLS, bd):
        col = MIX_COLS + LORA_DIM + j
        g = jnp.dot(xb, wm_ref[:, col:col + bd], preferred_element_type=F32)
        gate_ref[0, :, j:j + bd] = _sigmoid(g).astype(gate_ref.dtype)


def _in_proj(x3, w_in, conv_w, cos_t, sin_t):
    b, t, _ = x3.shape
    tm = MOBA_BLOCK
    nb = t // tm
    blk = lambda width: pl.BlockSpec((1, tm, width), lambda bi, i: (bi, i, 0))
    tab = pl.BlockSpec((tm, BRANCH_DIM), lambda bi, i: (i, 0))
    shape3 = lambda width, dtype: jax.ShapeDtypeStruct((b, t, width), dtype)
    return pl.pallas_call(
        _in_proj_kernel,
        grid=(b, nb),
        in_specs=[blk(D_MODEL), _vmem_whole(), pl.BlockSpec((3, BRANCH_DIM), lambda bi, i: (0, 0)), tab, tab],
        out_specs=[blk(BRANCH_DIM), blk(BRANCH_DIM), blk(BRANCH_DIM),
                   pl.BlockSpec((1, 1, BRANCH_DIM, tm), lambda bi, i: (bi, i, 0, 0)),
                   pl.BlockSpec((1, 1, 1, BRANCH_DIM), lambda bi, i: (bi, i, 0, 0)),
                   blk(3 * BRANCH_DIM), blk(LORA_DIM), blk(GATE_COLS)],
        out_shape=[shape3(BRANCH_DIM, BF16), shape3(BRANCH_DIM, F32), shape3(BRANCH_DIM, BF16),
                   jax.ShapeDtypeStruct((b, nb, BRANCH_DIM, tm), BF16),
                   jax.ShapeDtypeStruct((b, nb, 1, BRANCH_DIM), F32),
                   shape3(3 * BRANCH_DIM, BF16), shape3(LORA_DIM, F32), shape3(GATE_COLS, BF16)],
        scratch_shapes=[pltpu.VMEM((SUBLANES, BRANCH_DIM), F32)],
        compiler_params=_cparams("parallel", "arbitrary"),
        name="in_proj",
    )(x3, w_in, conv_w, cos_t, sin_t)


def _moba_kernel(q_ref, k_ref, vt_ref, km_ref, o_ref, *, nb, heads):
    i = pl.program_id(2)
    bs = MOBA_BLOCK
    width = heads * HEAD_DIM
    q = q_ref[0]
    km = km_ref[0]
    head_of_lane = lax.broadcasted_iota(jnp.int32, (1, width), 1) >> HEAD_SHIFT
    q_heads = jnp.concatenate([jnp.where(head_of_lane == g, q, 0.0) for g in range(heads)], axis=0)
    q_all = (q_heads * (HEAD_DIM ** -0.5 * LOG2_E)).astype(BF16)

    blk = lax.broadcasted_iota(jnp.int32, (nb, heads * bs), 0)
    valid = blk < i
    km_heads = jnp.concatenate([jnp.where(head_of_lane == g, km, 0.0) for g in range(heads)], axis=0)
    km_hi, km_lo = _split_bf16(km_heads)
    q_hi, q_lo = _split_bf16(q)
    hi_part = _dot_nt(jnp.concatenate([km_hi, km_lo], axis=0), q_hi)
    gate = hi_part[:heads * nb] + hi_part[heads * nb:] + _dot_nt(km_hi, q_lo)
    gate = jnp.concatenate([gate[g * nb:(g + 1) * nb] for g in range(heads)], axis=1)
    gate = jnp.where(valid, gate, -jnp.inf)
    rank = jnp.zeros((nb, heads * bs), jnp.int32)
    for m in range(nb):
        gm = gate[m:m + 1, :]
        beats = (gm > gate) | ((gm == gate) & (blk > m))
        rank = rank + beats.astype(jnp.int32)
    chosen = jnp.where(valid & (rank < MOBA_TOPK), (1 << blk).astype(F32), 0.0)
    picks = jnp.sum(chosen, axis=0, keepdims=True).astype(jnp.int32)

    ones_rows = jnp.ones((SUBLANES, bs), BF16)

    def scores(j, n_blocks):
        kb = k_ref[0, pl.ds(pl.multiple_of(j * bs, bs), n_blocks * bs), :]
        return lax.dot_general(kb, q_all, (((1,), (1,)), ((), ())), preferred_element_type=F32)

    def block_update(j, s, chosen, causal, carry):
        n_blocks = len(chosen)
        m, accs = carry
        if causal is not None:
            s = jnp.where(causal, s, -jnp.inf)
        parts = [s[n * bs:(n + 1) * bs] for n in range(n_blocks)]
        tops = [jnp.max(part, axis=0, keepdims=True) for part in parts]
        tops = [t if c is None else jnp.where(c, t, -jnp.inf) for t, c in zip(tops, chosen)]
        m_new = functools.reduce(jnp.maximum, tops, m)
        alpha = jnp.exp2(m - m_new)
        shifts = [m_new if c is None else jnp.where(c, m_new, jnp.inf) for c in chosen]
        ps = [jnp.exp2(part - shift).astype(BF16) for part, shift in zip(parts, shifts)]
        new_acc = []
        for g in range(heads):
            cols = slice(g * bs, (g + 1) * bs)
            acc = alpha[:, cols] * accs[g]
            for n in range(n_blocks):
                vt = jnp.concatenate([vt_ref[0, j + n, g * HEAD_DIM:(g + 1) * HEAD_DIM, :], ones_rows], axis=0)
                acc = acc + jnp.dot(vt, ps[n][:, cols], preferred_element_type=F32)
            new_acc.append(acc)
        return m_new, tuple(new_acc)

    key_id = lax.broadcasted_iota(jnp.int32, (bs, heads * bs), 0)
    query_id = lax.broadcasted_iota(jnp.int32, (bs, heads * bs), 1) & (bs - 1)
    init = (jnp.full((1, heads * bs), -jnp.inf, F32),
            tuple(jnp.zeros((HEAD_DIM + SUBLANES, bs), F32) for _ in range(heads)))
    carry = block_update(i, scores(i, 1), [None], key_id <= query_id, init)

    def pair_update(j, s, carry):
        bits = picks >> j
        return block_update(j, s, [(bits & 1) == 1, ((bits >> 1) & 1) == 1], None, carry)

    def two_pairs(jq, carry):
        j = 4 * jq
        s_a, s_b = scores(j, 2), scores(j + 2, 2)
        return pair_update(j + 2, s_b, pair_update(j, s_a, carry))

    carry = lax.fori_loop(0, i // 4, two_pairs, carry)
    done = (i // 4) * 4
    _, accs = lax.fori_loop(0, (i - done + 1) // 2,
                            lambda jj, c: pair_update(done + 2 * jj, scores(done + 2 * jj, 2), c), carry)
    out_t = jnp.concatenate([acc[:HEAD_DIM] / acc[HEAD_DIM:HEAD_DIM + 1] for acc in accs], axis=0)
    o_ref[0] = out_t.T.astype(o_ref.dtype)


def _moba(q3, k3, vt4, kmean, heads=4):
    b, t, _ = q3.shape
    nb = t // MOBA_BLOCK
    width = heads * HEAD_DIM
    qblk = pl.BlockSpec((1, MOBA_BLOCK, width), lambda bi, hi, i: (bi, i, hi))
    return pl.pallas_call(
        functools.partial(_moba_kernel, nb=nb, heads=heads),
        grid=(b, N_HEADS // heads, nb),
        in_specs=[qblk, pl.BlockSpec((1, t, width), lambda bi, hi, i: (bi, 0, hi)),
                  pl.BlockSpec((1, nb, width, MOBA_BLOCK), lambda bi, hi, i: (bi, 0, hi, 0)),
                  pl.BlockSpec((1, nb, width), lambda bi, hi, i: (bi, 0, hi))],
        out_specs=qblk,
        out_shape=jax.ShapeDtypeStruct((b, t, BRANCH_DIM), BF16),
        compiler_params=_cparams("parallel", "parallel", "arbitrary"),
        name="moba",
    )(q3, k3, vt4, kmean)


def _rwkv_kernel(r_ref, k_ref, v_ref, lo_ref, hr_ref, hk_ref, hv_ref, hl_ref,
                 mu_r_ref, mu_k_ref, mu_v_ref, mu_l_ref, w0_ref, w2_ref, a0_ref, a2_ref, g2_ref,
                 kk_ref, ka_ref, rk_ref, gng_ref, gnb_ref, o_ref, state_ref):
    i = pl.program_id(1)
    first = i == 0
    tq, ch = RWKV_TILE, RWKV_CHUNK
    hd, bd = HEAD_DIM, BRANCH_DIM

    @pl.when(first)
    def _():
        state_ref[...] = jnp.zeros_like(state_ref)

    def lerp(x_ref, h_ref, mu_ref):
        x = x_ref[0].astype(F32)
        halo = jnp.where(first, 0.0, h_ref[0].astype(F32))
        return x + (_shift_rows(x, halo, 1) - x) * mu_ref[...]

    r = lerp(r_ref, hr_ref, mu_r_ref)
    k = lerp(k_ref, hk_ref, mu_k_ref)
    v = lerp(v_ref, hv_ref, mu_v_ref)
    lo = lerp(lo_ref, hl_ref, mu_l_ref)

    z = w0_ref[...] + _dot(jnp.tanh(lo), w2_ref[...])
    ew = jnp.exp(-0.5) * _sigmoid(z)
    a = _sigmoid(a0_ref[...] + _dot(lo, a2_ref[...]))
    g = _dot(_sigmoid(lo), g2_ref[...])

    half = bd // 2
    lane_r = lax.broadcasted_iota(jnp.int32, (half, half), 0)
    lane_c = lax.broadcasted_iota(jnp.int32, (half, half), 1)
    ones_blocks = jnp.where((lane_r >> HEAD_SHIFT) == (lane_c >> HEAD_SHIFT), 1.0, 0.0).astype(BF16)

    def head_sum(x):
        return jnp.concatenate([_dot_split(x[:, :half], ones_blocks, data_on_left=True),
                                _dot_split(x[:, half:], ones_blocks, data_on_left=True)], axis=1)

    kk = k * kk_ref[...]
    kk = kk / jnp.maximum(jnp.sqrt(head_sum(kk * kk)), 1e-12)
    k2 = k * (1.0 + (a - 1.0) * ka_ref[...])
    bonus = head_sum(r * k2 * rk_ref[...]) * v

    t_r = lax.broadcasted_iota(jnp.int32, (tq, tq), 0)
    t_c = lax.broadcasted_iota(jnp.int32, (tq, tq), 1)
    same_chunk = (t_r >> CHUNK_SHIFT) == (t_c >> CHUNK_SHIFT)
    cs = _dot_split(jnp.where(same_chunk & (t_c <= t_r), 1.0, 0.0).astype(BF16), ew, data_on_left=False)
    cs_end = jnp.concatenate(
        [jnp.broadcast_to(cs[c * ch + ch - 1:c * ch + ch, :], (ch, bd)) for c in range(tq // ch)], axis=0)

    kka = kk * a
    a_t = -kk * jnp.exp(ew - cs)
    b_hat = (kka * jnp.exp(cs)).astype(BF16)
    k_hat = (k2 * jnp.exp(cs)).astype(BF16)
    r_t = r * jnp.exp(-cs)
    b_e = (kka * jnp.exp(cs - cs_end)).astype(BF16)
    k_e = (k2 * jnp.exp(cs - cs_end)).astype(BF16)
    p_c = jnp.exp(-cs_end)
    v_b = v.astype(BF16)

    strict = jnp.where(same_chunk & (t_c < t_r), 1.0, 0.0)
    incl = jnp.where(same_chunk & (t_c <= t_r), 1.0, 0.0)
    eye = jnp.where(t_c == t_r, 1.0, 0.0)
    pw = PAIR_WIDTH
    second = (lax.broadcasted_iota(jnp.int32, (1, pw), 1) >> HEAD_SHIFT) == 1
    same_head = ((lax.broadcasted_iota(jnp.int32, (pw, pw), 0) >> HEAD_SHIFT)
                 == (lax.broadcasted_iota(jnp.int32, (pw, pw), 1) >> HEAD_SHIFT))

    n_pairs = N_HEADS // 2
    pair = lambda x, p: x[:, p * pw:(p + 1) * pw]
    head_list = [(p, hm) for p in range(n_pairs) for hm in (~second, second)]
    a_hs = [jnp.where(hm, pair(a_t, p), 0.0).astype(BF16) for p, hm in head_list]
    l_abs = [strict * _dot_nt(a_h, pair(b_hat, p)) for a_h, (p, _) in zip(a_hs, head_list)]
    t_invs = [eye + x for x in l_abs]
    powers = l_abs
    n = 2
    while n < ch:
        powers = [_dot(x, x) for x in powers]
        t_invs = [t + _dot(t, x) for t, x in zip(t_invs, powers)]
        n *= 2
    l_aks = [strict * _dot_nt(a_h, pair(k_hat, p)) for a_h, (p, _) in zip(a_hs, head_list)]
    lakvs = [_dot(l_ak, pair(v_b, p)) for l_ak, (p, _) in zip(l_aks, head_list)]
    sols = [_dot(t, jnp.concatenate([pair(a_t, p).astype(BF16), lakv.astype(BF16)], axis=1))
            for t, lakv, (p, _) in zip(t_invs, lakvs, head_list)]
    r_hs = [jnp.where(hm, pair(r_t, p), 0.0).astype(BF16) for p, hm in head_list]
    m_rbs = [incl * _dot_nt(r_h, pair(b_hat, p)) for r_h, (p, _) in zip(r_hs, head_list)]
    m_rks = [incl * _dot_nt(r_h, pair(k_hat, p)) for r_h, (p, _) in zip(r_hs, head_list)]
    apps = [_dot(m_rb, sol) for m_rb, sol in zip(m_rbs, sols)]
    mrkvs = [_dot(m_rk, pair(v_b, p)) for m_rk, (p, _) in zip(m_rks, head_list)]

    def both(xs, p, cols):
        return jnp.where(second, xs[2 * p + 1][:, cols], xs[2 * p][:, cols])

    left, right = slice(0, pw), slice(pw, 2 * pw)
    a_til = [both(sols, p, left).astype(BF16) for p in range(n_pairs)]
    u_0 = [both(sols, p, right).astype(BF16) for p in range(n_pairs)]
    r_til = [(both(apps, p, left) + pair(r_t, p)).astype(BF16) for p in range(n_pairs)]
    y_hat = [both(apps, p, right) + both(mrkvs, p, left) for p in range(n_pairs)]

    states = [state_ref[p] for p in range(n_pairs)]
    y_rows = []
    for c in range(tq // ch):
        rows = slice(c * ch, (c + 1) * ch)
        g_cs = [jnp.where(same_head, _dot_tn(a_til[p][rows], pair(b_e, p)[rows]), 0.0) for p in range(n_pairs)]
        h_cs = [jnp.where(same_head, _dot_tn(u_0[p][rows], pair(b_e, p)[rows])
                          + _dot_tn(pair(v_b, p)[rows], pair(k_e, p)[rows]), 0.0) for p in range(n_pairs)]
        y_rows.append(jnp.concatenate(
            [_dot_nt(r_til[p][rows], states[p]) + y_hat[p][rows] for p in range(n_pairs)], axis=1))
        states = [states[p] * pair(p_c, p)[c * ch:c * ch + 1] + _dot(states[p], g_cs[p]) + h_cs[p]
                  for p in range(n_pairs)]
    for p in range(n_pairs):
        state_ref[p] = states[p]
    y = jnp.concatenate(y_rows, axis=0)

    mu = head_sum(y) * (1.0 / hd)
    d = y - mu
    var = head_sum(d * d) * (1.0 / hd)
    y = d * lax.rsqrt(var + GN_EPS) * gng_ref[...] + gnb_ref[...]
    o_ref[0] = ((y + bonus) * g).astype(o_ref.dtype)


def _rwkv(rkv3, lora3, p):
    b, t, _ = rkv3.shape
    tq = RWKV_TILE
    halo = _halo_index(tq, SUBLANES)
    halo_bf16 = _halo_index(tq, BF16_ROWS)
    blk = lambda col: pl.BlockSpec((1, tq, BRANCH_DIM), lambda bi, i: (bi, i, col))
    hblk = lambda col: pl.BlockSpec((1, BF16_ROWS, BRANCH_DIM), halo_bf16(col))
    vec = lambda width: pl.BlockSpec((1, width), lambda bi, i: (0, 0))
    mat = lambda rows: pl.BlockSpec((rows, BRANCH_DIM), lambda bi, i: (0, 0))
    return pl.pallas_call(
        _rwkv_kernel,
        grid=(b, t // tq),
        in_specs=[blk(0), blk(1), blk(2), pl.BlockSpec((1, tq, LORA_DIM), lambda bi, i: (bi, i, 0)),
                  hblk(0), hblk(1), hblk(2), pl.BlockSpec((1, SUBLANES, LORA_DIM), halo(0)),
                  vec(BRANCH_DIM), vec(BRANCH_DIM), vec(BRANCH_DIM), vec(LORA_DIM),
                  vec(BRANCH_DIM), mat(LORA_DIM), vec(BRANCH_DIM), mat(LORA_DIM), mat(LORA_DIM),
                  vec(BRANCH_DIM), vec(BRANCH_DIM), vec(BRANCH_DIM), vec(BRANCH_DIM), vec(BRANCH_DIM)],
        out_specs=pl.BlockSpec((1, tq, BRANCH_DIM), lambda bi, i: (bi, i, 0)),
        out_shape=jax.ShapeDtypeStruct((b, t, BRANCH_DIM), BF16),
        scratch_shapes=[pltpu.VMEM((N_HEADS // 2, PAIR_WIDTH, PAIR_WIDTH), F32)],
        compiler_params=_cparams("parallel", "arbitrary"),
        name="rwkv7",
    )(rkv3, rkv3, rkv3, lora3, rkv3, rkv3, rkv3, lora3,
      p["mu_r"], p["mu_k"], p["mu_v"], p["mu_l"], p["w0"], p["w2"], p["a0"], p["a2"], p["g2"],
      p["k_k"], p["k_a"], p["r_k"], p["gn_g"], p["gn_b"])


def _merge_kernel(oc_ref, oa_ref, or_ref, gate_ref, x_ref, wb_ref, wo_ref, g_ref, b_ref, o_ref):
    d = D_MODEL
    bd = BRANCH_DIM
    merged = (gate_ref[:, 0:d] * jnp.dot(oc_ref[...], wb_ref[0:bd, :], preferred_element_type=F32)
              + gate_ref[:, d:2 * d] * jnp.dot(oa_ref[...], wb_ref[bd:2 * bd, :], preferred_element_type=F32)
              + gate_ref[:, 2 * d:3 * d] * jnp.dot(or_ref[...], wb_ref[2 * bd:3 * bd, :],
                                                   preferred_element_type=F32))
    h = jnp.dot(merged.astype(BF16), wo_ref[...], preferred_element_type=F32)
    o_ref[...] = _layer_norm(DN_ALPHA * x_ref[...] + h, g_ref[...], b_ref[...])


def _merge(o_conv, o_att, o_rwkv, gates, x2, w_branch, w_out, ln_g, ln_b, tm=512):
    n = x2.shape[0]
    row = lambda i: (i, 0)
    vec = pl.BlockSpec((1, D_MODEL), lambda i: (0, 0))
    return pl.pallas_call(
        _merge_kernel,
        grid=(n // tm,),
        in_specs=[pl.BlockSpec((tm, BRANCH_DIM), row)] * 3
                 + [pl.BlockSpec((tm, GATE_COLS), row), pl.BlockSpec((tm, D_MODEL), row),
                    _vmem_whole(), _vmem_whole(), vec, vec],
        out_specs=pl.BlockSpec((tm, D_MODEL), row),
        out_shape=jax.ShapeDtypeStruct((n, D_MODEL), F32),
        compiler_params=_cparams("parallel"),
        name="merge_ln",
    )(o_conv, o_att, o_rwkv, gates, x2, w_branch, w_out, ln_g, ln_b)


def _swiglu_rows(xb, wg_ref, wu_ref, wd_ref, lead):
    hidden = []
    for c in range(0, D_FF, FF_CHUNK):
        cols = slice(c, c + FF_CHUNK)
        hg = jnp.dot(xb, wg_ref[lead + (slice(None), cols)], preferred_element_type=F32)
        hu = jnp.dot(xb, wu_ref[lead + (slice(None), cols)], preferred_element_type=F32)
        hidden.append((hg * _sigmoid(hg) * hu).astype(BF16))
    return jnp.dot(jnp.concatenate(hidden, axis=1), wd_ref[lead + (slice(None), slice(None))],
                   preferred_element_type=F32)


def _ffn_kernel(x_ref, wg_ref, wu_ref, wd_ref, g_ref, b_ref, o_ref):
    x = x_ref[...]
    f = _swiglu_rows(x.astype(BF16), wg_ref, wu_ref, wd_ref, ())
    o_ref[...] = _layer_norm(DN_ALPHA * x + f, g_ref[...], b_ref[...])


def _ffn(x2, wg, wu, wd, ln_g, ln_b, tm=512):
    n = x2.shape[0]
    row = lambda i: (i, 0)
    vec = pl.BlockSpec((1, D_MODEL), lambda i: (0, 0))
    return pl.pallas_call(
        _ffn_kernel,
        grid=(n // tm,),
        in_specs=[pl.BlockSpec((tm, D_MODEL), row), _vmem_whole(), _vmem_whole(), _vmem_whole(), vec, vec],
        out_specs=pl.BlockSpec((tm, D_MODEL), row),
        out_shape=jax.ShapeDtypeStruct((n, D_MODEL), F32),
        compiler_params=_cparams("parallel"),
        name="ffn_ln",
    )(x2, wg, wu, wd, ln_g, ln_b)


def _router_kernel(x_ref, w_ref, e_ref, g_ref):
    x_hi, x_lo = _split_bf16(x_ref[...])
    w_hi, w_lo = _split_bf16(w_ref[...])
    tm = x_hi.shape[0]
    hi_part = _dot(jnp.concatenate([x_hi, x_lo], axis=0), w_hi)
    logits = hi_part[:tm] + hi_part[tm:] + _dot(x_hi, w_lo)
    lane = lax.broadcasted_iota(jnp.int32, logits.shape, 1).astype(F32)
    m1 = jnp.max(logits, axis=1, keepdims=True)
    e1 = jnp.min(jnp.where(logits == m1, lane, float(N_EXPERTS)), axis=1, keepdims=True)
    rest = jnp.where(lane == e1, -jnp.inf, logits)
    m2 = jnp.max(rest, axis=1, keepdims=True)
    e2 = jnp.min(jnp.where(rest == m2, lane, float(N_EXPERTS)), axis=1, keepdims=True)
    w2 = jnp.exp(m2 - m1)
    denom = 1.0 + w2
    e_ref[...] = jnp.where(lane == 0.0, e1, e2).astype(jnp.int32)
    g_ref[...] = jnp.where(lane == 0.0, 1.0 / denom, w2 / denom)


def _router(x2, router_w, tm=512):
    n = x2.shape[0]
    row = lambda i: (i, 0)
    return pl.pallas_call(
        _router_kernel,
        grid=(n // tm,),
        in_specs=[pl.BlockSpec((tm, D_MODEL), row), pl.BlockSpec((D_MODEL, N_EXPERTS), lambda i: (0, 0))],
        out_specs=[pl.BlockSpec((tm, N_EXPERTS), row), pl.BlockSpec((tm, N_EXPERTS), row)],
        out_shape=[jax.ShapeDtypeStruct((n, N_EXPERTS), jnp.int32),
                   jax.ShapeDtypeStruct((n, N_EXPERTS), F32)],
        compiler_params=_cparams("parallel"),
        name="router",
    )(x2, router_w)


ROW_TILES = D_MODEL // 128


def _to_row_tiles(dst_ref, lead, x):
    for c in range(ROW_TILES):
        dst_ref[lead + (slice(None), c, slice(None))] = x[:, c * 128:(c + 1) * 128]


def _from_row_tiles(src_ref, dtype):
    return jnp.concatenate([src_ref[:, c, :].astype(dtype) for c in range(ROW_TILES)], axis=1)


def _scatter_step(step, n_steps, wait_step, start_this_step):
    @pl.when(step >= 2)
    def _():
        wait_step(step - 2)

    start_this_step()

    @pl.when(step == n_steps - 1)
    def _():
        @pl.when(step >= 1)
        def _():
            wait_step(step - 1)

        wait_step(step)


def _wait_rows(buf_rows, hbm_ref, n_rows, sem):
    pltpu.make_async_copy(buf_rows.at[pl.ds(0, n_rows)], hbm_ref.at[pl.ds(0, n_rows)], sem).wait()


def _dispatch_kernel(dest_ref, x_ref, init_ref, xb_ref, back_ref, buf, sems, *, tm, n_tokens, n_rows):
    del init_ref
    i = pl.program_id(0)

    @pl.when(i == 0)
    def _():
        def clear(r, carry):
            back_ref[r] = 0
            return carry
        lax.fori_loop(0, n_rows, clear, 0, unroll=8)

    def wait_step(step):
        for _ in range(EXPERT_TOPK):
            _wait_rows(buf.at[step % 2], xb_ref, tm, sems.at[step % 2])

    def start():
        slot = i % 2
        _to_row_tiles(buf, (slot,), x_ref[...])

        def one(r, carry):
            for k in range(EXPERT_TOPK):
                dst = dest_ref[(i * tm + r) * EXPERT_TOPK + k]
                back_ref[dst] = k * n_tokens + i * tm + r
                pltpu.make_async_copy(buf.at[slot, r], xb_ref.at[dst], sems.at[slot]).start(priority=k)
            return carry
        lax.fori_loop(0, tm, one, 0, unroll=SCATTER_UNROLL)

    _scatter_step(i, pl.num_programs(0), wait_step, start)


def _dispatch(x2, dest, n_rows, tm=256):
    n = x2.shape[0]
    init = jnp.zeros((n_rows, ROW_TILES, 128), F32)
    return pl.pallas_call(
        functools.partial(_dispatch_kernel, tm=tm, n_tokens=n, n_rows=n_rows),
        grid_spec=pltpu.PrefetchScalarGridSpec(
            num_scalar_prefetch=1,
            grid=(n // tm,),
            in_specs=[pl.BlockSpec((tm, D_MODEL), lambda i, d: (i, 0)), pl.BlockSpec(memory_space=pl.ANY)],
            out_specs=[pl.BlockSpec(memory_space=pl.ANY), pl.BlockSpec(memory_space=pltpu.SMEM)],
            scratch_shapes=[pltpu.VMEM((2, tm, ROW_TILES, 128), F32), pltpu.SemaphoreType.DMA((2,))],
        ),
        out_shape=[jax.ShapeDtypeStruct((n_rows, ROW_TILES, 128), F32),
                   jax.ShapeDtypeStruct((n_rows,), jnp.int32)],
        input_output_aliases={2: 0},
        compiler_params=_cparams("arbitrary"),
        name="moe_dispatch",
    )(dest, x2, init)


def _moe_kernel(blk_e_ref, n_valid_ref, row_dst_ref, x_ref, wg_ref, wu_ref, wd_ref, y_ref, xs, ybuf, sems):
    del blk_e_ref
    i = pl.program_id(0)

    def wait_step(step):
        @pl.when(n_valid_ref[step] > 0)
        def _():
            _wait_rows(ybuf.at[step % 2], y_ref, n_valid_ref[step], sems.at[step % 2])

    def start():
        @pl.when(n_valid_ref[i] > 0)
        def _():
            slot = i % 2
            xs[...] = _from_row_tiles(x_ref, BF16)
            _to_row_tiles(ybuf, (slot,), _swiglu_rows(xs[...], wg_ref, wu_ref, wd_ref, (0,)))

            def one(r, carry, priority=0):
                dst = row_dst_ref[i * MOE_ROWS + r]
                pltpu.make_async_copy(ybuf.at[slot, r], y_ref.at[dst], sems.at[slot]).start(priority=priority)
                return carry

            def group(rg, carry):
                for k in range(SCATTER_UNROLL):
                    one(SCATTER_UNROLL * rg + k, carry, priority=k % 2)
                return carry

            n_rows = n_valid_ref[i]
            n_groups = n_rows // SCATTER_UNROLL
            lax.fori_loop(0, n_groups, group, 0)
            lax.fori_loop(n_groups * SCATTER_UNROLL, n_rows, one, 0)

    _scatter_step(i, pl.num_programs(0), wait_step, start)


def _moe_experts(xb, blk_e, n_valid, row_dst, wg, wu, wd, n_out_rows):
    n_blocks = xb.shape[0] // MOE_ROWS
    weight = lambda shape: pl.BlockSpec((1,) + shape, lambda i, be, nv, rd: (be[i], 0, 0))
    return pl.pallas_call(
        _moe_kernel,
        grid_spec=pltpu.PrefetchScalarGridSpec(
            num_scalar_prefetch=3,
            grid=(n_blocks,),
            in_specs=[pl.BlockSpec((MOE_ROWS, ROW_TILES, 128), lambda i, be, nv, rd: (i, 0, 0)),
                      weight((D_MODEL, D_FF)), weight((D_MODEL, D_FF)), weight((D_FF, D_MODEL))],
            out_specs=pl.BlockSpec(memory_space=pl.ANY),
            scratch_shapes=[pltpu.VMEM((MOE_ROWS, D_MODEL), BF16),
                            pltpu.VMEM((2, MOE_ROWS, ROW_TILES, 128), F32),
                            pltpu.SemaphoreType.DMA((2,))],
        ),
        out_shape=jax.ShapeDtypeStruct((n_out_rows, ROW_TILES, 128), F32),
        compiler_params=_cparams("arbitrary"),
        name="moe_experts",
    )(blk_e, n_valid, row_dst, xb, wg, wu, wd)


def _combine_kernel(x_ref, y0_ref, y1_ref, gate_ref, g_ref, b_ref, o_ref):
    gate = gate_ref[...]
    f = gate[:, 0:1] * _from_row_tiles(y0_ref, F32) + gate[:, 1:2] * _from_row_tiles(y1_ref, F32)
    o_ref[...] = _layer_norm(DN_ALPHA * x_ref[...] + f, g_ref[...], b_ref[...])


def _combine(x2, y_slots, gate, ln_g, ln_b, tm=512):
    n = x2.shape[0]
    row = lambda i: (i, 0)
    vec = pl.BlockSpec((1, D_MODEL), lambda i: (0, 0))
    return pl.pallas_call(
        _combine_kernel,
        grid=(n // tm,),
        in_specs=[pl.BlockSpec((tm, D_MODEL), row),
                  pl.BlockSpec((tm, ROW_TILES, 128), lambda i: (i, 0, 0)),
                  pl.BlockSpec((tm, ROW_TILES, 128), lambda i: (n // tm + i, 0, 0)),
                  pl.BlockSpec((tm, EXPERT_TOPK), row), vec, vec],
        out_specs=pl.BlockSpec((tm, D_MODEL), row),
        out_shape=jax.ShapeDtypeStruct((n, D_MODEL), F32),
        compiler_params=_cparams("parallel"),
        name="combine_ln",
    )(x2, y_slots, y_slots, gate, ln_g, ln_b)


def _moe_layer(x2, router_w, wg, wu, wd, ln_g, ln_b):
    n = x2.shape[0]
    nk = n * EXPERT_TOPK
    top_e, gate = _router(x2, router_w)
    top_e = top_e[:, :EXPERT_TOPK]
    gate = gate[:, :EXPERT_TOPK]
    flat_e = top_e.reshape(-1)
    onehot = (flat_e[:, None] == jnp.arange(N_EXPERTS, dtype=jnp.int32)[None, :]).astype(jnp.int32)
    counts = jnp.sum(onehot, axis=0)
    rank = jnp.sum((jnp.cumsum(onehot, axis=0) - onehot) * onehot, axis=1)
    padded = (counts + MOE_ROWS - 1) // MOE_ROWS * MOE_ROWS
    pad_end = jnp.cumsum(padded)
    pad_start = pad_end - padded
    dest = (pad_start[flat_e] + rank).astype(jnp.int32)
    n_blocks = -(-nk // MOE_ROWS) + N_EXPERTS
    n_rows = n_blocks * MOE_ROWS
    blk_start = jnp.arange(n_blocks, dtype=jnp.int32) * MOE_ROWS
    blk_e = jnp.sum((blk_start[:, None] >= pad_end[None, :]).astype(jnp.int32), axis=1)
    valid_end = jnp.concatenate([pad_start + counts, jnp.zeros((1,), jnp.int32)])
    n_valid = jnp.clip(valid_end[blk_e] - blk_start, 0, MOE_ROWS).astype(jnp.int32)
    blk_e = jnp.minimum(blk_e, N_EXPERTS - 1).astype(jnp.int32)
    xb, row_dst = _dispatch(x2, dest, n_rows)
    y_slots = _moe_experts(xb, blk_e, n_valid, row_dst, wg, wu, wd, nk)
    return _combine(x2, y_slots, gate, ln_g, ln_b)


def _rope_tables(t):
    half = HEAD_DIM // 2
    inv = ROPE_THETA ** (-jnp.arange(half, dtype=F32) / half)
    ang = jnp.arange(t).astype(F32)[:, None] * inv[None, :]
    cos = jnp.cos(ang)
    sin = jnp.sin(ang)
    cos_t = jnp.tile(jnp.concatenate([cos, cos], -1), (1, N_HEADS))
    sin_t = jnp.tile(jnp.concatenate([-sin, sin], -1), (1, N_HEADS))
    return cos_t, sin_t


def _pad_rows(w, start):
    out = jnp.zeros((LORA_DIM, BRANCH_DIM), F32)
    return lax.dynamic_update_slice(out, w, (start, 0)).astype(BF16)


def _mixer(x2, b, t, cos_t, sin_t, w_in, conv_w, shift_mu, decay_w0, decay_w2, aaa_a0, aaa_w2, gate_w2,
           k_k, k_a, r_k, gn_g, gn_b, w_branch, w_out, ln_g, ln_b):
    w_in = w_in.astype(BF16)
    o_conv, q3, k3, vt4, kmean, rkv3, lora3, gates = _in_proj(
        x2.reshape(b, t, D_MODEL), w_in, conv_w, cos_t, sin_t)
    o_att = _moba(q3, k3, vt4, kmean.reshape(b, t // MOBA_BLOCK, BRANCH_DIM)).reshape(b * t, BRANCH_DIM)

    row = lambda a: a.reshape(1, -1)
    p = {
        "mu_r": row(shift_mu[0:BRANCH_DIM]), "mu_k": row(shift_mu[BRANCH_DIM:2 * BRANCH_DIM]),
        "mu_v": row(shift_mu[2 * BRANCH_DIM:3 * BRANCH_DIM]), "mu_l": row(shift_mu[3 * BRANCH_DIM:]),
        "w0": row(decay_w0), "w2": _pad_rows(decay_w2, 0),
        "a0": row(aaa_a0), "a2": _pad_rows(aaa_w2, DECAY_LORA),
        "g2": _pad_rows(gate_w2, DECAY_LORA + AAA_LORA),
        "k_k": row(k_k), "k_a": row(k_a), "r_k": row(r_k), "gn_g": row(gn_g), "gn_b": row(gn_b),
    }
    o_rwkv = _rwkv(rkv3, lora3, p)

    return _merge(o_conv.reshape(b * t, BRANCH_DIM), o_att, o_rwkv.reshape(b * t, BRANCH_DIM),
                  gates.reshape(b * t, GATE_COLS), x2,
                  w_branch.astype(BF16), w_out.astype(BF16), row(ln_g), row(ln_b))


def kernel(x, w_in, conv_w, shift_mu, decay_w0, decay_w2, aaa_a0, aaa_w2, gate_w2, k_k, k_a, r_k, gn_g, gn_b,
           w_branch, w_out, ln1_g, ln1_b, ln2_g, ln2_b, ffn_w_gate, ffn_w_up, ffn_w_down, router_w,
           moe_w_gate, moe_w_up, moe_w_down):
    b, t, d = x.shape
    x2 = x.reshape(b * t, d)
    cos_t, sin_t = _rope_tables(t)
    row = lambda a: a.reshape(1, -1)
    for l in range(DEPTH):
        x2 = _mixer(x2, b, t, cos_t, sin_t, w_in[l], conv_w[l], shift_mu[l], decay_w0[l], decay_w2[l],
                    aaa_a0[l], aaa_w2[l], gate_w2[l], k_k[l], k_a[l], r_k[l], gn_g[l], gn_b[l],
                    w_branch[l], w_out[l], ln1_g[l], ln1_b[l])
        j = l // 2
        if l % 2 == 0:
            x2 = _ffn(x2, ffn_w_gate[j].astype(BF16), ffn_w_up[j].astype(BF16), ffn_w_down[j].astype(BF16),
                      row(ln2_g[l]), row(ln2_b[l]))
        else:
            x2 = _moe_layer(x2, router_w[j], moe_w_gate[j].astype(BF16), moe_w_up[j].astype(BF16),
                            moe_w_down[j].astype(BF16), row(ln2_g[l]), row(ln2_b[l]))
    return x2.reshape(b, t, d)
```

```python
import functools

import jax
import jax.numpy as jnp
from jax import lax
from jax.experimental import pallas as pl
from jax.experimental.pallas import tpu as pltpu

F32 = jnp.float32
BF16 = jnp.bfloat16

D_MODEL = 1024
HEAD_DIM = 64
BRANCH_DIM = 512
N_HEADS = BRANCH_DIM // HEAD_DIM
DECAY_LORA = 64
AAA_LORA = 64
GATE_LORA = 128
LORA_DIM = DECAY_LORA + AAA_LORA + GATE_LORA
MIX_COLS = 9 * BRANCH_DIM
GATE_COLS = 3 * D_MODEL
MOBA_BLOCK = 256
MOBA_TOPK = 3
ROPE_THETA = 10000.0
D_FF = 2816
N_EXPERTS = 8
EXPERT_TOPK = 2
MOE_ROWS = 256
LN_EPS = 1e-5
GN_EPS = 64e-5
DEPTH = 2
DN_ALPHA = (2 * DEPTH) ** 0.25
LOG2_E = 1.4426950408889634

VMEM_LIMIT_BYTES = 56 * 1024 * 1024
SUBLANES = 8
BF16_ROWS = 16
RWKV_TILE = 256
RWKV_CHUNK = 64
FF_CHUNK = 256
SCATTER_UNROLL = 8
PAIR_WIDTH = 2 * HEAD_DIM
HEAD_SHIFT = HEAD_DIM.bit_length() - 1
CHUNK_SHIFT = RWKV_CHUNK.bit_length() - 1


def _cparams(*sem):
    return pltpu.CompilerParams(dimension_semantics=sem, vmem_limit_bytes=VMEM_LIMIT_BYTES)


def _vmem_whole():
    return pl.BlockSpec(memory_space=pltpu.VMEM)


def _dot(a, b):
    return jnp.dot(a.astype(BF16), b.astype(BF16), preferred_element_type=F32)


def _dot_nt(a, b):
    return lax.dot_general(a.astype(BF16), b.astype(BF16), (((1,), (1,)), ((), ())),
                           preferred_element_type=F32)


def _dot_tn(a, b):
    return lax.dot_general(a.astype(BF16), b.astype(BF16), (((0,), (0,)), ((), ())),
                           preferred_element_type=F32)


def _split_bf16(x):
    hi = x.astype(BF16)
    return hi, (x - hi.astype(F32)).astype(BF16)


def _dot_split(a, b, *, data_on_left):
    hi, lo = _split_bf16(a if data_on_left else b)
    if data_on_left:
        return jnp.dot(hi, b, preferred_element_type=F32) + jnp.dot(lo, b, preferred_element_type=F32)
    return jnp.dot(a, hi, preferred_element_type=F32) + jnp.dot(a, lo, preferred_element_type=F32)


def _sigmoid(x):
    return 1.0 / (1.0 + jnp.exp(-x))


def _layer_norm(y, g, b):
    mu = jnp.mean(y, axis=-1, keepdims=True)
    d = y - mu
    var = jnp.mean(d * d, axis=-1, keepdims=True)
    return d * lax.rsqrt(var + LN_EPS) * g + b


def _shift_rows(u, halo, n):
    out = pltpu.roll(u, n, axis=0)
    row = lax.broadcasted_iota(jnp.int32, u.shape, 0)
    last = halo.shape[0]
    for r in range(n):
        src = halo[last - n + r:last - n + r + 1, :]
        out = jnp.where(row == r, src, out)
    return out


def _halo_index(tile_rows, halo_rows):
    step = tile_rows // halo_rows

    def index(col):
        return lambda b, i: (b, jnp.maximum(i * step - 1, 0), col)
    return index


def _in_proj_kernel(*refs, n_side):
    x_ref, wm_ref, convw_ref, cos_ref, sin_ref = refs[:5]
    side_in = refs[5:5 + n_side]
    conv_ref, q_ref, k_ref, vt_ref, km_ref, rkv_ref, lora_ref, gate_ref = refs[5 + n_side:13 + n_side]
    side_out = refs[13 + n_side:13 + 2 * n_side]
    halo_ref = refs[13 + 2 * n_side]
    for src, dst in zip(side_in, side_out):
        dst[...] = src[...].astype(dst.dtype)

    first = pl.program_id(1) == 0
    bd = BRANCH_DIM
    xb = x_ref[0].astype(BF16)
    proj = lambda g: jnp.dot(xb, wm_ref[:, g * bd:(g + 1) * bd], preferred_element_type=F32)

    u = proj(2) * proj(0)
    halo = jnp.where(first, 0.0, halo_ref[...])
    w = convw_ref[...]
    y = w[2:3, :] * u + w[1:2, :] * _shift_rows(u, halo, 1) + w[0:1, :] * _shift_rows(u, halo, 2)
    conv_ref[0] = (proj(1) * y).astype(conv_ref.dtype)
    halo_ref[...] = u[u.shape[0] - SUBLANES:, :]

    cos = cos_ref[...]
    sin = sin_ref[...]
    lane = lax.broadcasted_iota(jnp.int32, cos.shape, 1)
    first_half = (lane & (HEAD_DIM - 1)) < (HEAD_DIM // 2)

    def rope(t):
        swapped = jnp.where(first_half, pltpu.roll(t, bd - HEAD_DIM // 2, axis=1),
                            pltpu.roll(t, HEAD_DIM // 2, axis=1))
        return t * cos + swapped * sin

    kr = rope(proj(4))
    q_ref[0] = rope(proj(3))
    k_ref[0] = kr.astype(BF16)
    km_ref[0, 0] = jnp.mean(kr, axis=0, keepdims=True)
    vt_ref[0, 0] = proj(5).astype(BF16).T

    for g in range(3):
        rkv_ref[0, :, g * bd:(g + 1) * bd] = proj(6 + g).astype(rkv_ref.dtype)
    lora_ref[0] = jnp.dot(xb, wm_ref[:, MIX_COLS:MIX_COLS + LORA_DIM], preferred_element_type=F32)
    for j in range(0, GATE_COLS, bd):
        col = MIX_COLS + LORA_DIM + j
        g = jnp.dot(xb, wm_ref[:, col:col + bd], preferred_element_type=F32)
        gate_ref[0, :, j:j + bd] = _sigmoid(g).astype(gate_ref.dtype)


def _in_proj(x3, w_in, conv_w, cos_t, sin_t, side_casts=()):
    b, t, _ = x3.shape
    tm = MOBA_BLOCK
    nb = t // tm
    blk = lambda width: pl.BlockSpec((1, tm, width), lambda bi, i: (bi, i, 0))
    tab = pl.BlockSpec((tm, BRANCH_DIM), lambda bi, i: (i, 0))
    shape3 = lambda width, dtype: jax.ShapeDtypeStruct((b, t, width), dtype)
    for a in side_casts:
        assert a.shape[0] % (b * nb * BF16_ROWS) == 0, a.shape
    slab = lambda a: pl.BlockSpec((a.shape[0] // (b * nb), a.shape[1]), lambda bi, i: (bi * nb + i, 0))
    return pl.pallas_call(
        functools.partial(_in_proj_kernel, n_side=len(side_casts)),
        grid=(b, nb),
        in_specs=[blk(D_MODEL), _vmem_whole(), pl.BlockSpec((3, BRANCH_DIM), lambda bi, i: (0, 0)), tab, tab]
                 + [slab(a) for a in side_casts],
        out_specs=[blk(BRANCH_DIM), blk(BRANCH_DIM), blk(BRANCH_DIM),
                   pl.BlockSpec((1, 1, BRANCH_DIM, tm), lambda bi, i: (bi, i, 0, 0)),
                   pl.BlockSpec((1, 1, 1, BRANCH_DIM), lambda bi, i: (bi, i, 0, 0)),
                   blk(3 * BRANCH_DIM), blk(LORA_DIM), blk(GATE_COLS)] + [slab(a) for a in side_casts],
        out_shape=[shape3(BRANCH_DIM, BF16), shape3(BRANCH_DIM, F32), shape3(BRANCH_DIM, BF16),
                   jax.ShapeDtypeStruct((b, nb, BRANCH_DIM, tm), BF16),
                   jax.ShapeDtypeStruct((b, nb, 1, BRANCH_DIM), F32),
                   shape3(3 * BRANCH_DIM, BF16), shape3(LORA_DIM, F32), shape3(GATE_COLS, BF16)]
                  + [jax.ShapeDtypeStruct(a.shape, BF16) for a in side_casts],
        scratch_shapes=[pltpu.VMEM((SUBLANES, BRANCH_DIM), F32)],
        compiler_params=_cparams("parallel", "arbitrary"),
        name="in_proj",
    )(x3, w_in, conv_w, cos_t, sin_t, *side_casts)


def _moba_kernel(q_ref, k_ref, vt_ref, km_ref, o_ref, *, nb, heads):
    i = pl.program_id(2)
    bs = MOBA_BLOCK
    width = heads * HEAD_DIM
    q = q_ref[0]
    km = km_ref[0]
    head_of_lane = lax.broadcasted_iota(jnp.int32, (1, width), 1) >> HEAD_SHIFT
    q_heads = jnp.concatenate([jnp.where(head_of_lane == g, q, 0.0) for g in range(heads)], axis=0)
    q_all = (q_heads * (HEAD_DIM ** -0.5 * LOG2_E)).astype(BF16)

    blk = lax.broadcasted_iota(jnp.int32, (nb, heads * bs), 0)
    valid = blk < i
    km_heads = jnp.concatenate([jnp.where(head_of_lane == g, km, 0.0) for g in range(heads)], axis=0)
    km_hi, km_lo = _split_bf16(km_heads)
    q_hi, q_lo = _split_bf16(q)
    hi_part = _dot_nt(jnp.concatenate([km_hi, km_lo], axis=0), q_hi)
    gate = hi_part[:heads * nb] + hi_part[heads * nb:] + _dot_nt(km_hi, q_lo)
    gate = jnp.concatenate([gate[g * nb:(g + 1) * nb] for g in range(heads)], axis=1)
    gate = jnp.where(valid, gate, -jnp.inf)
    rank = jnp.zeros((nb, heads * bs), jnp.int32)
    for m in range(nb):
        gm = gate[m:m + 1, :]
        beats = (gm > gate) | ((gm == gate) & (blk > m))
        rank = rank + beats.astype(jnp.int32)
    chosen = jnp.where(valid & (rank < MOBA_TOPK), (1 << blk).astype(F32), 0.0)
    picks = jnp.sum(chosen, axis=0, keepdims=True).astype(jnp.int32)

    ones_rows = jnp.ones((SUBLANES, bs), BF16)

    def scores(j, n_blocks):
        kb = k_ref[0, pl.ds(pl.multiple_of(j * bs, bs), n_blocks * bs), :]
        return lax.dot_general(kb, q_all, (((1,), (1,)), ((), ())), preferred_element_type=F32)

    def block_update(j, s, chosen, causal, carry):
        n_blocks = len(chosen)
        m, accs = carry
        if causal is not None:
            s = jnp.where(causal, s, -jnp.inf)
        parts = [s[n * bs:(n + 1) * bs] for n in range(n_blocks)]
        tops = [jnp.max(part, axis=0, keepdims=True) for part in parts]
        tops = [t if c is None else jnp.where(c, t, -jnp.inf) for t, c in zip(tops, chosen)]
        m_new = functools.reduce(jnp.maximum, tops, m)
        alpha = jnp.exp2(m - m_new)
        shifts = [m_new if c is None else jnp.where(c, m_new, jnp.inf) for c in chosen]
        ps = [jnp.exp2(part - shift).astype(BF16) for part, shift in zip(parts, shifts)]
        new_acc = []
        for g in range(heads):
            cols = slice(g * bs, (g + 1) * bs)
            acc = alpha[:, cols] * accs[g]
            for n in range(n_blocks):
                vt = jnp.concatenate([vt_ref[0, j + n, g * HEAD_DIM:(g + 1) * HEAD_DIM, :], ones_rows], axis=0)
                acc = acc + jnp.dot(vt, ps[n][:, cols], preferred_element_type=F32)
            new_acc.append(acc)
        return m_new, tuple(new_acc)

    key_id = lax.broadcasted_iota(jnp.int32, (bs, heads * bs), 0)
    query_id = lax.broadcasted_iota(jnp.int32, (bs, heads * bs), 1) & (bs - 1)
    init = (jnp.full((1, heads * bs), -jnp.inf, F32),
            tuple(jnp.zeros((HEAD_DIM + SUBLANES, bs), F32) for _ in range(heads)))
    carry = block_update(i, scores(i, 1), [None], key_id <= query_id, init)

    def pair_update(j, s, carry):
        bits = picks >> j
        return block_update(j, s, [(bits & 1) == 1, ((bits >> 1) & 1) == 1], None, carry)

    def two_pairs(jq, carry):
        j = 4 * jq
        s_a, s_b = scores(j, 2), scores(j + 2, 2)
        return pair_update(j + 2, s_b, pair_update(j, s_a, carry))

    carry = lax.fori_loop(0, i // 4, two_pairs, carry)
    done = (i // 4) * 4
    _, accs = lax.fori_loop(0, (i - done + 1) // 2,
                            lambda jj, c: pair_update(done + 2 * jj, scores(done + 2 * jj, 2), c), carry)
    out_t = jnp.concatenate([acc[:HEAD_DIM] / acc[HEAD_DIM:HEAD_DIM + 1] for acc in accs], axis=0)
    o_ref[0] = out_t.T.astype(o_ref.dtype)


def _moba(q3, k3, vt4, kmean, heads=4):
    b, t, _ = q3.shape
    nb = t // MOBA_BLOCK
    width = heads * HEAD_DIM
    qblk = pl.BlockSpec((1, MOBA_BLOCK, width), lambda bi, hi, i: (bi, i, hi))
    return pl.pallas_call(
        functools.partial(_moba_kernel, nb=nb, heads=heads),
        grid=(b, N_HEADS // heads, nb),
        in_specs=[qblk, pl.BlockSpec((1, t, width), lambda bi, hi, i: (bi, 0, hi)),
                  pl.BlockSpec((1, nb, width, MOBA_BLOCK), lambda bi, hi, i: (bi, 0, hi, 0)),
                  pl.BlockSpec((1, nb, width), lambda bi, hi, i: (bi, 0, hi))],
        out_specs=qblk,
        out_shape=jax.ShapeDtypeStruct((b, t, BRANCH_DIM), BF16),
        compiler_params=_cparams("parallel", "parallel", "arbitrary"),
        name="moba",
    )(q3, k3, vt4, kmean)


def _rwkv_kernel(r_ref, k_ref, v_ref, lo_ref, hr_ref, hk_ref, hv_ref, hl_ref,
                 mu_r_ref, mu_k_ref, mu_v_ref, mu_l_ref, w0_ref, w2_ref, a0_ref, a2_ref, g2_ref,
                 kk_ref, ka_ref, rk_ref, gng_ref, gnb_ref, o_ref, state_ref):
    i = pl.program_id(1)
    first = i == 0
    tq, ch = RWKV_TILE, RWKV_CHUNK
    hd, bd = HEAD_DIM, BRANCH_DIM

    @pl.when(first)
    def _():
        state_ref[...] = jnp.zeros_like(state_ref)

    def lerp(x_ref, h_ref, mu_ref):
        x = x_ref[0].astype(F32)
        halo = jnp.where(first, 0.0, h_ref[0].astype(F32))
        return x + (_shift_rows(x, halo, 1) - x) * mu_ref[...]

    r = lerp(r_ref, hr_ref, mu_r_ref)
    k = lerp(k_ref, hk_ref, mu_k_ref)
    v = lerp(v_ref, hv_ref, mu_v_ref)
    lo = lerp(lo_ref, hl_ref, mu_l_ref)

    z = w0_ref[...] + _dot(jnp.tanh(lo), w2_ref[...])
    ew = jnp.exp(-0.5) * _sigmoid(z)
    a = _sigmoid(a0_ref[...] + _dot(lo, a2_ref[...]))
    g = _dot(_sigmoid(lo), g2_ref[...])

    half = bd // 2
    lane_r = lax.broadcasted_iota(jnp.int32, (half, half), 0)
    lane_c = lax.broadcasted_iota(jnp.int32, (half, half), 1)
    ones_blocks = jnp.where((lane_r >> HEAD_SHIFT) == (lane_c >> HEAD_SHIFT), 1.0, 0.0).astype(BF16)

    def head_sum(x):
        return jnp.concatenate([_dot_split(x[:, :half], ones_blocks, data_on_left=True),
                                _dot_split(x[:, half:], ones_blocks, data_on_left=True)], axis=1)

    kk = k * kk_ref[...]
    kk = kk / jnp.maximum(jnp.sqrt(head_sum(kk * kk)), 1e-12)
    k2 = k * (1.0 + (a - 1.0) * ka_ref[...])
    bonus = head_sum(r * k2 * rk_ref[...]) * v

    t_r = lax.broadcasted_iota(jnp.int32, (tq, tq), 0)
    t_c = lax.broadcasted_iota(jnp.int32, (tq, tq), 1)
    same_chunk = (t_r >> CHUNK_SHIFT) == (t_c >> CHUNK_SHIFT)
    cs = _dot_split(jnp.where(same_chunk & (t_c <= t_r), 1.0, 0.0).astype(BF16), ew, data_on_left=False)
    cs_end = jnp.concatenate(
        [jnp.broadcast_to(cs[c * ch + ch - 1:c * ch + ch, :], (ch, bd)) for c in range(tq // ch)], axis=0)

    kka = kk * a
    a_t = -kk * jnp.exp(ew - cs)
    b_hat = (kka * jnp.exp(cs)).astype(BF16)
    k_hat = (k2 * jnp.exp(cs)).astype(BF16)
    r_t = r * jnp.exp(-cs)
    b_e = (kka * jnp.exp(cs - cs_end)).astype(BF16)
    k_e = (k2 * jnp.exp(cs - cs_end)).astype(BF16)
    p_c = jnp.exp(-cs_end)
    v_b = v.astype(BF16)

    strict = jnp.where(same_chunk & (t_c < t_r), 1.0, 0.0)
    incl = jnp.where(same_chunk & (t_c <= t_r), 1.0, 0.0)
    eye = jnp.where(t_c == t_r, 1.0, 0.0)
    pw = PAIR_WIDTH
    second = (lax.broadcasted_iota(jnp.int32, (1, pw), 1) >> HEAD_SHIFT) == 1
    same_head = ((lax.broadcasted_iota(jnp.int32, (pw, pw), 0) >> HEAD_SHIFT)
                 == (lax.broadcasted_iota(jnp.int32, (pw, pw), 1) >> HEAD_SHIFT))

    n_pairs = N_HEADS // 2
    pair = lambda x, p: x[:, p * pw:(p + 1) * pw]
    head_list = [(p, hm) for p in range(n_pairs) for hm in (~second, second)]
    a_hs = [jnp.where(hm, pair(a_t, p), 0.0).astype(BF16) for p, hm in head_list]
    l_abs = [strict * _dot_nt(a_h, pair(b_hat, p)) for a_h, (p, _) in zip(a_hs, head_list)]
    t_invs = [eye + x for x in l_abs]
    powers = l_abs
    n = 2
    while n < ch:
        powers = [_dot(x, x) for x in powers]
        t_invs = [t + _dot(t, x) for t, x in zip(t_invs, powers)]
        n *= 2
    l_aks = [strict * _dot_nt(a_h, pair(k_hat, p)) for a_h, (p, _) in zip(a_hs, head_list)]
    lakvs = [_dot(l_ak, pair(v_b, p)) for l_ak, (p, _) in zip(l_aks, head_list)]
    sols = [_dot(t, jnp.concatenate([pair(a_t, p).astype(BF16), lakv.astype(BF16)], axis=1))
            for t, lakv, (p, _) in zip(t_invs, lakvs, head_list)]
    r_hs = [jnp.where(hm, pair(r_t, p), 0.0).astype(BF16) for p, hm in head_list]
    m_rbs = [incl * _dot_nt(r_h, pair(b_hat, p)) for r_h, (p, _) in zip(r_hs, head_list)]
    m_rks = [incl * _dot_nt(r_h, pair(k_hat, p)) for r_h, (p, _) in zip(r_hs, head_list)]
    apps = [_dot(m_rb, sol) for m_rb, sol in zip(m_rbs, sols)]
    mrkvs = [_dot(m_rk, pair(v_b, p)) for m_rk, (p, _) in zip(m_rks, head_list)]

    def both(xs, p, cols):
        return jnp.where(second, xs[2 * p + 1][:, cols], xs[2 * p][:, cols])

    left, right = slice(0, pw), slice(pw, 2 * pw)
    a_til = [both(sols, p, left).astype(BF16) for p in range(n_pairs)]
    u_0 = [both(sols, p, right).astype(BF16) for p in range(n_pairs)]
    r_til = [(both(apps, p, left) + pair(r_t, p)).astype(BF16) for p in range(n_pairs)]
    y_hat = [both(apps, p, right) + both(mrkvs, p, left) for p in range(n_pairs)]

    states = [state_ref[p] for p in range(n_pairs)]
    y_rows = []
    for c in range(tq // ch):
        rows = slice(c * ch, (c + 1) * ch)
        g_cs = [jnp.where(same_head, _dot_tn(a_til[p][rows], pair(b_e, p)[rows]), 0.0) for p in range(n_pairs)]
        h_cs = [jnp.where(same_head, _dot_tn(u_0[p][rows], pair(b_e, p)[rows])
                          + _dot_tn(pair(v_b, p)[rows], pair(k_e, p)[rows]), 0.0) for p in range(n_pairs)]
        y_rows.append(jnp.concatenate(
            [_dot_nt(r_til[p][rows], states[p]) + y_hat[p][rows] for p in range(n_pairs)], axis=1))
        states = [states[p] * pair(p_c, p)[c * ch:c * ch + 1] + _dot(states[p], g_cs[p]) + h_cs[p]
                  for p in range(n_pairs)]
    for p in range(n_pairs):
        state_ref[p] = states[p]
    y = jnp.concatenate(y_rows, axis=0)

    mu = head_sum(y) * (1.0 / hd)
    d = y - mu
    var = head_sum(d * d) * (1.0 / hd)
    y = d * lax.rsqrt(var + GN_EPS) * gng_ref[...] + gnb_ref[...]
    o_ref[0] = ((y + bonus) * g).astype(o_ref.dtype)


def _rwkv(rkv3, lora3, p):
    b, t, _ = rkv3.shape
    tq = RWKV_TILE
    halo = _halo_index(tq, SUBLANES)
    halo_bf16 = _halo_index(tq, BF16_ROWS)
    blk = lambda col: pl.BlockSpec((1, tq, BRANCH_DIM), lambda bi, i: (bi, i, col))
    hblk = lambda col: pl.BlockSpec((1, BF16_ROWS, BRANCH_DIM), halo_bf16(col))
    vec = lambda width: pl.BlockSpec((1, width), lambda bi, i: (0, 0))
    mat = lambda rows: pl.BlockSpec((rows, BRANCH_DIM), lambda bi, i: (0, 0))
    return pl.pallas_call(
        _rwkv_kernel,
        grid=(b, t // tq),
        in_specs=[blk(0), blk(1), blk(2), pl.BlockSpec((1, tq, LORA_DIM), lambda bi, i: (bi, i, 0)),
                  hblk(0), hblk(1), hblk(2), pl.BlockSpec((1, SUBLANES, LORA_DIM), halo(0)),
                  vec(BRANCH_DIM), vec(BRANCH_DIM), vec(BRANCH_DIM), vec(LORA_DIM),
                  vec(BRANCH_DIM), mat(LORA_DIM), vec(BRANCH_DIM), mat(LORA_DIM), mat(LORA_DIM),
                  vec(BRANCH_DIM), vec(BRANCH_DIM), vec(BRANCH_DIM), vec(BRANCH_DIM), vec(BRANCH_DIM)],
        out_specs=pl.BlockSpec((1, tq, BRANCH_DIM), lambda bi, i: (bi, i, 0)),
        out_shape=jax.ShapeDtypeStruct((b, t, BRANCH_DIM), BF16),
        scratch_shapes=[pltpu.VMEM((N_HEADS // 2, PAIR_WIDTH, PAIR_WIDTH), F32)],
        compiler_params=_cparams("parallel", "arbitrary"),
        name="rwkv7",
    )(rkv3, rkv3, rkv3, lora3, rkv3, rkv3, rkv3, lora3,
      p["mu_r"], p["mu_k"], p["mu_v"], p["mu_l"], p["w0"], p["w2"], p["a0"], p["a2"], p["g2"],
      p["k_k"], p["k_a"], p["r_k"], p["gn_g"], p["gn_b"])


def _merge_kernel(oc_ref, oa_ref, or_ref, gate_ref, x_ref, wb_ref, wo_ref, g_ref, b_ref, o_ref):
    d = D_MODEL
    bd = BRANCH_DIM
    merged = (gate_ref[:, 0:d] * jnp.dot(oc_ref[...], wb_ref[0:bd, :], preferred_element_type=F32)
              + gate_ref[:, d:2 * d] * jnp.dot(oa_ref[...], wb_ref[bd:2 * bd, :], preferred_element_type=F32)
              + gate_ref[:, 2 * d:3 * d] * jnp.dot(or_ref[...], wb_ref[2 * bd:3 * bd, :],
                                                   preferred_element_type=F32))
    h = jnp.dot(merged.astype(BF16), wo_ref[...], preferred_element_type=F32)
    o_ref[...] = _layer_norm(DN_ALPHA * x_ref[...] + h, g_ref[...], b_ref[...])


def _merge(o_conv, o_att, o_rwkv, gates, x2, w_branch, w_out, ln_g, ln_b, tm=512):
    n = x2.shape[0]
    row = lambda i: (i, 0)
    vec = pl.BlockSpec((1, D_MODEL), lambda i: (0, 0))
    return pl.pallas_call(
        _merge_kernel,
        grid=(n // tm,),
        in_specs=[pl.BlockSpec((tm, BRANCH_DIM), row)] * 3
                 + [pl.BlockSpec((tm, GATE_COLS), row), pl.BlockSpec((tm, D_MODEL), row),
                    _vmem_whole(), _vmem_whole(), vec, vec],
        out_specs=pl.BlockSpec((tm, D_MODEL), row),
        out_shape=jax.ShapeDtypeStruct((n, D_MODEL), F32),
        compiler_params=_cparams("parallel"),
        name="merge_ln",
    )(o_conv, o_att, o_rwkv, gates, x2, w_branch, w_out, ln_g, ln_b)


def _swiglu_rows(xb, wg_ref, wu_ref, wd_ref, lead):
    hidden = []
    for c in range(0, D_FF, FF_CHUNK):
        cols = slice(c, c + FF_CHUNK)
        hg = jnp.dot(xb, wg_ref[lead + (slice(None), cols)], preferred_element_type=F32)
        hu = jnp.dot(xb, wu_ref[lead + (slice(None), cols)], preferred_element_type=F32)
        hidden.append((hg * _sigmoid(hg) * hu).astype(BF16))
    return jnp.dot(jnp.concatenate(hidden, axis=1), wd_ref[lead + (slice(None), slice(None))],
                   preferred_element_type=F32)


def _ffn_kernel(x_ref, wg_ref, wu_ref, wd_ref, g_ref, b_ref, o_ref):
    x = x_ref[...]
    f = _swiglu_rows(x.astype(BF16), wg_ref, wu_ref, wd_ref, ())
    o_ref[...] = _layer_norm(DN_ALPHA * x + f, g_ref[...], b_ref[...])


def _ffn(x2, wg, wu, wd, ln_g, ln_b, tm=512):
    n = x2.shape[0]
    row = lambda i: (i, 0)
    vec = pl.BlockSpec((1, D_MODEL), lambda i: (0, 0))
    return pl.pallas_call(
        _ffn_kernel,
        grid=(n // tm,),
        in_specs=[pl.BlockSpec((tm, D_MODEL), row), _vmem_whole(), _vmem_whole(), _vmem_whole(), vec, vec],
        out_specs=pl.BlockSpec((tm, D_MODEL), row),
        out_shape=jax.ShapeDtypeStruct((n, D_MODEL), F32),
        compiler_params=_cparams("parallel"),
        name="ffn_ln",
    )(x2, wg, wu, wd, ln_g, ln_b)


def _router_kernel(x_ref, w_ref, e_ref, g_ref):
    x_hi, x_lo = _split_bf16(x_ref[...])
    w_hi, w_lo = _split_bf16(w_ref[...])
    tm = x_hi.shape[0]
    hi_part = _dot(jnp.concatenate([x_hi, x_lo], axis=0), w_hi)
    logits = hi_part[:tm] + hi_part[tm:] + _dot(x_hi, w_lo)
    lane = lax.broadcasted_iota(jnp.int32, logits.shape, 1).astype(F32)
    m1 = jnp.max(logits, axis=1, keepdims=True)
    e1 = jnp.min(jnp.where(logits == m1, lane, float(N_EXPERTS)), axis=1, keepdims=True)
    rest = jnp.where(lane == e1, -jnp.inf, logits)
    m2 = jnp.max(rest, axis=1, keepdims=True)
    e2 = jnp.min(jnp.where(rest == m2, lane, float(N_EXPERTS)), axis=1, keepdims=True)
    w2 = jnp.exp(m2 - m1)
    denom = 1.0 + w2
    e_ref[...] = jnp.where(lane == 0.0, e1, e2).astype(jnp.int32)
    g_ref[...] = jnp.where(lane == 0.0, 1.0 / denom, w2 / denom)


def _router(x2, router_w, tm=512):
    n = x2.shape[0]
    row = lambda i: (i, 0)
    return pl.pallas_call(
        _router_kernel,
        grid=(n // tm,),
        in_specs=[pl.BlockSpec((tm, D_MODEL), row), pl.BlockSpec((D_MODEL, N_EXPERTS), lambda i: (0, 0))],
        out_specs=[pl.BlockSpec((tm, N_EXPERTS), row), pl.BlockSpec((tm, N_EXPERTS), row)],
        out_shape=[jax.ShapeDtypeStruct((n, N_EXPERTS), jnp.int32),
                   jax.ShapeDtypeStruct((n, N_EXPERTS), F32)],
        compiler_params=_cparams("parallel"),
        name="router",
    )(x2, router_w)


ROW_TILES = D_MODEL // 128


def _to_row_tiles(dst_ref, lead, x):
    for c in range(ROW_TILES):
        dst_ref[lead + (slice(None), c, slice(None))] = x[:, c * 128:(c + 1) * 128]


def _from_row_tiles(src_ref, dtype):
    return jnp.concatenate([src_ref[:, c, :].astype(dtype) for c in range(ROW_TILES)], axis=1)


def _scatter_step(step, n_steps, wait_step, start_this_step):
    @pl.when(step >= 2)
    def _():
        wait_step(step - 2)

    start_this_step()

    @pl.when(step == n_steps - 1)
    def _():
        @pl.when(step >= 1)
        def _():
            wait_step(step - 1)

        wait_step(step)


def _wait_rows(buf_rows, hbm_ref, n_rows, sem):
    pltpu.make_async_copy(buf_rows.at[pl.ds(0, n_rows)], hbm_ref.at[pl.ds(0, n_rows)], sem).wait()


def _dispatch_kernel(dest_ref, x_ref, init_ref, xb_ref, back_ref, buf, sems, *, tm, n_tokens, n_rows):
    del init_ref
    i = pl.program_id(0)

    @pl.when(i == 0)
    def _():
        def clear(r, carry):
            back_ref[r] = 0
            return carry
        lax.fori_loop(0, n_rows, clear, 0, unroll=8)

    def wait_step(step):
        for _ in range(EXPERT_TOPK):
            _wait_rows(buf.at[step % 2], xb_ref, tm, sems.at[step % 2])

    def start():
        slot = i % 2
        _to_row_tiles(buf, (slot,), x_ref[...])

        def one(r, carry):
            for k in range(EXPERT_TOPK):
                dst = dest_ref[(i * tm + r) * EXPERT_TOPK + k]
                back_ref[dst] = k * n_tokens + i * tm + r
                pltpu.make_async_copy(buf.at[slot, r], xb_ref.at[dst], sems.at[slot]).start(priority=k)
            return carry
        lax.fori_loop(0, tm, one, 0, unroll=SCATTER_UNROLL)

    _scatter_step(i, pl.num_programs(0), wait_step, start)


def _dispatch(x2, dest, n_rows, tm=256):
    n = x2.shape[0]
    init = jnp.zeros((n_rows, ROW_TILES, 128), F32)
    return pl.pallas_call(
        functools.partial(_dispatch_kernel, tm=tm, n_tokens=n, n_rows=n_rows),
        grid_spec=pltpu.PrefetchScalarGridSpec(
            num_scalar_prefetch=1,
            grid=(n // tm,),
            in_specs=[pl.BlockSpec((tm, D_MODEL), lambda i, d: (i, 0)), pl.BlockSpec(memory_space=pl.ANY)],
            out_specs=[pl.BlockSpec(memory_space=pl.ANY), pl.BlockSpec(memory_space=pltpu.SMEM)],
            scratch_shapes=[pltpu.VMEM((2, tm, ROW_TILES, 128), F32), pltpu.SemaphoreType.DMA((2,))],
        ),
        out_shape=[jax.ShapeDtypeStruct((n_rows, ROW_TILES, 128), F32),
                   jax.ShapeDtypeStruct((n_rows,), jnp.int32)],
        input_output_aliases={2: 0},
        compiler_params=_cparams("arbitrary"),
        name="moe_dispatch",
    )(dest, x2, init)


def _moe_kernel(blk_e_ref, n_valid_ref, row_dst_ref, x_ref, wg_ref, wu_ref, wd_ref, y_ref, xs, ybuf, sems):
    del blk_e_ref
    i = pl.program_id(0)

    def wait_step(step):
        @pl.when(n_valid_ref[step] > 0)
        def _():
            _wait_rows(ybuf.at[step % 2], y_ref, n_valid_ref[step], sems.at[step % 2])

    def start():
        @pl.when(n_valid_ref[i] > 0)
        def _():
            slot = i % 2
            xs[...] = _from_row_tiles(x_ref, BF16)
            _to_row_tiles(ybuf, (slot,), _swiglu_rows(xs[...], wg_ref, wu_ref, wd_ref, (0,)))

            def one(r, carry, priority=0):
                dst = row_dst_ref[i * MOE_ROWS + r]
                pltpu.make_async_copy(ybuf.at[slot, r], y_ref.at[dst], sems.at[slot]).start(priority=priority)
                return carry

            def group(rg, carry):
                for k in range(SCATTER_UNROLL):
                    one(SCATTER_UNROLL * rg + k, carry, priority=k % 2)
                return carry

            n_rows = n_valid_ref[i]
            n_groups = n_rows // SCATTER_UNROLL
            lax.fori_loop(0, n_groups, group, 0)
            lax.fori_loop(n_groups * SCATTER_UNROLL, n_rows, one, 0)

    _scatter_step(i, pl.num_programs(0), wait_step, start)


def _moe_experts(xb, blk_e, n_valid, row_dst, wg, wu, wd, n_out_rows):
    n_blocks = xb.shape[0] // MOE_ROWS
    weight = lambda shape: pl.BlockSpec((1,) + shape, lambda i, be, nv, rd: (be[i], 0, 0))
    return pl.pallas_call(
        _moe_kernel,
        grid_spec=pltpu.PrefetchScalarGridSpec(
            num_scalar_prefetch=3,
            grid=(n_blocks,),
            in_specs=[pl.BlockSpec((MOE_ROWS, ROW_TILES, 128), lambda i, be, nv, rd: (i, 0, 0)),
                      weight((D_MODEL, D_FF)), weight((D_MODEL, D_FF)), weight((D_FF, D_MODEL))],
            out_specs=pl.BlockSpec(memory_space=pl.ANY),
            scratch_shapes=[pltpu.VMEM((MOE_ROWS, D_MODEL), BF16),
                            pltpu.VMEM((2, MOE_ROWS, ROW_TILES, 128), F32),
                            pltpu.SemaphoreType.DMA((2,))],
        ),
        out_shape=jax.ShapeDtypeStruct((n_out_rows, ROW_TILES, 128), F32),
        compiler_params=_cparams("arbitrary"),
        name="moe_experts",
    )(blk_e, n_valid, row_dst, xb, wg, wu, wd)


def _combine_kernel(x_ref, y0_ref, y1_ref, gate_ref, g_ref, b_ref, o_ref):
    gate = gate_ref[...]
    f = gate[:, 0:1] * _from_row_tiles(y0_ref, F32) + gate[:, 1:2] * _from_row_tiles(y1_ref, F32)
    o_ref[...] = _layer_norm(DN_ALPHA * x_ref[...] + f, g_ref[...], b_ref[...])


def _combine(x2, y_slots, gate, ln_g, ln_b, tm=512):
    n = x2.shape[0]
    row = lambda i: (i, 0)
    vec = pl.BlockSpec((1, D_MODEL), lambda i: (0, 0))
    return pl.pallas_call(
        _combine_kernel,
        grid=(n // tm,),
        in_specs=[pl.BlockSpec((tm, D_MODEL), row),
                  pl.BlockSpec((tm, ROW_TILES, 128), lambda i: (i, 0, 0)),
                  pl.BlockSpec((tm, ROW_TILES, 128), lambda i: (n // tm + i, 0, 0)),
                  pl.BlockSpec((tm, EXPERT_TOPK), row), vec, vec],
        out_specs=pl.BlockSpec((tm, D_MODEL), row),
        out_shape=jax.ShapeDtypeStruct((n, D_MODEL), F32),
        compiler_params=_cparams("parallel"),
        name="combine_ln",
    )(x2, y_slots, y_slots, gate, ln_g, ln_b)


def _moe_layer(x2, router_w, wg, wu, wd, ln_g, ln_b):
    n = x2.shape[0]
    nk = n * EXPERT_TOPK
    top_e, gate = _router(x2, router_w)
    top_e = top_e[:, :EXPERT_TOPK]
    gate = gate[:, :EXPERT_TOPK]
    flat_e = top_e.reshape(-1)
    onehot = (flat_e[:, None] == jnp.arange(N_EXPERTS, dtype=jnp.int32)[None, :]).astype(jnp.int32)
    counts = jnp.sum(onehot, axis=0)
    rank = jnp.sum((jnp.cumsum(onehot, axis=0) - onehot) * onehot, axis=1)
    padded = (counts + MOE_ROWS - 1) // MOE_ROWS * MOE_ROWS
    pad_end = jnp.cumsum(padded)
    pad_start = pad_end - padded
    dest = (pad_start[flat_e] + rank).astype(jnp.int32)
    n_blocks = -(-nk // MOE_ROWS) + N_EXPERTS
    n_rows = n_blocks * MOE_ROWS
    blk_start = jnp.arange(n_blocks, dtype=jnp.int32) * MOE_ROWS
    blk_e = jnp.sum((blk_start[:, None] >= pad_end[None, :]).astype(jnp.int32), axis=1)
    valid_end = jnp.concatenate([pad_start + counts, jnp.zeros((1,), jnp.int32)])
    n_valid = jnp.clip(valid_end[blk_e] - blk_start, 0, MOE_ROWS).astype(jnp.int32)
    blk_e = jnp.minimum(blk_e, N_EXPERTS - 1).astype(jnp.int32)
    xb, row_dst = _dispatch(x2, dest, n_rows)
    y_slots = _moe_experts(xb, blk_e, n_valid, row_dst, wg, wu, wd, nk)
    return _combine(x2, y_slots, gate, ln_g, ln_b)


def _rope_tables(t):
    half = HEAD_DIM // 2
    inv = ROPE_THETA ** (-jnp.arange(half, dtype=F32) / half)
    ang = jnp.arange(t).astype(F32)[:, None] * inv[None, :]
    cos = jnp.cos(ang)
    sin = jnp.sin(ang)
    cos_t = jnp.tile(jnp.concatenate([cos, cos], -1), (1, N_HEADS))
    sin_t = jnp.tile(jnp.concatenate([-sin, sin], -1), (1, N_HEADS))
    return cos_t, sin_t


def _pad_rows(w, start):
    out = jnp.zeros((LORA_DIM, BRANCH_DIM), F32)
    return lax.dynamic_update_slice(out, w, (start, 0)).astype(BF16)


def _mixer(x2, b, t, cos_t, sin_t, w_in, conv_w, shift_mu, decay_w0, decay_w2, aaa_a0, aaa_w2, gate_w2,
           k_k, k_a, r_k, gn_g, gn_b, w_branch, w_out, ln_g, ln_b, ffn_weights):
    w_in = w_in.astype(BF16)
    slabs = [w.reshape(-1, w.shape[-1]) for w in ffn_weights]
    n_steps = b * (t // MOBA_BLOCK)
    in_kernel = [s.shape[0] % (n_steps * BF16_ROWS) == 0 for s in slabs]
    o_conv, q3, k3, vt4, kmean, rkv3, lora3, gates, *cast = _in_proj(
        x2.reshape(b, t, D_MODEL), w_in, conv_w, cos_t, sin_t,
        tuple(s for s, ok in zip(slabs, in_kernel) if ok))
    cast = iter(cast)
    ffn_bf16 = [(next(cast) if ok else s.astype(BF16)).reshape(w.shape)
                for s, ok, w in zip(slabs, in_kernel, ffn_weights)]
    o_att = _moba(q3, k3, vt4, kmean.reshape(b, t // MOBA_BLOCK, BRANCH_DIM)).reshape(b * t, BRANCH_DIM)

    row = lambda a: a.reshape(1, -1)
    p = {
        "mu_r": row(shift_mu[0:BRANCH_DIM]), "mu_k": row(shift_mu[BRANCH_DIM:2 * BRANCH_DIM]),
        "mu_v": row(shift_mu[2 * BRANCH_DIM:3 * BRANCH_DIM]), "mu_l": row(shift_mu[3 * BRANCH_DIM:]),
        "w0": row(decay_w0), "w2": _pad_rows(decay_w2, 0),
        "a0": row(aaa_a0), "a2": _pad_rows(aaa_w2, DECAY_LORA),
        "g2": _pad_rows(gate_w2, DECAY_LORA + AAA_LORA),
        "k_k": row(k_k), "k_a": row(k_a), "r_k": row(r_k), "gn_g": row(gn_g), "gn_b": row(gn_b),
    }
    o_rwkv = _rwkv(rkv3, lora3, p)

    x2 = _merge(o_conv.reshape(b * t, BRANCH_DIM), o_att, o_rwkv.reshape(b * t, BRANCH_DIM),
                gates.reshape(b * t, GATE_COLS), x2,
                w_branch.astype(BF16), w_out.astype(BF16), row(ln_g), row(ln_b))
    return x2, ffn_bf16


def kernel(x, w_in, conv_w, shift_mu, decay_w0, decay_w2, aaa_a0, aaa_w2, gate_w2, k_k, k_a, r_k, gn_g, gn_b,
           w_branch, w_out, ln1_g, ln1_b, ln2_g, ln2_b, ffn_w_gate, ffn_w_up, ffn_w_down, router_w,
           moe_w_gate, moe_w_up, moe_w_down):
    b, t, d = x.shape
    x2 = x.reshape(b * t, d)
    cos_t, sin_t = _rope_tables(t)
    row = lambda a: a.reshape(1, -1)
    for l in range(DEPTH):
        j = l // 2
        dense = l % 2 == 0
        ffn_weights = ((ffn_w_gate[j], ffn_w_up[j], ffn_w_down[j]) if dense
                       else (moe_w_gate[j], moe_w_up[j], moe_w_down[j]))
        x2, (wg, wu, wd) = _mixer(x2, b, t, cos_t, sin_t, w_in[l], conv_w[l], shift_mu[l], decay_w0[l],
                                  decay_w2[l], aaa_a0[l], aaa_w2[l], gate_w2[l], k_k[l], k_a[l], r_k[l],
                                  gn_g[l], gn_b[l], w_branch[l], w_out[l], ln1_g[l], ln1_b[l], ffn_weights)
        if dense:
            x2 = _ffn(x2, wg, wu, wd, row(ln2_g[l]), row(ln2_b[l]))
        else:
            x2 = _moe_layer(x2, router_w[j], wg, wu, wd, row(ln2_g[l]), row(ln2_b[l]))
    return x2.reshape(b, t, d)
```

```python
import functools

import jax
import jax.numpy as jnp
from jax import lax
from jax.experimental import pallas as pl
from jax.experimental.pallas import tpu as pltpu

F32 = jnp.float32
BF16 = jnp.bfloat16

D_MODEL = 1024
HEAD_DIM = 64
BRANCH_DIM = 512
N_HEADS = BRANCH_DIM // HEAD_DIM
DECAY_LORA = 64
AAA_LORA = 64
GATE_LORA = 128
LORA_DIM = DECAY_LORA + AAA_LORA + GATE_LORA
MIX_COLS = 9 * BRANCH_DIM
GATE_COLS = 3 * D_MODEL
MOBA_BLOCK = 256
MOBA_TOPK = 3
ROPE_THETA = 10000.0
D_FF = 2816
N_EXPERTS = 8
EXPERT_TOPK = 2
MOE_ROWS = 256
LN_EPS = 1e-5
GN_EPS = 64e-5
DEPTH = 2
DN_ALPHA = (2 * DEPTH) ** 0.25
LOG2_E = 1.4426950408889634

VMEM_LIMIT_BYTES = 56 * 1024 * 1024
SUBLANES = 8
BF16_ROWS = 16
RWKV_TILE = 256
RWKV_CHUNK = 64
FF_CHUNK = 256
SCATTER_UNROLL = 8
PAIR_WIDTH = 2 * HEAD_DIM
HEAD_SHIFT = HEAD_DIM.bit_length() - 1
CHUNK_SHIFT = RWKV_CHUNK.bit_length() - 1


def _cparams(*sem):
    return pltpu.CompilerParams(dimension_semantics=sem, vmem_limit_bytes=VMEM_LIMIT_BYTES)


def _vmem_whole():
    return pl.BlockSpec(memory_space=pltpu.VMEM)


def _dot(a, b):
    return jnp.dot(a.astype(BF16), b.astype(BF16), preferred_element_type=F32)


def _dot_nt(a, b):
    return lax.dot_general(a.astype(BF16), b.astype(BF16), (((1,), (1,)), ((), ())),
                           preferred_element_type=F32)


def _dot_tn(a, b):
    return lax.dot_general(a.astype(BF16), b.astype(BF16), (((0,), (0,)), ((), ())),
                           preferred_element_type=F32)


def _split_bf16(x):
    hi = x.astype(BF16)
    return hi, (x - hi.astype(F32)).astype(BF16)


def _dot_split(a, b, *, data_on_left):
    hi, lo = _split_bf16(a if data_on_left else b)
    if data_on_left:
        return jnp.dot(hi, b, preferred_element_type=F32) + jnp.dot(lo, b, preferred_element_type=F32)
    return jnp.dot(a, hi, preferred_element_type=F32) + jnp.dot(a, lo, preferred_element_type=F32)


def _sigmoid(x):
    return 1.0 / (1.0 + jnp.exp(-x))


def _layer_norm(y, g, b):
    mu = jnp.mean(y, axis=-1, keepdims=True)
    d = y - mu
    var = jnp.mean(d * d, axis=-1, keepdims=True)
    return d * lax.rsqrt(var + LN_EPS) * g + b


def _shift_rows(u, halo, n):
    out = pltpu.roll(u, n, axis=0)
    row = lax.broadcasted_iota(jnp.int32, u.shape, 0)
    last = halo.shape[0]
    for r in range(n):
        src = halo[last - n + r:last - n + r + 1, :]
        out = jnp.where(row == r, src, out)
    return out


def _halo_index(tile_rows, halo_rows):
    step = tile_rows // halo_rows

    def index(col):
        return lambda b, i: (b, jnp.maximum(i * step - 1, 0), col)
    return index


def _in_proj_kernel(*refs, n_side):
    x_ref, wm_ref, convw_ref, cos_ref, sin_ref = refs[:5]
    side_in = refs[5:5 + n_side]
    conv_ref, q_ref, k_ref, vt_ref, km_ref, rkv_ref, lora_ref, gate_ref = refs[5 + n_side:13 + n_side]
    side_out = refs[13 + n_side:13 + 2 * n_side]
    halo_ref = refs[13 + 2 * n_side]
    for src, dst in zip(side_in, side_out):
        dst[...] = src[...].astype(dst.dtype)

    first = pl.program_id(1) == 0
    bd = BRANCH_DIM
    xb = x_ref[0].astype(BF16)
    proj = lambda g: jnp.dot(xb, wm_ref[:, g * bd:(g + 1) * bd], preferred_element_type=F32)

    u = proj(2) * proj(0)
    halo = jnp.where(first, 0.0, halo_ref[...])
    w = convw_ref[...]
    y = w[2:3, :] * u + w[1:2, :] * _shift_rows(u, halo, 1) + w[0:1, :] * _shift_rows(u, halo, 2)
    conv_ref[0] = (proj(1) * y).astype(conv_ref.dtype)
    halo_ref[...] = u[u.shape[0] - SUBLANES:, :]

    cos = cos_ref[...]
    sin = sin_ref[...]
    lane = lax.broadcasted_iota(jnp.int32, cos.shape, 1)
    first_half = (lane & (HEAD_DIM - 1)) < (HEAD_DIM // 2)

    def rope(t):
        swapped = jnp.where(first_half, pltpu.roll(t, bd - HEAD_DIM // 2, axis=1),
                            pltpu.roll(t, HEAD_DIM // 2, axis=1))
        return t * cos + swapped * sin

    kr = rope(proj(4))
    q_ref[0] = rope(proj(3))
    k_ref[0] = kr.astype(BF16)
    km_ref[0, 0] = jnp.mean(kr, axis=0, keepdims=True)
    vt_ref[0, 0] = proj(5).astype(BF16).T

    for g in range(3):
        rkv_ref[0, :, g * bd:(g + 1) * bd] = proj(6 + g).astype(rkv_ref.dtype)
    lora_ref[0] = jnp.dot(xb, wm_ref[:, MIX_COLS:MIX_COLS + LORA_DIM], preferred_element_type=F32)
    for j in range(0, GATE_COLS, bd):
        col = MIX_COLS + LORA_DIM + j
        g = jnp.dot(xb, wm_ref[:, col:col + bd], preferred_element_type=F32)
        gate_ref[0, :, j:j + bd] = _sigmoid(g).astype(gate_ref.dtype)


def _in_proj(x3, w_in, conv_w, cos_t, sin_t, side_casts=()):
    b, t, _ = x3.shape
    tm = MOBA_BLOCK
    nb = t // tm
    blk = lambda width: pl.BlockSpec((1, tm, width), lambda bi, i: (bi, i, 0))
    tab = pl.BlockSpec((tm, BRANCH_DIM), lambda bi, i: (i, 0))
    shape3 = lambda width, dtype: jax.ShapeDtypeStruct((b, t, width), dtype)
    for a in side_casts:
        assert a.shape[0] % (b * nb * BF16_ROWS) == 0, a.shape
    slab = lambda a: pl.BlockSpec((a.shape[0] // (b * nb), a.shape[1]), lambda bi, i: (bi * nb + i, 0))
    return pl.pallas_call(
        functools.partial(_in_proj_kernel, n_side=len(side_casts)),
        grid=(b, nb),
        in_specs=[blk(D_MODEL), _vmem_whole(), pl.BlockSpec((3, BRANCH_DIM), lambda bi, i: (0, 0)), tab, tab]
                 + [slab(a) for a in side_casts],
        out_specs=[blk(BRANCH_DIM), blk(BRANCH_DIM), blk(BRANCH_DIM),
                   pl.BlockSpec((1, 1, BRANCH_DIM, tm), lambda bi, i: (bi, i, 0, 0)),
                   pl.BlockSpec((1, 1, 1, BRANCH_DIM), lambda bi, i: (bi, i, 0, 0)),
                   blk(3 * BRANCH_DIM), blk(LORA_DIM), blk(GATE_COLS)] + [slab(a) for a in side_casts],
        out_shape=[shape3(BRANCH_DIM, BF16), shape3(BRANCH_DIM, F32), shape3(BRANCH_DIM, BF16),
                   jax.ShapeDtypeStruct((b, nb, BRANCH_DIM, tm), BF16),
                   jax.ShapeDtypeStruct((b, nb, 1, BRANCH_DIM), F32),
                   shape3(3 * BRANCH_DIM, BF16), shape3(LORA_DIM, F32), shape3(GATE_COLS, BF16)]
                  + [jax.ShapeDtypeStruct(a.shape, BF16) for a in side_casts],
        scratch_shapes=[pltpu.VMEM((SUBLANES, BRANCH_DIM), F32)],
        compiler_params=_cparams("parallel", "arbitrary"),
        name="in_proj",
    )(x3, w_in, conv_w, cos_t, sin_t, *side_casts)


def _moba_kernel(q_ref, k_ref, vt_ref, km_ref, o_ref, *, nb, heads):
    i = pl.program_id(2)
    bs = MOBA_BLOCK
    width = heads * HEAD_DIM
    q = q_ref[0]
    km = km_ref[0]
    head_of_lane = lax.broadcasted_iota(jnp.int32, (1, width), 1) >> HEAD_SHIFT
    q_heads = jnp.concatenate([jnp.where(head_of_lane == g, q, 0.0) for g in range(heads)], axis=0)
    q_all = (q_heads * (HEAD_DIM ** -0.5 * LOG2_E)).astype(BF16)

    blk = lax.broadcasted_iota(jnp.int32, (nb, heads * bs), 0)
    valid = blk < i
    km_heads = jnp.concatenate([jnp.where(head_of_lane == g, km, 0.0) for g in range(heads)], axis=0)
    km_hi, km_lo = _split_bf16(km_heads)
    q_hi, q_lo = _split_bf16(q)
    hi_part = _dot_nt(jnp.concatenate([km_hi, km_lo], axis=0), q_hi)
    gate = hi_part[:heads * nb] + hi_part[heads * nb:] + _dot_nt(km_hi, q_lo)
    gate = jnp.concatenate([gate[g * nb:(g + 1) * nb] for g in range(heads)], axis=1)
    gate = jnp.where(valid, gate, -jnp.inf)
    rank = jnp.zeros((nb, heads * bs), jnp.int32)
    for m in range(nb):
        gm = gate[m:m + 1, :]
        beats = (gm > gate) | ((gm == gate) & (blk > m))
        rank = rank + beats.astype(jnp.int32)
    chosen = jnp.where(valid & (rank < MOBA_TOPK), (1 << blk).astype(F32), 0.0)
    picks = jnp.sum(chosen, axis=0, keepdims=True).astype(jnp.int32)

    ones_rows = jnp.ones((SUBLANES, bs), BF16)

    def scores(j, n_blocks):
        kb = k_ref[0, pl.ds(pl.multiple_of(j * bs, bs), n_blocks * bs), :]
        return lax.dot_general(kb, q_all, (((1,), (1,)), ((), ())), preferred_element_type=F32)

    def block_update(j, s, chosen, causal, carry):
        n_blocks = len(chosen)
        m, accs = carry
        if causal is not None:
            s = jnp.where(causal, s, -jnp.inf)
        parts = [s[n * bs:(n + 1) * bs] for n in range(n_blocks)]
        tops = [jnp.max(part, axis=0, keepdims=True) for part in parts]
        tops = [t if c is None else jnp.where(c, t, -jnp.inf) for t, c in zip(tops, chosen)]
        m_new = functools.reduce(jnp.maximum, tops, m)
        alpha = jnp.exp2(m - m_new)
        shifts = [m_new if c is None else jnp.where(c, m_new, jnp.inf) for c in chosen]
        ps = [jnp.exp2(part - shift).astype(BF16) for part, shift in zip(parts, shifts)]
        new_acc = []
        for g in range(heads):
            cols = slice(g * bs, (g + 1) * bs)
            acc = alpha[:, cols] * accs[g]
            for n in range(n_blocks):
                vt = jnp.concatenate([vt_ref[0, j + n, g * HEAD_DIM:(g + 1) * HEAD_DIM, :], ones_rows], axis=0)
                acc = acc + jnp.dot(vt, ps[n][:, cols], preferred_element_type=F32)
            new_acc.append(acc)
        return m_new, tuple(new_acc)

    key_id = lax.broadcasted_iota(jnp.int32, (bs, heads * bs), 0)
    query_id = lax.broadcasted_iota(jnp.int32, (bs, heads * bs), 1) & (bs - 1)
    init = (jnp.full((1, heads * bs), -jnp.inf, F32),
            tuple(jnp.zeros((HEAD_DIM + SUBLANES, bs), F32) for _ in range(heads)))
    carry = block_update(i, scores(i, 1), [None], key_id <= query_id, init)

    def pair_update(j, s, carry):
        bits = picks >> j
        return block_update(j, s, [(bits & 1) == 1, ((bits >> 1) & 1) == 1], None, carry)

    def two_pairs(jq, carry):
        j = 4 * jq
        s_a, s_b = scores(j, 2), scores(j + 2, 2)
        return pair_update(j + 2, s_b, pair_update(j, s_a, carry))

    carry = lax.fori_loop(0, i // 4, two_pairs, carry)
    done = (i // 4) * 4
    _, accs = lax.fori_loop(0, (i - done + 1) // 2,
                            lambda jj, c: pair_update(done + 2 * jj, scores(done + 2 * jj, 2), c), carry)
    out_t = jnp.concatenate([acc[:HEAD_DIM] / acc[HEAD_DIM:HEAD_DIM + 1] for acc in accs], axis=0)
    o_ref[0] = out_t.T.astype(o_ref.dtype)


def _moba(q3, k3, vt4, kmean, heads=4):
    b, t, _ = q3.shape
    nb = t // MOBA_BLOCK
    width = heads * HEAD_DIM
    qblk = pl.BlockSpec((1, MOBA_BLOCK, width), lambda bi, hi, i: (bi, i, hi))
    return pl.pallas_call(
        functools.partial(_moba_kernel, nb=nb, heads=heads),
        grid=(b, N_HEADS // heads, nb),
        in_specs=[qblk, pl.BlockSpec((1, t, width), lambda bi, hi, i: (bi, 0, hi)),
                  pl.BlockSpec((1, nb, width, MOBA_BLOCK), lambda bi, hi, i: (bi, 0, hi, 0)),
                  pl.BlockSpec((1, nb, width), lambda bi, hi, i: (bi, 0, hi))],
        out_specs=qblk,
        out_shape=jax.ShapeDtypeStruct((b, t, BRANCH_DIM), BF16),
        compiler_params=_cparams("parallel", "parallel", "arbitrary"),
        name="moba",
    )(q3, k3, vt4, kmean)


def _rwkv_kernel(r_ref, k_ref, v_ref, lo_ref, hr_ref, hk_ref, hv_ref, hl_ref,
                 mu_r_ref, mu_k_ref, mu_v_ref, mu_l_ref, w0_ref, w2_ref, a0_ref, a2_ref, g2_ref,
                 kk_ref, ka_ref, rk_ref, gng_ref, gnb_ref, o_ref, state_ref):
    i = pl.program_id(1)
    first = i == 0
    tq, ch = RWKV_TILE, RWKV_CHUNK
    hd, bd = HEAD_DIM, BRANCH_DIM

    @pl.when(first)
    def _():
        state_ref[...] = jnp.zeros_like(state_ref)

    def lerp(x_ref, h_ref, mu_ref):
        x = x_ref[0].astype(F32)
        halo = jnp.where(first, 0.0, h_ref[0].astype(F32))
        return x + (_shift_rows(x, halo, 1) - x) * mu_ref[...]

    r = lerp(r_ref, hr_ref, mu_r_ref)
    k = lerp(k_ref, hk_ref, mu_k_ref)
    v = lerp(v_ref, hv_ref, mu_v_ref)
    lo = lerp(lo_ref, hl_ref, mu_l_ref)

    z = w0_ref[...] + _dot(jnp.tanh(lo), w2_ref[...])
    ew = jnp.exp(-0.5) * _sigmoid(z)
    a = _sigmoid(a0_ref[...] + _dot(lo, a2_ref[...]))
    g = _dot(_sigmoid(lo), g2_ref[...])

    half = bd // 2
    lane_r = lax.broadcasted_iota(jnp.int32, (half, half), 0)
    lane_c = lax.broadcasted_iota(jnp.int32, (half, half), 1)
    ones_blocks = jnp.where((lane_r >> HEAD_SHIFT) == (lane_c >> HEAD_SHIFT), 1.0, 0.0).astype(BF16)

    def head_sum(x):
        return jnp.concatenate([_dot_split(x[:, :half], ones_blocks, data_on_left=True),
                                _dot_split(x[:, half:], ones_blocks, data_on_left=True)], axis=1)

    kk = k * kk_ref[...]
    kk = kk / jnp.maximum(jnp.sqrt(head_sum(kk * kk)), 1e-12)
    k2 = k * (1.0 + (a - 1.0) * ka_ref[...])
    bonus = head_sum(r * k2 * rk_ref[...]) * v

    t_r = lax.broadcasted_iota(jnp.int32, (tq, tq), 0)
    t_c = lax.broadcasted_iota(jnp.int32, (tq, tq), 1)
    same_chunk = (t_r >> CHUNK_SHIFT) == (t_c >> CHUNK_SHIFT)
    cs = _dot_split(jnp.where(same_chunk & (t_c <= t_r), 1.0, 0.0).astype(BF16), ew, data_on_left=False)
    cs_end = jnp.concatenate(
        [jnp.broadcast_to(cs[c * ch + ch - 1:c * ch + ch, :], (ch, bd)) for c in range(tq // ch)], axis=0)

    kka = kk * a
    a_t = -kk * jnp.exp(ew - cs)
    b_hat = (kka * jnp.exp(cs)).astype(BF16)
    k_hat = (k2 * jnp.exp(cs)).astype(BF16)
    r_t = r * jnp.exp(-cs)
    b_e = (kka * jnp.exp(cs - cs_end)).astype(BF16)
    k_e = (k2 * jnp.exp(cs - cs_end)).astype(BF16)
    p_c = jnp.exp(-cs_end)
    v_b = v.astype(BF16)

    strict = jnp.where(same_chunk & (t_c < t_r), 1.0, 0.0)
    incl = jnp.where(same_chunk & (t_c <= t_r), 1.0, 0.0)
    eye = jnp.where(t_c == t_r, 1.0, 0.0)
    pw = PAIR_WIDTH
    second = (lax.broadcasted_iota(jnp.int32, (1, pw), 1) >> HEAD_SHIFT) == 1
    same_head = ((lax.broadcasted_iota(jnp.int32, (pw, pw), 0) >> HEAD_SHIFT)
                 == (lax.broadcasted_iota(jnp.int32, (pw, pw), 1) >> HEAD_SHIFT))

    n_pairs = N_HEADS // 2
    pair = lambda x, p: x[:, p * pw:(p + 1) * pw]
    head_list = [(p, hm) for p in range(n_pairs) for hm in (~second, second)]
    a_hs = [jnp.where(hm, pair(a_t, p), 0.0).astype(BF16) for p, hm in head_list]
    l_abs = [strict * _dot_nt(a_h, pair(b_hat, p)) for a_h, (p, _) in zip(a_hs, head_list)]
    t_invs = [eye + x for x in l_abs]
    powers = l_abs
    n = 2
    while n < ch:
        powers = [_dot(x, x) for x in powers]
        t_invs = [t + _dot(t, x) for t, x in zip(t_invs, powers)]
        n *= 2
    l_aks = [strict * _dot_nt(a_h, pair(k_hat, p)) for a_h, (p, _) in zip(a_hs, head_list)]
    lakvs = [_dot(l_ak, pair(v_b, p)) for l_ak, (p, _) in zip(l_aks, head_list)]
    sols = [_dot(t, jnp.concatenate([pair(a_t, p).astype(BF16), lakv.astype(BF16)], axis=1))
            for t, lakv, (p, _) in zip(t_invs, lakvs, head_list)]
    r_hs = [jnp.where(hm, pair(r_t, p), 0.0).astype(BF16) for p, hm in head_list]
    m_rbs = [incl * _dot_nt(r_h, pair(b_hat, p)) for r_h, (p, _) in zip(r_hs, head_list)]
    m_rks = [incl * _dot_nt(r_h, pair(k_hat, p)) for r_h, (p, _) in zip(r_hs, head_list)]
    apps = [_dot(m_rb, sol) for m_rb, sol in zip(m_rbs, sols)]
    mrkvs = [_dot(m_rk, pair(v_b, p)) for m_rk, (p, _) in zip(m_rks, head_list)]

    def both(xs, p, cols):
        return jnp.where(second, xs[2 * p + 1][:, cols], xs[2 * p][:, cols])

    left, right = slice(0, pw), slice(pw, 2 * pw)
    a_til = [both(sols, p, left).astype(BF16) for p in range(n_pairs)]
    u_0 = [both(sols, p, right).astype(BF16) for p in range(n_pairs)]
    r_til = [(both(apps, p, left) + pair(r_t, p)).astype(BF16) for p in range(n_pairs)]
    y_hat = [both(apps, p, right) + both(mrkvs, p, left) for p in range(n_pairs)]

    states = [state_ref[p] for p in range(n_pairs)]
    y_rows = []
    for c in range(tq // ch):
        rows = slice(c * ch, (c + 1) * ch)
        g_cs = [jnp.where(same_head, _dot_tn(a_til[p][rows], pair(b_e, p)[rows]), 0.0) for p in range(n_pairs)]
        h_cs = [jnp.where(same_head, _dot_tn(u_0[p][rows], pair(b_e, p)[rows])
                          + _dot_tn(pair(v_b, p)[rows], pair(k_e, p)[rows]), 0.0) for p in range(n_pairs)]
        y_rows.append(jnp.concatenate(
            [_dot_nt(r_til[p][rows], states[p]) + y_hat[p][rows] for p in range(n_pairs)], axis=1))
        states = [states[p] * pair(p_c, p)[c * ch:c * ch + 1] + _dot(states[p], g_cs[p]) + h_cs[p]
                  for p in range(n_pairs)]
    for p in range(n_pairs):
        state_ref[p] = states[p]
    y = jnp.concatenate(y_rows, axis=0)

    mu = head_sum(y) * (1.0 / hd)
    d = y - mu
    var = head_sum(d * d) * (1.0 / hd)
    y = d * lax.rsqrt(var + GN_EPS) * gng_ref[...] + gnb_ref[...]
    o_ref[0] = ((y + bonus) * g).astype(o_ref.dtype)


def _rwkv(rkv3, lora3, p):
    b, t, _ = rkv3.shape
    tq = RWKV_TILE
    halo = _halo_index(tq, SUBLANES)
    halo_bf16 = _halo_index(tq, BF16_ROWS)
    blk = lambda col: pl.BlockSpec((1, tq, BRANCH_DIM), lambda bi, i: (bi, i, col))
    hblk = lambda col: pl.BlockSpec((1, BF16_ROWS, BRANCH_DIM), halo_bf16(col))
    vec = lambda width: pl.BlockSpec((1, width), lambda bi, i: (0, 0))
    mat = lambda rows: pl.BlockSpec((rows, BRANCH_DIM), lambda bi, i: (0, 0))
    return pl.pallas_call(
        _rwkv_kernel,
        grid=(b, t // tq),
        in_specs=[blk(0), blk(1), blk(2), pl.BlockSpec((1, tq, LORA_DIM), lambda bi, i: (bi, i, 0)),
                  hblk(0), hblk(1), hblk(2), pl.BlockSpec((1, SUBLANES, LORA_DIM), halo(0)),
                  vec(BRANCH_DIM), vec(BRANCH_DIM), vec(BRANCH_DIM), vec(LORA_DIM),
                  vec(BRANCH_DIM), mat(LORA_DIM), vec(BRANCH_DIM), mat(LORA_DIM), mat(LORA_DIM),
                  vec(BRANCH_DIM), vec(BRANCH_DIM), vec(BRANCH_DIM), vec(BRANCH_DIM), vec(BRANCH_DIM)],
        out_specs=pl.BlockSpec((1, tq, BRANCH_DIM), lambda bi, i: (bi, i, 0)),
        out_shape=jax.ShapeDtypeStruct((b, t, BRANCH_DIM), BF16),
        scratch_shapes=[pltpu.VMEM((N_HEADS // 2, PAIR_WIDTH, PAIR_WIDTH), F32)],
        compiler_params=_cparams("parallel", "arbitrary"),
        name="rwkv7",
    )(rkv3, rkv3, rkv3, lora3, rkv3, rkv3, rkv3, lora3,
      p["mu_r"], p["mu_k"], p["mu_v"], p["mu_l"], p["w0"], p["w2"], p["a0"], p["a2"], p["g2"],
      p["k_k"], p["k_a"], p["r_k"], p["gn_g"], p["gn_b"])


def _merge_kernel(oc_ref, oa_ref, or_ref, gate_ref, x_ref, wb_ref, wo_ref, g_ref, b_ref, o_ref):
    d = D_MODEL
    bd = BRANCH_DIM
    merged = (gate_ref[:, 0:d] * jnp.dot(oc_ref[...], wb_ref[0:bd, :], preferred_element_type=F32)
              + gate_ref[:, d:2 * d] * jnp.dot(oa_ref[...], wb_ref[bd:2 * bd, :], preferred_element_type=F32)
              + gate_ref[:, 2 * d:3 * d] * jnp.dot(or_ref[...], wb_ref[2 * bd:3 * bd, :],
                                                   preferred_element_type=F32))
    h = jnp.dot(merged.astype(BF16), wo_ref[...], preferred_element_type=F32)
    o_ref[...] = _layer_norm(DN_ALPHA * x_ref[...] + h, g_ref[...], b_ref[...])


def _merge(o_conv, o_att, o_rwkv, gates, x2, w_branch, w_out, ln_g, ln_b, tm=512):
    n = x2.shape[0]
    row = lambda i: (i, 0)
    vec = pl.BlockSpec((1, D_MODEL), lambda i: (0, 0))
    return pl.pallas_call(
        _merge_kernel,
        grid=(n // tm,),
        in_specs=[pl.BlockSpec((tm, BRANCH_DIM), row)] * 3
                 + [pl.BlockSpec((tm, GATE_COLS), row), pl.BlockSpec((tm, D_MODEL), row),
                    _vmem_whole(), _vmem_whole(), vec, vec],
        out_specs=pl.BlockSpec((tm, D_MODEL), row),
        out_shape=jax.ShapeDtypeStruct((n, D_MODEL), F32),
        compiler_params=_cparams("parallel"),
        name="merge_ln",
    )(o_conv, o_att, o_rwkv, gates, x2, w_branch, w_out, ln_g, ln_b)


def _swiglu_rows(xb, wg_ref, wu_ref, wd_ref, lead):
    hidden = []
    for c in range(0, D_FF, FF_CHUNK):
        cols = slice(c, c + FF_CHUNK)
        hg = jnp.dot(xb, wg_ref[lead + (slice(None), cols)], preferred_element_type=F32)
        hu = jnp.dot(xb, wu_ref[lead + (slice(None), cols)], preferred_element_type=F32)
        hidden.append((hg * _sigmoid(hg) * hu).astype(BF16))
    return jnp.dot(jnp.concatenate(hidden, axis=1), wd_ref[lead + (slice(None), slice(None))],
                   preferred_element_type=F32)


def _ffn_kernel(x_ref, wg_ref, wu_ref, wd_ref, g_ref, b_ref, side_in, o_ref, side_out):
    side_out[...] = side_in[...].astype(side_out.dtype)
    x = x_ref[...]
    f = _swiglu_rows(x.astype(BF16), wg_ref, wu_ref, wd_ref, ())
    o_ref[...] = _layer_norm(DN_ALPHA * x + f, g_ref[...], b_ref[...])


def _ffn(x2, wg, wu, wd, ln_g, ln_b, side_cast, tm=512):
    n = x2.shape[0]
    n_steps = n // tm
    assert side_cast.shape[0] % (n_steps * BF16_ROWS) == 0, side_cast.shape
    row = lambda i: (i, 0)
    vec = pl.BlockSpec((1, D_MODEL), lambda i: (0, 0))
    slab = pl.BlockSpec((side_cast.shape[0] // n_steps, side_cast.shape[1]), row)
    return pl.pallas_call(
        _ffn_kernel,
        grid=(n_steps,),
        in_specs=[pl.BlockSpec((tm, D_MODEL), row), _vmem_whole(), _vmem_whole(), _vmem_whole(), vec, vec, slab],
        out_specs=[pl.BlockSpec((tm, D_MODEL), row), slab],
        out_shape=[jax.ShapeDtypeStruct((n, D_MODEL), F32), jax.ShapeDtypeStruct(side_cast.shape, BF16)],
        compiler_params=_cparams("parallel"),
        name="ffn_ln",
    )(x2, wg, wu, wd, ln_g, ln_b, side_cast)


def _router_kernel(x_ref, w_ref, e_ref, g_ref):
    x_hi, x_lo = _split_bf16(x_ref[...])
    w_hi, w_lo = _split_bf16(w_ref[...])
    tm = x_hi.shape[0]
    hi_part = _dot(jnp.concatenate([x_hi, x_lo], axis=0), w_hi)
    logits = hi_part[:tm] + hi_part[tm:] + _dot(x_hi, w_lo)
    lane = lax.broadcasted_iota(jnp.int32, logits.shape, 1).astype(F32)
    m1 = jnp.max(logits, axis=1, keepdims=True)
    e1 = jnp.min(jnp.where(logits == m1, lane, float(N_EXPERTS)), axis=1, keepdims=True)
    rest = jnp.where(lane == e1, -jnp.inf, logits)
    m2 = jnp.max(rest, axis=1, keepdims=True)
    e2 = jnp.min(jnp.where(rest == m2, lane, float(N_EXPERTS)), axis=1, keepdims=True)
    w2 = jnp.exp(m2 - m1)
    denom = 1.0 + w2
    e_ref[...] = jnp.where(lane == 0.0, e1, e2).astype(jnp.int32)
    g_ref[...] = jnp.where(lane == 0.0, 1.0 / denom, w2 / denom)


def _router(x2, router_w, tm=512):
    n = x2.shape[0]
    row = lambda i: (i, 0)
    return pl.pallas_call(
        _router_kernel,
        grid=(n // tm,),
        in_specs=[pl.BlockSpec((tm, D_MODEL), row), pl.BlockSpec((D_MODEL, N_EXPERTS), lambda i: (0, 0))],
        out_specs=[pl.BlockSpec((tm, N_EXPERTS), row), pl.BlockSpec((tm, N_EXPERTS), row)],
        out_shape=[jax.ShapeDtypeStruct((n, N_EXPERTS), jnp.int32),
                   jax.ShapeDtypeStruct((n, N_EXPERTS), F32)],
        compiler_params=_cparams("parallel"),
        name="router",
    )(x2, router_w)


ROW_TILES = D_MODEL // 128


def _to_row_tiles(dst_ref, lead, x):
    for c in range(ROW_TILES):
        dst_ref[lead + (slice(None), c, slice(None))] = x[:, c * 128:(c + 1) * 128]


def _from_row_tiles(src_ref, dtype):
    return jnp.concatenate([src_ref[:, c, :].astype(dtype) for c in range(ROW_TILES)], axis=1)


def _scatter_step(step, n_steps, wait_step, start_this_step):
    @pl.when(step >= 2)
    def _():
        wait_step(step - 2)

    start_this_step()

    @pl.when(step == n_steps - 1)
    def _():
        @pl.when(step >= 1)
        def _():
            wait_step(step - 1)

        wait_step(step)


def _wait_rows(buf_rows, hbm_ref, n_rows, sem):
    pltpu.make_async_copy(buf_rows.at[pl.ds(0, n_rows)], hbm_ref.at[pl.ds(0, n_rows)], sem).wait()


def _dispatch_kernel(dest_ref, x_ref, init_ref, xb_ref, back_ref, buf, sems, *, tm, n_tokens, n_rows):
    del init_ref
    i = pl.program_id(0)

    @pl.when(i == 0)
    def _():
        def clear(r, carry):
            back_ref[r] = 0
            return carry
        lax.fori_loop(0, n_rows, clear, 0, unroll=8)

    def wait_step(step):
        for _ in range(EXPERT_TOPK):
            _wait_rows(buf.at[step % 2], xb_ref, tm, sems.at[step % 2])

    def start():
        slot = i % 2
        _to_row_tiles(buf, (slot,), x_ref[...])

        def one(r, carry):
            for k in range(EXPERT_TOPK):
                dst = dest_ref[(i * tm + r) * EXPERT_TOPK + k]
                back_ref[dst] = k * n_tokens + i * tm + r
                pltpu.make_async_copy(buf.at[slot, r], xb_ref.at[dst], sems.at[slot]).start(priority=k)
            return carry
        lax.fori_loop(0, tm, one, 0, unroll=SCATTER_UNROLL)

    _scatter_step(i, pl.num_programs(0), wait_step, start)


def _dispatch(x2, dest, n_rows, tm=256):
    n = x2.shape[0]
    init = jnp.zeros((n_rows, ROW_TILES, 128), F32)
    return pl.pallas_call(
        functools.partial(_dispatch_kernel, tm=tm, n_tokens=n, n_rows=n_rows),
        grid_spec=pltpu.PrefetchScalarGridSpec(
            num_scalar_prefetch=1,
            grid=(n // tm,),
            in_specs=[pl.BlockSpec((tm, D_MODEL), lambda i, d: (i, 0)), pl.BlockSpec(memory_space=pl.ANY)],
            out_specs=[pl.BlockSpec(memory_space=pl.ANY), pl.BlockSpec(memory_space=pltpu.SMEM)],
            scratch_shapes=[pltpu.VMEM((2, tm, ROW_TILES, 128), F32), pltpu.SemaphoreType.DMA((2,))],
        ),
        out_shape=[jax.ShapeDtypeStruct((n_rows, ROW_TILES, 128), F32),
                   jax.ShapeDtypeStruct((n_rows,), jnp.int32)],
        input_output_aliases={2: 0},
        compiler_params=_cparams("arbitrary"),
        name="moe_dispatch",
    )(dest, x2, init)


def _moe_kernel(blk_e_ref, n_valid_ref, row_dst_ref, x_ref, wg_ref, wu_ref, wd_ref, y_ref, xs, ybuf, sems):
    del blk_e_ref
    i = pl.program_id(0)

    def wait_step(step):
        @pl.when(n_valid_ref[step] > 0)
        def _():
            _wait_rows(ybuf.at[step % 2], y_ref, n_valid_ref[step], sems.at[step % 2])

    def start():
        @pl.when(n_valid_ref[i] > 0)
        def _():
            slot = i % 2
            xs[...] = _from_row_tiles(x_ref, BF16)
            _to_row_tiles(ybuf, (slot,), _swiglu_rows(xs[...], wg_ref, wu_ref, wd_ref, (0,)))

            def one(r, carry, priority=0):
                dst = row_dst_ref[i * MOE_ROWS + r]
                pltpu.make_async_copy(ybuf.at[slot, r], y_ref.at[dst], sems.at[slot]).start(priority=priority)
                return carry

            def group(rg, carry):
                for k in range(SCATTER_UNROLL):
                    one(SCATTER_UNROLL * rg + k, carry, priority=k % 2)
                return carry

            n_rows = n_valid_ref[i]
            n_groups = n_rows // SCATTER_UNROLL
            lax.fori_loop(0, n_groups, group, 0)
            lax.fori_loop(n_groups * SCATTER_UNROLL, n_rows, one, 0)

    _scatter_step(i, pl.num_programs(0), wait_step, start)


def _moe_experts(xb, blk_e, n_valid, row_dst, wg, wu, wd, n_out_rows):
    n_blocks = xb.shape[0] // MOE_ROWS
    weight = lambda shape: pl.BlockSpec((1,) + shape, lambda i, be, nv, rd: (be[i], 0, 0))
    return pl.pallas_call(
        _moe_kernel,
        grid_spec=pltpu.PrefetchScalarGridSpec(
            num_scalar_prefetch=3,
            grid=(n_blocks,),
            in_specs=[pl.BlockSpec((MOE_ROWS, ROW_TILES, 128), lambda i, be, nv, rd: (i, 0, 0)),
                      weight((D_MODEL, D_FF)), weight((D_MODEL, D_FF)), weight((D_FF, D_MODEL))],
            out_specs=pl.BlockSpec(memory_space=pl.ANY),
            scratch_shapes=[pltpu.VMEM((MOE_ROWS, D_MODEL), BF16),
                            pltpu.VMEM((2, MOE_ROWS, ROW_TILES, 128), F32),
                            pltpu.SemaphoreType.DMA((2,))],
        ),
        out_shape=jax.ShapeDtypeStruct((n_out_rows, ROW_TILES, 128), F32),
        compiler_params=_cparams("arbitrary"),
        name="moe_experts",
    )(blk_e, n_valid, row_dst, xb, wg, wu, wd)


def _combine_kernel(x_ref, y0_ref, y1_ref, gate_ref, g_ref, b_ref, o_ref):
    gate = gate_ref[...]
    f = gate[:, 0:1] * _from_row_tiles(y0_ref, F32) + gate[:, 1:2] * _from_row_tiles(y1_ref, F32)
    o_ref[...] = _layer_norm(DN_ALPHA * x_ref[...] + f, g_ref[...], b_ref[...])


def _combine(x2, y_slots, gate, ln_g, ln_b, tm=512):
    n = x2.shape[0]
    row = lambda i: (i, 0)
    vec = pl.BlockSpec((1, D_MODEL), lambda i: (0, 0))
    return pl.pallas_call(
        _combine_kernel,
        grid=(n // tm,),
        in_specs=[pl.BlockSpec((tm, D_MODEL), row),
                  pl.BlockSpec((tm, ROW_TILES, 128), lambda i: (i, 0, 0)),
                  pl.BlockSpec((tm, ROW_TILES, 128), lambda i: (n // tm + i, 0, 0)),
                  pl.BlockSpec((tm, EXPERT_TOPK), row), vec, vec],
        out_specs=pl.BlockSpec((tm, D_MODEL), row),
        out_shape=jax.ShapeDtypeStruct((n, D_MODEL), F32),
        compiler_params=_cparams("parallel"),
        name="combine_ln",
    )(x2, y_slots, y_slots, gate, ln_g, ln_b)


def _moe_layer(x2, router_w, wg, wu, wd, ln_g, ln_b):
    n = x2.shape[0]
    nk = n * EXPERT_TOPK
    top_e, gate = _router(x2, router_w)
    top_e = top_e[:, :EXPERT_TOPK]
    gate = gate[:, :EXPERT_TOPK]
    flat_e = top_e.reshape(-1)
    onehot = (flat_e[:, None] == jnp.arange(N_EXPERTS, dtype=jnp.int32)[None, :]).astype(jnp.int32)
    counts = jnp.sum(onehot, axis=0)
    rank = jnp.sum((jnp.cumsum(onehot, axis=0) - onehot) * onehot, axis=1)
    padded = (counts + MOE_ROWS - 1) // MOE_ROWS * MOE_ROWS
    pad_end = jnp.cumsum(padded)
    pad_start = pad_end - padded
    dest = (pad_start[flat_e] + rank).astype(jnp.int32)
    n_blocks = -(-nk // MOE_ROWS) + N_EXPERTS
    n_rows = n_blocks * MOE_ROWS
    blk_start = jnp.arange(n_blocks, dtype=jnp.int32) * MOE_ROWS
    blk_e = jnp.sum((blk_start[:, None] >= pad_end[None, :]).astype(jnp.int32), axis=1)
    valid_end = jnp.concatenate([pad_start + counts, jnp.zeros((1,), jnp.int32)])
    n_valid = jnp.clip(valid_end[blk_e] - blk_start, 0, MOE_ROWS).astype(jnp.int32)
    blk_e = jnp.minimum(blk_e, N_EXPERTS - 1).astype(jnp.int32)
    xb, row_dst = _dispatch(x2, dest, n_rows)
    y_slots = _moe_experts(xb, blk_e, n_valid, row_dst, wg, wu, wd, nk)
    return _combine(x2, y_slots, gate, ln_g, ln_b)


def _rope_tables(t):
    half = HEAD_DIM // 2
    inv = ROPE_THETA ** (-jnp.arange(half, dtype=F32) / half)
    ang = jnp.arange(t).astype(F32)[:, None] * inv[None, :]
    cos = jnp.cos(ang)
    sin = jnp.sin(ang)
    cos_t = jnp.tile(jnp.concatenate([cos, cos], -1), (1, N_HEADS))
    sin_t = jnp.tile(jnp.concatenate([-sin, sin], -1), (1, N_HEADS))
    return cos_t, sin_t


def _pad_rows(w, start):
    out = jnp.zeros((LORA_DIM, BRANCH_DIM), F32)
    return lax.dynamic_update_slice(out, w, (start, 0)).astype(BF16)


def _mixer(x2, b, t, cos_t, sin_t, w_in, conv_w, shift_mu, decay_w0, decay_w2, aaa_a0, aaa_w2, gate_w2,
           k_k, k_a, r_k, gn_g, gn_b, w_branch, w_out, ln_g, ln_b, ffn_weights):
    slabs = [w.reshape(-1, w.shape[-1]) for w in ffn_weights]
    n_steps = b * (t // MOBA_BLOCK)
    in_kernel = [s.shape[0] % (n_steps * BF16_ROWS) == 0 for s in slabs]
    o_conv, q3, k3, vt4, kmean, rkv3, lora3, gates, *cast = _in_proj(
        x2.reshape(b, t, D_MODEL), w_in, conv_w, cos_t, sin_t,
        tuple(s for s, ok in zip(slabs, in_kernel) if ok))
    cast = iter(cast)
    ffn_bf16 = [(next(cast) if ok else s.astype(BF16)).reshape(w.shape)
                for s, ok, w in zip(slabs, in_kernel, ffn_weights)]
    o_att = _moba(q3, k3, vt4, kmean.reshape(b, t // MOBA_BLOCK, BRANCH_DIM)).reshape(b * t, BRANCH_DIM)

    row = lambda a: a.reshape(1, -1)
    p = {
        "mu_r": row(shift_mu[0:BRANCH_DIM]), "mu_k": row(shift_mu[BRANCH_DIM:2 * BRANCH_DIM]),
        "mu_v": row(shift_mu[2 * BRANCH_DIM:3 * BRANCH_DIM]), "mu_l": row(shift_mu[3 * BRANCH_DIM:]),
        "w0": row(decay_w0), "w2": _pad_rows(decay_w2, 0),
        "a0": row(aaa_a0), "a2": _pad_rows(aaa_w2, DECAY_LORA),
        "g2": _pad_rows(gate_w2, DECAY_LORA + AAA_LORA),
        "k_k": row(k_k), "k_a": row(k_a), "r_k": row(r_k), "gn_g": row(gn_g), "gn_b": row(gn_b),
    }
    o_rwkv = _rwkv(rkv3, lora3, p)

    x2 = _merge(o_conv.reshape(b * t, BRANCH_DIM), o_att, o_rwkv.reshape(b * t, BRANCH_DIM),
                gates.reshape(b * t, GATE_COLS), x2,
                w_branch.astype(BF16), w_out.astype(BF16), row(ln_g), row(ln_b))
    return x2, ffn_bf16


def kernel(x, w_in, conv_w, shift_mu, decay_w0, decay_w2, aaa_a0, aaa_w2, gate_w2, k_k, k_a, r_k, gn_g, gn_b,
           w_branch, w_out, ln1_g, ln1_b, ln2_g, ln2_b, ffn_w_gate, ffn_w_up, ffn_w_down, router_w,
           moe_w_gate, moe_w_up, moe_w_down):
    b, t, d = x.shape
    x2 = x.reshape(b * t, d)
    cos_t, sin_t = _rope_tables(t)
    row = lambda a: a.reshape(1, -1)
    w_in_bf16 = w_in[0].astype(BF16)
    for l in range(DEPTH):
        j = l // 2
        dense = l % 2 == 0
        ffn_weights = ((ffn_w_gate[j], ffn_w_up[j], ffn_w_down[j]) if dense
                       else (moe_w_gate[j], moe_w_up[j], moe_w_down[j]))
        x2, (wg, wu, wd) = _mixer(x2, b, t, cos_t, sin_t, w_in_bf16, conv_w[l], shift_mu[l], decay_w0[l],
                                  decay_w2[l], aaa_a0[l], aaa_w2[l], gate_w2[l], k_k[l], k_a[l], r_k[l],
                                  gn_g[l], gn_b[l], w_branch[l], w_out[l], ln1_g[l], ln1_b[l], ffn_weights)
        if dense:
            x2, w_in_bf16 = _ffn(x2, wg, wu, wd, row(ln2_g[l]), row(ln2_b[l]), w_in[min(l + 1, DEPTH - 1)])
        else:
            x2 = _moe_layer(x2, router_w[j], wg, wu, wd, row(ln2_g[l]), row(ln2_b[l]))
            if l + 1 < DEPTH:
                w_in_bf16 = w_in[l + 1].astype(BF16)
    return x2.reshape(b, t, d)
```

```python
import functools

import jax
import jax.numpy as jnp
from jax import lax
from jax.experimental import pallas as pl
from jax.experimental.pallas import tpu as pltpu

F32 = jnp.float32
BF16 = jnp.bfloat16

D_MODEL = 1024
HEAD_DIM = 64
BRANCH_DIM = 512
N_HEADS = BRANCH_DIM // HEAD_DIM
DECAY_LORA = 64
AAA_LORA = 64
GATE_LORA = 128
LORA_DIM = DECAY_LORA + AAA_LORA + GATE_LORA
MIX_COLS = 9 * BRANCH_DIM
GATE_COLS = 3 * D_MODEL
MOBA_BLOCK = 256
MOBA_TOPK = 3
ROPE_THETA = 10000.0
D_FF = 2816
N_EXPERTS = 8
EXPERT_TOPK = 2
MOE_ROWS = 256
LN_EPS = 1e-5
GN_EPS = 64e-5
DEPTH = 2
DN_ALPHA = (2 * DEPTH) ** 0.25
LOG2_E = 1.4426950408889634

VMEM_LIMIT_BYTES = 56 * 1024 * 1024
SUBLANES = 8
BF16_ROWS = 16
RWKV_TILE = 256
RWKV_CHUNK = 64
FF_CHUNK = 256
SCATTER_UNROLL = 8
PAIR_WIDTH = 2 * HEAD_DIM
HEAD_SHIFT = HEAD_DIM.bit_length() - 1
CHUNK_SHIFT = RWKV_CHUNK.bit_length() - 1


def _cparams(*sem):
    return pltpu.CompilerParams(dimension_semantics=sem, vmem_limit_bytes=VMEM_LIMIT_BYTES)


def _vmem_whole():
    return pl.BlockSpec(memory_space=pltpu.VMEM)


def _dot(a, b):
    return jnp.dot(a.astype(BF16), b.astype(BF16), preferred_element_type=F32)


def _dot_nt(a, b):
    return lax.dot_general(a.astype(BF16), b.astype(BF16), (((1,), (1,)), ((), ())),
                           preferred_element_type=F32)


def _dot_tn(a, b):
    return lax.dot_general(a.astype(BF16), b.astype(BF16), (((0,), (0,)), ((), ())),
                           preferred_element_type=F32)


def _split_bf16(x):
    hi = x.astype(BF16)
    return hi, (x - hi.astype(F32)).astype(BF16)


def _dot_split(a, b, *, data_on_left):
    hi, lo = _split_bf16(a if data_on_left else b)
    if data_on_left:
        return jnp.dot(hi, b, preferred_element_type=F32) + jnp.dot(lo, b, preferred_element_type=F32)
    return jnp.dot(a, hi, preferred_element_type=F32) + jnp.dot(a, lo, preferred_element_type=F32)


def _sigmoid(x):
    return 1.0 / (1.0 + jnp.exp(-x))


def _layer_norm(y, g, b):
    mu = jnp.mean(y, axis=-1, keepdims=True)
    d = y - mu
    var = jnp.mean(d * d, axis=-1, keepdims=True)
    return d * lax.rsqrt(var + LN_EPS) * g + b


def _shift_rows(u, halo, n):
    out = pltpu.roll(u, n, axis=0)
    row = lax.broadcasted_iota(jnp.int32, u.shape, 0)
    last = halo.shape[0]
    for r in range(n):
        src = halo[last - n + r:last - n + r + 1, :]
        out = jnp.where(row == r, src, out)
    return out


def _halo_index(tile_rows, halo_rows):
    step = tile_rows // halo_rows

    def index(col):
        return lambda b, i: (b, jnp.maximum(i * step - 1, 0), col)
    return index


def _in_proj_kernel(*refs, n_side):
    x_ref, wm_ref, convw_ref, cos_ref, sin_ref = refs[:5]
    side_in = refs[5:5 + n_side]
    conv_ref, q_ref, k_ref, vt_ref, km_ref, rkv_ref, lora_ref, gate_ref = refs[5 + n_side:13 + n_side]
    side_out = refs[13 + n_side:13 + 2 * n_side]
    halo_ref = refs[13 + 2 * n_side]
    for src, dst in zip(side_in, side_out):
        dst[...] = src[...].astype(dst.dtype)

    first = pl.program_id(1) == 0
    bd = BRANCH_DIM
    xb = x_ref[0].astype(BF16)
    proj = lambda g: jnp.dot(xb, wm_ref[:, g * bd:(g + 1) * bd], preferred_element_type=F32)

    u = proj(2) * proj(0)
    halo = jnp.where(first, 0.0, halo_ref[...])
    w = convw_ref[...]
    y = w[2:3, :] * u + w[1:2, :] * _shift_rows(u, halo, 1) + w[0:1, :] * _shift_rows(u, halo, 2)
    conv_ref[0] = (proj(1) * y).astype(conv_ref.dtype)
    halo_ref[...] = u[u.shape[0] - SUBLANES:, :]

    cos = cos_ref[...]
    sin = sin_ref[...]
    lane = lax.broadcasted_iota(jnp.int32, cos.shape, 1)
    first_half = (lane & (HEAD_DIM - 1)) < (HEAD_DIM // 2)

    def rope(t):
        swapped = jnp.where(first_half, pltpu.roll(t, bd - HEAD_DIM // 2, axis=1),
                            pltpu.roll(t, HEAD_DIM // 2, axis=1))
        return t * cos + swapped * sin

    kr = rope(proj(4))
    q_ref[0] = rope(proj(3))
    k_ref[0] = kr.astype(BF16)
    km_ref[0, 0] = jnp.mean(kr, axis=0, keepdims=True)
    vt_ref[0, 0] = proj(5).astype(BF16).T

    for g in range(3):
        rkv_ref[0, :, g * bd:(g + 1) * bd] = proj(6 + g).astype(rkv_ref.dtype)
    lora_ref[0] = jnp.dot(xb, wm_ref[:, MIX_COLS:MIX_COLS + LORA_DIM], preferred_element_type=F32)
    for j in range(0, GATE_COLS, bd):
        col = MIX_COLS + LORA_DIM + j
        g = jnp.dot(xb, wm_ref[:, col:col + bd], preferred_element_type=F32)
        gate_ref[0, :, j:j + bd] = _sigmoid(g).astype(gate_ref.dtype)


def _in_proj(x3, w_in, conv_w, cos_t, sin_t, side_casts=()):
    b, t, _ = x3.shape
    tm = MOBA_BLOCK
    nb = t // tm
    blk = lambda width: pl.BlockSpec((1, tm, width), lambda bi, i: (bi, i, 0))
    tab = pl.BlockSpec((tm, BRANCH_DIM), lambda bi, i: (i, 0))
    shape3 = lambda width, dtype: jax.ShapeDtypeStruct((b, t, width), dtype)
    for a in side_casts:
        assert a.shape[0] % (b * nb * BF16_ROWS) == 0, a.shape
    slab = lambda a: pl.BlockSpec((a.shape[0] // (b * nb), a.shape[1]), lambda bi, i: (bi * nb + i, 0))
    return pl.pallas_call(
        functools.partial(_in_proj_kernel, n_side=len(side_casts)),
        grid=(b, nb),
        in_specs=[blk(D_MODEL), _vmem_whole(), pl.BlockSpec((3, BRANCH_DIM), lambda bi, i: (0, 0)), tab, tab]
                 + [slab(a) for a in side_casts],
        out_specs=[blk(BRANCH_DIM), blk(BRANCH_DIM), blk(BRANCH_DIM),
                   pl.BlockSpec((1, 1, BRANCH_DIM, tm), lambda bi, i: (bi, i, 0, 0)),
                   pl.BlockSpec((1, 1, 1, BRANCH_DIM), lambda bi, i: (bi, i, 0, 0)),
                   blk(3 * BRANCH_DIM), blk(LORA_DIM), blk(GATE_COLS)] + [slab(a) for a in side_casts],
        out_shape=[shape3(BRANCH_DIM, BF16), shape3(BRANCH_DIM, F32), shape3(BRANCH_DIM, BF16),
                   jax.ShapeDtypeStruct((b, nb, BRANCH_DIM, tm), BF16),
                   jax.ShapeDtypeStruct((b, nb, 1, BRANCH_DIM), F32),
                   shape3(3 * BRANCH_DIM, BF16), shape3(LORA_DIM, F32), shape3(GATE_COLS, BF16)]
                  + [jax.ShapeDtypeStruct(a.shape, BF16) for a in side_casts],
        scratch_shapes=[pltpu.VMEM((SUBLANES, BRANCH_DIM), F32)],
        compiler_params=_cparams("parallel", "arbitrary"),
        name="in_proj",
    )(x3, w_in, conv_w, cos_t, sin_t, *side_casts)


def _moba_kernel(q_ref, k_ref, vt_ref, km_ref, o_ref, *, nb, heads):
    i = pl.program_id(2)
    bs = MOBA_BLOCK
    width = heads * HEAD_DIM
    q = q_ref[0]
    km = km_ref[0]
    head_of_lane = lax.broadcasted_iota(jnp.int32, (1, width), 1) >> HEAD_SHIFT
    q_heads = jnp.concatenate([jnp.where(head_of_lane == g, q, 0.0) for g in range(heads)], axis=0)
    q_all = (q_heads * (HEAD_DIM ** -0.5 * LOG2_E)).astype(BF16)

    blk = lax.broadcasted_iota(jnp.int32, (nb, heads * bs), 0)
    valid = blk < i
    km_heads = jnp.concatenate([jnp.where(head_of_lane == g, km, 0.0) for g in range(heads)], axis=0)
    km_hi, km_lo = _split_bf16(km_heads)
    q_hi, q_lo = _split_bf16(q)
    hi_part = _dot_nt(jnp.concatenate([km_hi, km_lo], axis=0), q_hi)
    gate = hi_part[:heads * nb] + hi_part[heads * nb:] + _dot_nt(km_hi, q_lo)
    gate = jnp.concatenate([gate[g * nb:(g + 1) * nb] for g in range(heads)], axis=1)
    gate = jnp.where(valid, gate, -jnp.inf)
    rank = jnp.zeros((nb, heads * bs), jnp.int32)
    for m in range(nb):
        gm = gate[m:m + 1, :]
        beats = (gm > gate) | ((gm == gate) & (blk > m))
        rank = rank + beats.astype(jnp.int32)
    chosen = jnp.where(valid & (rank < MOBA_TOPK), (1 << blk).astype(F32), 0.0)
    picks = jnp.sum(chosen, axis=0, keepdims=True).astype(jnp.int32)

    ones_rows = jnp.ones((SUBLANES, bs), BF16)

    def scores(j, n_blocks):
        kb = k_ref[0, pl.ds(pl.multiple_of(j * bs, bs), n_blocks * bs), :]
        return lax.dot_general(kb, q_all, (((1,), (1,)), ((), ())), preferred_element_type=F32)

    def block_update(j, s, chosen, causal, carry):
        n_blocks = len(chosen)
        m, accs = carry
        if causal is not None:
            s = jnp.where(causal, s, -jnp.inf)
        parts = [s[n * bs:(n + 1) * bs] for n in range(n_blocks)]
        tops = [jnp.max(part, axis=0, keepdims=True) for part in parts]
        tops = [t if c is None else jnp.where(c, t, -jnp.inf) for t, c in zip(tops, chosen)]
        m_new = functools.reduce(jnp.maximum, tops, m)
        alpha = jnp.exp2(m - m_new)
        shifts = [m_new if c is None else jnp.where(c, m_new, jnp.inf) for c in chosen]
        ps = [jnp.exp2(part - shift).astype(BF16) for part, shift in zip(parts, shifts)]
        new_acc = []
        for g in range(heads):
            cols = slice(g * bs, (g + 1) * bs)
            acc = alpha[:, cols] * accs[g]
            for n in range(n_blocks):
                vt = jnp.concatenate([vt_ref[0, j + n, g * HEAD_DIM:(g + 1) * HEAD_DIM, :], ones_rows], axis=0)
                acc = acc + jnp.dot(vt, ps[n][:, cols], preferred_element_type=F32)
            new_acc.append(acc)
        return m_new, tuple(new_acc)

    key_id = lax.broadcasted_iota(jnp.int32, (bs, heads * bs), 0)
    query_id = lax.broadcasted_iota(jnp.int32, (bs, heads * bs), 1) & (bs - 1)
    init = (jnp.full((1, heads * bs), -jnp.inf, F32),
            tuple(jnp.zeros((HEAD_DIM + SUBLANES, bs), F32) for _ in range(heads)))
    carry = block_update(i, scores(i, 1), [None], key_id <= query_id, init)

    def pair_update(j, s, carry):
        bits = picks >> j
        return block_update(j, s, [(bits & 1) == 1, ((bits >> 1) & 1) == 1], None, carry)

    def two_pairs(jq, carry):
        j = 4 * jq
        s_a, s_b = scores(j, 2), scores(j + 2, 2)
        return pair_update(j + 2, s_b, pair_update(j, s_a, carry))

    carry = lax.fori_loop(0, i // 4, two_pairs, carry)
    done = (i // 4) * 4
    _, accs = lax.fori_loop(0, (i - done + 1) // 2,
                            lambda jj, c: pair_update(done + 2 * jj, scores(done + 2 * jj, 2), c), carry)
    out_t = jnp.concatenate([acc[:HEAD_DIM] / acc[HEAD_DIM:HEAD_DIM + 1] for acc in accs], axis=0)
    o_ref[0] = out_t.T.astype(o_ref.dtype)


def _moba(q3, k3, vt4, kmean, heads=4):
    b, t, _ = q3.shape
    nb = t // MOBA_BLOCK
    width = heads * HEAD_DIM
    qblk = pl.BlockSpec((1, MOBA_BLOCK, width), lambda bi, hi, i: (bi, i, hi))
    return pl.pallas_call(
        functools.partial(_moba_kernel, nb=nb, heads=heads),
        grid=(b, N_HEADS // heads, nb),
        in_specs=[qblk, pl.BlockSpec((1, t, width), lambda bi, hi, i: (bi, 0, hi)),
                  pl.BlockSpec((1, nb, width, MOBA_BLOCK), lambda bi, hi, i: (bi, 0, hi, 0)),
                  pl.BlockSpec((1, nb, width), lambda bi, hi, i: (bi, 0, hi))],
        out_specs=qblk,
        out_shape=jax.ShapeDtypeStruct((b, t, BRANCH_DIM), BF16),
        compiler_params=_cparams("parallel", "parallel", "arbitrary"),
        name="moba",
    )(q3, k3, vt4, kmean)


def _rwkv_kernel(r_ref, k_ref, v_ref, lo_ref, hr_ref, hk_ref, hv_ref, hl_ref,
                 mu_r_ref, mu_k_ref, mu_v_ref, mu_l_ref, w0_ref, w2_ref, a0_ref, a2_ref, g2_ref,
                 kk_ref, ka_ref, rk_ref, gng_ref, gnb_ref, o_ref, state_ref):
    i = pl.program_id(1)
    first = i == 0
    tq, ch = RWKV_TILE, RWKV_CHUNK
    hd, bd = HEAD_DIM, BRANCH_DIM

    @pl.when(first)
    def _():
        state_ref[...] = jnp.zeros_like(state_ref)

    def lerp(x_ref, h_ref, mu_ref):
        x = x_ref[0].astype(F32)
        halo = jnp.where(first, 0.0, h_ref[0].astype(F32))
        return x + (_shift_rows(x, halo, 1) - x) * mu_ref[...]

    r = lerp(r_ref, hr_ref, mu_r_ref)
    k = lerp(k_ref, hk_ref, mu_k_ref)
    v = lerp(v_ref, hv_ref, mu_v_ref)
    lo = lerp(lo_ref, hl_ref, mu_l_ref)

    z = w0_ref[...] + _dot(jnp.tanh(lo), w2_ref[...])
    ew = jnp.exp(-0.5) * _sigmoid(z)
    a = _sigmoid(a0_ref[...] + _dot(lo, a2_ref[...]))
    g = _dot(_sigmoid(lo), g2_ref[...])

    half = bd // 2
    lane_r = lax.broadcasted_iota(jnp.int32, (half, half), 0)
    lane_c = lax.broadcasted_iota(jnp.int32, (half, half), 1)
    ones_blocks = jnp.where((lane_r >> HEAD_SHIFT) == (lane_c >> HEAD_SHIFT), 1.0, 0.0).astype(BF16)

    def head_sum(x):
        return jnp.concatenate([_dot_split(x[:, :half], ones_blocks, data_on_left=True),
                                _dot_split(x[:, half:], ones_blocks, data_on_left=True)], axis=1)

    kk = k * kk_ref[...]
    kk = kk / jnp.maximum(jnp.sqrt(head_sum(kk * kk)), 1e-12)
    k2 = k * (1.0 + (a - 1.0) * ka_ref[...])
    bonus = head_sum(r * k2 * rk_ref[...]) * v

    t_r = lax.broadcasted_iota(jnp.int32, (tq, tq), 0)
    t_c = lax.broadcasted_iota(jnp.int32, (tq, tq), 1)
    same_chunk = (t_r >> CHUNK_SHIFT) == (t_c >> CHUNK_SHIFT)
    cs = _dot_split(jnp.where(same_chunk & (t_c <= t_r), 1.0, 0.0).astype(BF16), ew, data_on_left=False)
    cs_end = jnp.concatenate(
        [jnp.broadcast_to(cs[c * ch + ch - 1:c * ch + ch, :], (ch, bd)) for c in range(tq // ch)], axis=0)

    kka = kk * a
    a_t = -kk * jnp.exp(ew - cs)
    b_hat = (kka * jnp.exp(cs)).astype(BF16)
    k_hat = (k2 * jnp.exp(cs)).astype(BF16)
    r_t = r * jnp.exp(-cs)
    b_e = (kka * jnp.exp(cs - cs_end)).astype(BF16)
    k_e = (k2 * jnp.exp(cs - cs_end)).astype(BF16)
    p_c = jnp.exp(-cs_end)
    v_b = v.astype(BF16)

    strict = jnp.where(same_chunk & (t_c < t_r), 1.0, 0.0)
    incl = jnp.where(same_chunk & (t_c <= t_r), 1.0, 0.0)
    eye = jnp.where(t_c == t_r, 1.0, 0.0)
    pw = PAIR_WIDTH
    second = (lax.broadcasted_iota(jnp.int32, (1, pw), 1) >> HEAD_SHIFT) == 1
    same_head = ((lax.broadcasted_iota(jnp.int32, (pw, pw), 0) >> HEAD_SHIFT)
                 == (lax.broadcasted_iota(jnp.int32, (pw, pw), 1) >> HEAD_SHIFT))

    n_pairs = N_HEADS // 2
    pair = lambda x, p: x[:, p * pw:(p + 1) * pw]
    head_list = [(p, hm) for p in range(n_pairs) for hm in (~second, second)]
    a_hs = [jnp.where(hm, pair(a_t, p), 0.0).astype(BF16) for p, hm in head_list]
    l_abs = [strict * _dot_nt(a_h, pair(b_hat, p)) for a_h, (p, _) in zip(a_hs, head_list)]
    t_invs = [eye + x for x in l_abs]
    powers = l_abs
    n = 2
    while n < ch:
        powers = [_dot(x, x) for x in powers]
        t_invs = [t + _dot(t, x) for t, x in zip(t_invs, powers)]
        n *= 2
    l_aks = [strict * _dot_nt(a_h, pair(k_hat, p)) for a_h, (p, _) in zip(a_hs, head_list)]
    lakvs = [_dot(l_ak, pair(v_b, p)) for l_ak, (p, _) in zip(l_aks, head_list)]
    sols = [_dot(t, jnp.concatenate([pair(a_t, p).astype(BF16), lakv.astype(BF16)], axis=1))
            for t, lakv, (p, _) in zip(t_invs, lakvs, head_list)]
    r_hs = [jnp.where(hm, pair(r_t, p), 0.0).astype(BF16) for p, hm in head_list]
    m_rbs = [incl * _dot_nt(r_h, pair(b_hat, p)) for r_h, (p, _) in zip(r_hs, head_list)]
    m_rks = [incl * _dot_nt(r_h, pair(k_hat, p)) for r_h, (p, _) in zip(r_hs, head_list)]
    apps = [_dot(m_rb, sol) for m_rb, sol in zip(m_rbs, sols)]
    mrkvs = [_dot(m_rk, pair(v_b, p)) for m_rk, (p, _) in zip(m_rks, head_list)]

    def both(xs, p, cols):
        return jnp.where(second, xs[2 * p + 1][:, cols], xs[2 * p][:, cols])

    left, right = slice(0, pw), slice(pw, 2 * pw)
    a_til = [both(sols, p, left).astype(BF16) for p in range(n_pairs)]
    u_0 = [both(sols, p, right).astype(BF16) for p in range(n_pairs)]
    r_til = [(both(apps, p, left) + pair(r_t, p)).astype(BF16) for p in range(n_pairs)]
    y_hat = [both(apps, p, right) + both(mrkvs, p, left) for p in range(n_pairs)]

    states = [state_ref[p] for p in range(n_pairs)]
    y_rows = []
    for c in range(tq // ch):
        rows = slice(c * ch, (c + 1) * ch)
        g_cs = [jnp.where(same_head, _dot_tn(a_til[p][rows], pair(b_e, p)[rows]), 0.0) for p in range(n_pairs)]
        h_cs = [jnp.where(same_head, _dot_tn(u_0[p][rows], pair(b_e, p)[rows])
                          + _dot_tn(pair(v_b, p)[rows], pair(k_e, p)[rows]), 0.0) for p in range(n_pairs)]
        y_rows.append(jnp.concatenate(
            [_dot_nt(r_til[p][rows], states[p]) + y_hat[p][rows] for p in range(n_pairs)], axis=1))
        states = [states[p] * pair(p_c, p)[c * ch:c * ch + 1] + _dot(states[p], g_cs[p]) + h_cs[p]
                  for p in range(n_pairs)]
    for p in range(n_pairs):
        state_ref[p] = states[p]
    y = jnp.concatenate(y_rows, axis=0)

    mu = head_sum(y) * (1.0 / hd)
    d = y - mu
    var = head_sum(d * d) * (1.0 / hd)
    y = d * lax.rsqrt(var + GN_EPS) * gng_ref[...] + gnb_ref[...]
    o_ref[0] = ((y + bonus) * g).astype(o_ref.dtype)


def _rwkv(rkv3, lora3, p):
    b, t, _ = rkv3.shape
    tq = RWKV_TILE
    halo = _halo_index(tq, SUBLANES)
    halo_bf16 = _halo_index(tq, BF16_ROWS)
    blk = lambda col: pl.BlockSpec((1, tq, BRANCH_DIM), lambda bi, i: (bi, i, col))
    hblk = lambda col: pl.BlockSpec((1, BF16_ROWS, BRANCH_DIM), halo_bf16(col))
    vec = lambda width: pl.BlockSpec((1, width), lambda bi, i: (0, 0))
    mat = lambda rows: pl.BlockSpec((rows, BRANCH_DIM), lambda bi, i: (0, 0))
    return pl.pallas_call(
        _rwkv_kernel,
        grid=(b, t // tq),
        in_specs=[blk(0), blk(1), blk(2), pl.BlockSpec((1, tq, LORA_DIM), lambda bi, i: (bi, i, 0)),
                  hblk(0), hblk(1), hblk(2), pl.BlockSpec((1, SUBLANES, LORA_DIM), halo(0)),
                  vec(BRANCH_DIM), vec(BRANCH_DIM), vec(BRANCH_DIM), vec(LORA_DIM),
                  vec(BRANCH_DIM), mat(LORA_DIM), vec(BRANCH_DIM), mat(LORA_DIM), mat(LORA_DIM),
                  vec(BRANCH_DIM), vec(BRANCH_DIM), vec(BRANCH_DIM), vec(BRANCH_DIM), vec(BRANCH_DIM)],
        out_specs=pl.BlockSpec((1, tq, BRANCH_DIM), lambda bi, i: (bi, i, 0)),
        out_shape=jax.ShapeDtypeStruct((b, t, BRANCH_DIM), BF16),
        scratch_shapes=[pltpu.VMEM((N_HEADS // 2, PAIR_WIDTH, PAIR_WIDTH), F32)],
        compiler_params=_cparams("parallel", "arbitrary"),
        name="rwkv7",
    )(rkv3, rkv3, rkv3, lora3, rkv3, rkv3, rkv3, lora3,
      p["mu_r"], p["mu_k"], p["mu_v"], p["mu_l"], p["w0"], p["w2"], p["a0"], p["a2"], p["g2"],
      p["k_k"], p["k_a"], p["r_k"], p["gn_g"], p["gn_b"])


def _merge_kernel(oc_ref, oa_ref, or_ref, gate_ref, x_ref, wb_ref, wo_ref, g_ref, b_ref, o_ref):
    d = D_MODEL
    bd = BRANCH_DIM
    merged = (gate_ref[:, 0:d] * jnp.dot(oc_ref[...], wb_ref[0:bd, :], preferred_element_type=F32)
              + gate_ref[:, d:2 * d] * jnp.dot(oa_ref[...], wb_ref[bd:2 * bd, :], preferred_element_type=F32)
              + gate_ref[:, 2 * d:3 * d] * jnp.dot(or_ref[...], wb_ref[2 * bd:3 * bd, :],
                                                   preferred_element_type=F32))
    h = jnp.dot(merged.astype(BF16), wo_ref[...], preferred_element_type=F32)
    o_ref[...] = _layer_norm(DN_ALPHA * x_ref[...] + h, g_ref[...], b_ref[...])


def _merge(o_conv, o_att, o_rwkv, gates, x2, w_branch, w_out, ln_g, ln_b, tm=512):
    n = x2.shape[0]
    row = lambda i: (i, 0)
    vec = pl.BlockSpec((1, D_MODEL), lambda i: (0, 0))
    return pl.pallas_call(
        _merge_kernel,
        grid=(n // tm,),
        in_specs=[pl.BlockSpec((tm, BRANCH_DIM), row)] * 3
                 + [pl.BlockSpec((tm, GATE_COLS), row), pl.BlockSpec((tm, D_MODEL), row),
                    _vmem_whole(), _vmem_whole(), vec, vec],
        out_specs=pl.BlockSpec((tm, D_MODEL), row),
        out_shape=jax.ShapeDtypeStruct((n, D_MODEL), F32),
        compiler_params=_cparams("parallel"),
        name="merge_ln",
    )(o_conv, o_att, o_rwkv, gates, x2, w_branch, w_out, ln_g, ln_b)


def _swiglu_rows(xb, wg_ref, wu_ref, wd_ref, lead):
    hidden = []
    for c in range(0, D_FF, FF_CHUNK):
        cols = slice(c, c + FF_CHUNK)
        hg = jnp.dot(xb, wg_ref[lead + (slice(None), cols)], preferred_element_type=F32)
        hu = jnp.dot(xb, wu_ref[lead + (slice(None), cols)], preferred_element_type=F32)
        hidden.append((hg * _sigmoid(hg) * hu).astype(BF16))
    return jnp.dot(jnp.concatenate(hidden, axis=1), wd_ref[lead + (slice(None), slice(None))],
                   preferred_element_type=F32)


def _ffn_kernel(x_ref, wg_ref, wu_ref, wd_ref, g_ref, b_ref, side_in, o_ref, side_out):
    side_out[...] = side_in[...].astype(side_out.dtype)
    x = x_ref[...]
    f = _swiglu_rows(x.astype(BF16), wg_ref, wu_ref, wd_ref, ())
    o_ref[...] = _layer_norm(DN_ALPHA * x + f, g_ref[...], b_ref[...])


def _ffn(x2, wg, wu, wd, ln_g, ln_b, stacked, layer, tm=512):
    n = x2.shape[0]
    n_steps = n // tm
    n_layers, rows, cols = stacked.shape
    assert rows % (n_steps * BF16_ROWS) == 0, stacked.shape
    row = lambda i: (i, 0)
    vec = pl.BlockSpec((1, D_MODEL), lambda i: (0, 0))
    slab_rows = rows // n_steps
    return pl.pallas_call(
        _ffn_kernel,
        grid=(n_steps,),
        in_specs=[pl.BlockSpec((tm, D_MODEL), row), _vmem_whole(), _vmem_whole(), _vmem_whole(), vec, vec,
                  pl.BlockSpec((slab_rows, cols), lambda i: (layer * n_steps + i, 0))],
        out_specs=[pl.BlockSpec((tm, D_MODEL), row), pl.BlockSpec((slab_rows, cols), row)],
        out_shape=[jax.ShapeDtypeStruct((n, D_MODEL), F32), jax.ShapeDtypeStruct((rows, cols), BF16)],
        compiler_params=_cparams("parallel"),
        name="ffn_ln",
    )(x2, wg, wu, wd, ln_g, ln_b, stacked.reshape(n_layers * rows, cols))


def _router_kernel(x_ref, w_ref, e_ref, g_ref):
    x_hi, x_lo = _split_bf16(x_ref[...])
    w_hi, w_lo = _split_bf16(w_ref[...])
    tm = x_hi.shape[0]
    hi_part = _dot(jnp.concatenate([x_hi, x_lo], axis=0), w_hi)
    logits = hi_part[:tm] + hi_part[tm:] + _dot(x_hi, w_lo)
    lane = lax.broadcasted_iota(jnp.int32, logits.shape, 1).astype(F32)
    m1 = jnp.max(logits, axis=1, keepdims=True)
    e1 = jnp.min(jnp.where(logits == m1, lane, float(N_EXPERTS)), axis=1, keepdims=True)
    rest = jnp.where(lane == e1, -jnp.inf, logits)
    m2 = jnp.max(rest, axis=1, keepdims=True)
    e2 = jnp.min(jnp.where(rest == m2, lane, float(N_EXPERTS)), axis=1, keepdims=True)
    w2 = jnp.exp(m2 - m1)
    denom = 1.0 + w2
    e_ref[...] = jnp.where(lane == 0.0, e1, e2).astype(jnp.int32)
    g_ref[...] = jnp.where(lane == 0.0, 1.0 / denom, w2 / denom)


def _router(x2, router_w, tm=512):
    n = x2.shape[0]
    row = lambda i: (i, 0)
    return pl.pallas_call(
        _router_kernel,
        grid=(n // tm,),
        in_specs=[pl.BlockSpec((tm, D_MODEL), row), pl.BlockSpec((D_MODEL, N_EXPERTS), lambda i: (0, 0))],
        out_specs=[pl.BlockSpec((tm, N_EXPERTS), row), pl.BlockSpec((tm, N_EXPERTS), row)],
        out_shape=[jax.ShapeDtypeStruct((n, N_EXPERTS), jnp.int32),
                   jax.ShapeDtypeStruct((n, N_EXPERTS), F32)],
        compiler_params=_cparams("parallel"),
        name="router",
    )(x2, router_w)


ROW_TILES = D_MODEL // 128


def _to_row_tiles(dst_ref, lead, x):
    for c in range(ROW_TILES):
        dst_ref[lead + (slice(None), c, slice(None))] = x[:, c * 128:(c + 1) * 128]


def _from_row_tiles(src_ref, dtype):
    return jnp.concatenate([src_ref[:, c, :].astype(dtype) for c in range(ROW_TILES)], axis=1)


def _scatter_step(step, n_steps, wait_step, start_this_step):
    @pl.when(step >= 2)
    def _():
        wait_step(step - 2)

    start_this_step()

    @pl.when(step == n_steps - 1)
    def _():
        @pl.when(step >= 1)
        def _():
            wait_step(step - 1)

        wait_step(step)


def _wait_rows(buf_rows, hbm_ref, n_rows, sem):
    pltpu.make_async_copy(buf_rows.at[pl.ds(0, n_rows)], hbm_ref.at[pl.ds(0, n_rows)], sem).wait()


def _dispatch_kernel(dest_ref, x_ref, init_ref, xb_ref, back_ref, buf, sems, *, tm, n_tokens, n_rows):
    del init_ref
    i = pl.program_id(0)

    @pl.when(i == 0)
    def _():
        def clear(r, carry):
            back_ref[r] = 0
            return carry
        lax.fori_loop(0, n_rows, clear, 0, unroll=8)

    def wait_step(step):
        for _ in range(EXPERT_TOPK):
            _wait_rows(buf.at[step % 2], xb_ref, tm, sems.at[step % 2])

    def start():
        slot = i % 2
        _to_row_tiles(buf, (slot,), x_ref[...])

        def one(r, carry):
            for k in range(EXPERT_TOPK):
                dst = dest_ref[(i * tm + r) * EXPERT_TOPK + k]
                back_ref[dst] = k * n_tokens + i * tm + r
                pltpu.make_async_copy(buf.at[slot, r], xb_ref.at[dst], sems.at[slot]).start(priority=k)
            return carry
        lax.fori_loop(0, tm, one, 0, unroll=SCATTER_UNROLL)

    _scatter_step(i, pl.num_programs(0), wait_step, start)


def _dispatch(x2, dest, n_rows, tm=256):
    n = x2.shape[0]
    init = jnp.zeros((n_rows, ROW_TILES, 128), F32)
    return pl.pallas_call(
        functools.partial(_dispatch_kernel, tm=tm, n_tokens=n, n_rows=n_rows),
        grid_spec=pltpu.PrefetchScalarGridSpec(
            num_scalar_prefetch=1,
            grid=(n // tm,),
            in_specs=[pl.BlockSpec((tm, D_MODEL), lambda i, d: (i, 0)), pl.BlockSpec(memory_space=pl.ANY)],
            out_specs=[pl.BlockSpec(memory_space=pl.ANY), pl.BlockSpec(memory_space=pltpu.SMEM)],
            scratch_shapes=[pltpu.VMEM((2, tm, ROW_TILES, 128), F32), pltpu.SemaphoreType.DMA((2,))],
        ),
        out_shape=[jax.ShapeDtypeStruct((n_rows, ROW_TILES, 128), F32),
                   jax.ShapeDtypeStruct((n_rows,), jnp.int32)],
        input_output_aliases={2: 0},
        compiler_params=_cparams("arbitrary"),
        name="moe_dispatch",
    )(dest, x2, init)


def _moe_kernel(blk_e_ref, n_valid_ref, row_dst_ref, x_ref, wg_ref, wu_ref, wd_ref, y_ref, xs, ybuf, sems):
    del blk_e_ref
    i = pl.program_id(0)

    def wait_step(step):
        @pl.when(n_valid_ref[step] > 0)
        def _():
            _wait_rows(ybuf.at[step % 2], y_ref, n_valid_ref[step], sems.at[step % 2])

    def start():
        @pl.when(n_valid_ref[i] > 0)
        def _():
            slot = i % 2
            xs[...] = _from_row_tiles(x_ref, BF16)
            _to_row_tiles(ybuf, (slot,), _swiglu_rows(xs[...], wg_ref, wu_ref, wd_ref, (0,)))

            def one(r, carry, priority=0):
                dst = row_dst_ref[i * MOE_ROWS + r]
                pltpu.make_async_copy(ybuf.at[slot, r], y_ref.at[dst], sems.at[slot]).start(priority=priority)
                return carry

            def group(rg, carry):
                for k in range(SCATTER_UNROLL):
                    one(SCATTER_UNROLL * rg + k, carry, priority=k % 2)
                return carry

            n_rows = n_valid_ref[i]
            n_groups = n_rows // SCATTER_UNROLL
            lax.fori_loop(0, n_groups, group, 0)
            lax.fori_loop(n_groups * SCATTER_UNROLL, n_rows, one, 0)

    _scatter_step(i, pl.num_programs(0), wait_step, start)


def _moe_experts(xb, blk_e, n_valid, row_dst, wg, wu, wd, n_out_rows):
    n_blocks = xb.shape[0] // MOE_ROWS
    weight = lambda shape: pl.BlockSpec((1,) + shape, lambda i, be, nv, rd: (be[i], 0, 0))
    return pl.pallas_call(
        _moe_kernel,
        grid_spec=pltpu.PrefetchScalarGridSpec(
            num_scalar_prefetch=3,
            grid=(n_blocks,),
            in_specs=[pl.BlockSpec((MOE_ROWS, ROW_TILES, 128), lambda i, be, nv, rd: (i, 0, 0)),
                      weight((D_MODEL, D_FF)), weight((D_MODEL, D_FF)), weight((D_FF, D_MODEL))],
            out_specs=pl.BlockSpec(memory_space=pl.ANY),
            scratch_shapes=[pltpu.VMEM((MOE_ROWS, D_MODEL), BF16),
                            pltpu.VMEM((2, MOE_ROWS, ROW_TILES, 128), F32),
                            pltpu.SemaphoreType.DMA((2,))],
        ),
        out_shape=jax.ShapeDtypeStruct((n_out_rows, ROW_TILES, 128), F32),
        compiler_params=_cparams("arbitrary"),
        name="moe_experts",
    )(blk_e, n_valid, row_dst, xb, wg, wu, wd)


def _combine_kernel(x_ref, y0_ref, y1_ref, gate_ref, g_ref, b_ref, o_ref):
    gate = gate_ref[...]
    f = gate[:, 0:1] * _from_row_tiles(y0_ref, F32) + gate[:, 1:2] * _from_row_tiles(y1_ref, F32)
    o_ref[...] = _layer_norm(DN_ALPHA * x_ref[...] + f, g_ref[...], b_ref[...])


def _combine(x2, y_slots, gate, ln_g, ln_b, tm=512):
    n = x2.shape[0]
    row = lambda i: (i, 0)
    vec = pl.BlockSpec((1, D_MODEL), lambda i: (0, 0))
    return pl.pallas_call(
        _combine_kernel,
        grid=(n // tm,),
        in_specs=[pl.BlockSpec((tm, D_MODEL), row),
                  pl.BlockSpec((tm, ROW_TILES, 128), lambda i: (i, 0, 0)),
                  pl.BlockSpec((tm, ROW_TILES, 128), lambda i: (n // tm + i, 0, 0)),
                  pl.BlockSpec((tm, EXPERT_TOPK), row), vec, vec],
        out_specs=pl.BlockSpec((tm, D_MODEL), row),
        out_shape=jax.ShapeDtypeStruct((n, D_MODEL), F32),
        compiler_params=_cparams("parallel"),
        name="combine_ln",
    )(x2, y_slots, y_slots, gate, ln_g, ln_b)


def _moe_layer(x2, router_w, wg, wu, wd, ln_g, ln_b):
    n = x2.shape[0]
    nk = n * EXPERT_TOPK
    top_e, gate = _router(x2, router_w)
    top_e = top_e[:, :EXPERT_TOPK]
    gate = gate[:, :EXPERT_TOPK]
    flat_e = top_e.reshape(-1)
    onehot = (flat_e[:, None] == jnp.arange(N_EXPERTS, dtype=jnp.int32)[None, :]).astype(jnp.int32)
    counts = jnp.sum(onehot, axis=0)
    rank = jnp.sum((jnp.cumsum(onehot, axis=0) - onehot) * onehot, axis=1)
    padded = (counts + MOE_ROWS - 1) // MOE_ROWS * MOE_ROWS
    pad_end = jnp.cumsum(padded)
    pad_start = pad_end - padded
    dest = (pad_start[flat_e] + rank).astype(jnp.int32)
    n_blocks = -(-nk // MOE_ROWS) + N_EXPERTS
    n_rows = n_blocks * MOE_ROWS
    blk_start = jnp.arange(n_blocks, dtype=jnp.int32) * MOE_ROWS
    blk_e = jnp.sum((blk_start[:, None] >= pad_end[None, :]).astype(jnp.int32), axis=1)
    valid_end = jnp.concatenate([pad_start + counts, jnp.zeros((1,), jnp.int32)])
    n_valid = jnp.clip(valid_end[blk_e] - blk_start, 0, MOE_ROWS).astype(jnp.int32)
    blk_e = jnp.minimum(blk_e, N_EXPERTS - 1).astype(jnp.int32)
    xb, row_dst = _dispatch(x2, dest, n_rows)
    y_slots = _moe_experts(xb, blk_e, n_valid, row_dst, wg, wu, wd, nk)
    return _combine(x2, y_slots, gate, ln_g, ln_b)


def _rope_tables(t):
    half = HEAD_DIM // 2
    inv = ROPE_THETA ** (-jnp.arange(half, dtype=F32) / half)
    ang = jnp.arange(t).astype(F32)[:, None] * inv[None, :]
    cos = jnp.cos(ang)
    sin = jnp.sin(ang)
    cos_t = jnp.tile(jnp.concatenate([cos, cos], -1), (1, N_HEADS))
    sin_t = jnp.tile(jnp.concatenate([-sin, sin], -1), (1, N_HEADS))
    return cos_t, sin_t


def _pad_rows(w, start):
    out = jnp.zeros((LORA_DIM, BRANCH_DIM), F32)
    return lax.dynamic_update_slice(out, w, (start, 0)).astype(BF16)


def _mixer(x2, b, t, cos_t, sin_t, w_in, conv_w, shift_mu, decay_w0, decay_w2, aaa_a0, aaa_w2, gate_w2,
           k_k, k_a, r_k, gn_g, gn_b, w_branch, w_out, ln_g, ln_b, ffn_weights):
    slabs = [w.reshape(-1, w.shape[-1]) for w in ffn_weights]
    n_steps = b * (t // MOBA_BLOCK)
    in_kernel = [s.shape[0] % (n_steps * BF16_ROWS) == 0 for s in slabs]
    o_conv, q3, k3, vt4, kmean, rkv3, lora3, gates, *cast = _in_proj(
        x2.reshape(b, t, D_MODEL), w_in, conv_w, cos_t, sin_t,
        tuple(s for s, ok in zip(slabs, in_kernel) if ok))
    cast = iter(cast)
    ffn_bf16 = [(next(cast) if ok else s.astype(BF16)).reshape(w.shape)
                for s, ok, w in zip(slabs, in_kernel, ffn_weights)]
    o_att = _moba(q3, k3, vt4, kmean.reshape(b, t // MOBA_BLOCK, BRANCH_DIM)).reshape(b * t, BRANCH_DIM)

    row = lambda a: a.reshape(1, -1)
    p = {
        "mu_r": row(shift_mu[0:BRANCH_DIM]), "mu_k": row(shift_mu[BRANCH_DIM:2 * BRANCH_DIM]),
        "mu_v": row(shift_mu[2 * BRANCH_DIM:3 * BRANCH_DIM]), "mu_l": row(shift_mu[3 * BRANCH_DIM:]),
        "w0": row(decay_w0), "w2": _pad_rows(decay_w2, 0),
        "a0": row(aaa_a0), "a2": _pad_rows(aaa_w2, DECAY_LORA),
        "g2": _pad_rows(gate_w2, DECAY_LORA + AAA_LORA),
        "k_k": row(k_k), "k_a": row(k_a), "r_k": row(r_k), "gn_g": row(gn_g), "gn_b": row(gn_b),
    }
    o_rwkv = _rwkv(rkv3, lora3, p)

    x2 = _merge(o_conv.reshape(b * t, BRANCH_DIM), o_att, o_rwkv.reshape(b * t, BRANCH_DIM),
                gates.reshape(b * t, GATE_COLS), x2,
                w_branch.astype(BF16), w_out.astype(BF16), row(ln_g), row(ln_b))
    return x2, ffn_bf16


def kernel(x, w_in, conv_w, shift_mu, decay_w0, decay_w2, aaa_a0, aaa_w2, gate_w2, k_k, k_a, r_k, gn_g, gn_b,
           w_branch, w_out, ln1_g, ln1_b, ln2_g, ln2_b, ffn_w_gate, ffn_w_up, ffn_w_down, router_w,
           moe_w_gate, moe_w_up, moe_w_down):
    b, t, d = x.shape
    x2 = x.reshape(b * t, d)
    cos_t, sin_t = _rope_tables(t)
    row = lambda a: a.reshape(1, -1)
    w_in_bf16 = w_in[0].astype(BF16)
    for l in range(DEPTH):
        j = l // 2
        dense = l % 2 == 0
        ffn_weights = ((ffn_w_gate[j], ffn_w_up[j], ffn_w_down[j]) if dense
                       else (moe_w_gate[j], moe_w_up[j], moe_w_down[j]))
        x2, (wg, wu, wd) = _mixer(x2, b, t, cos_t, sin_t, w_in_bf16, conv_w[l], shift_mu[l], decay_w0[l],
                                  decay_w2[l], aaa_a0[l], aaa_w2[l], gate_w2[l], k_k[l], k_a[l], r_k[l],
                                  gn_g[l], gn_b[l], w_branch[l], w_out[l], ln1_g[l], ln1_b[l], ffn_weights)
        if dense:
            x2, w_in_bf16 = _ffn(x2, wg, wu, wd, row(ln2_g[l]), row(ln2_b[l]), w_in, min(l + 1, DEPTH - 1))
        else:
            x2 = _moe_layer(x2, router_w[j], wg, wu, wd, row(ln2_g[l]), row(ln2_b[l]))
            if l + 1 < DEPTH:
                w_in_bf16 = w_in[l + 1].astype(BF16)
    return x2.reshape(b, t, d)
```
